```python
import math
import jax, jax.numpy as jnp
from jax import lax
import numpy as np

D_MODEL = 1024
BATCH = 16
SEQ = 256
DEPTH = 4
DEC_BATCH = 2
DEC_SEQ = 2048
PAST_LEN = 512

GRID_W = 64
D_FOURIER = 256
N_FOURIER_GROUPS = 4
FOURIER_GROUP = D_FOURIER // N_FOURIER_GROUPS
D_HGRN = 256
N_HGRN_HEADS = 4
HGRN_HEAD = D_HGRN // N_HGRN_HEADS
HGRN_CHUNK = 32
N_Q_HEADS = 8
N_KV_HEADS = 2
GQA = N_Q_HEADS // N_KV_HEADS
HEAD_DIM = 64
D_ATTN = N_Q_HEADS * HEAD_DIM
D_KV = N_KV_HEADS * HEAD_DIM
D_MIX = D_FOURIER + D_HGRN + D_ATTN
WINDOW = 128
ATTN_BLOCK = 128
ATTN_SCALE = HEAD_DIM ** -0.5
ROPE_BASE = 10000.0
NEG_BIG = -1e30
D_FF = 2816
N_EXPERTS = 8
TOP_K = 2
D_FF_EXPERT = 3584
N_DENSE = (DEPTH + 1) // 2
N_MOE = DEPTH // 2
DEEPNORM_ALPHA = (2 * DEPTH) ** 0.25
DEEPNORM_BETA = (8 * DEPTH) ** -0.25
LN_EPS = 1e-5
RMS_EPS = 1e-6
N_MOD = 6
_IN_WIDTHS = (D_FOURIER, D_HGRN, D_HGRN, D_HGRN, D_HGRN, D_HGRN, D_ATTN, D_KV, D_KV)
D_IN = sum(_IN_WIDTHS)
IN_SPLITS = tuple(sum(_IN_WIDTHS[:i + 1]) for i in range(len(_IN_WIDTHS) - 1))

kernel_name = "hymba_fnet_hgrn2_swa_dit_step"


def layer_norm(x, gamma=None, beta=None):
    xf = x.astype(jnp.float32)
    mu = jnp.mean(xf, axis=-1, keepdims=True)
    var = jnp.mean(jnp.square(xf - mu), axis=-1, keepdims=True)
    y = (xf - mu) * lax.rsqrt(var + LN_EPS)
    if gamma is not None:
        y = y * gamma.astype(jnp.float32) + beta.astype(jnp.float32)
    return y.astype(x.dtype)


def modulation(cond, w_mod, b_mod):
    m = jax.nn.silu(cond) @ w_mod + b_mod
    return jnp.split(m[:, None, :], N_MOD, axis=-1)


def adaln(x, shift, scale):
    return layer_norm(x) * (1 + scale) + shift


def fourier_mix(u, w_fourier):
    B_, L, _ = u.shape
    ug = u.astype(jnp.float32).reshape(B_, L, N_FOURIER_GROUPS, FOURIER_GROUP)
    z = jnp.fft.fft2(ug, axes=(1, 3), norm="ortho").real
    return z.reshape(B_, L, D_FOURIER).astype(u.dtype) @ w_fourier


def hgrn_scan(q, logf, k, v, s0):
    B_, L, H, _ = q.shape
    dv = v.shape[-1]
    nc = L // HGRN_CHUNK

    def to_chunks(a):
        return a.reshape(B_, nc, HGRN_CHUNK, H, a.shape[-1]).transpose(1, 0, 3, 2, 4)

    mask = jnp.tril(jnp.ones((HGRN_CHUNK, HGRN_CHUNK), dtype=bool))[:, :, None]

    def step(S, inp):
        qc, gc, kc, vc = inp
        b = jnp.cumsum(gc, axis=2)
        o_inter = jnp.einsum('bhck,bhkv->bhcv', qc * jnp.exp(b), S)
        diff = b[:, :, :, None, :] - b[:, :, None, :, :]
        decay = jnp.where(mask, jnp.exp(jnp.where(mask, diff, 0.0)), 0.0)
        attn = jnp.einsum('bhtk,bhtsk,bhsk->bhts', qc, decay, kc)
        o = o_inter + jnp.einsum('bhts,bhsv->bhtv', attn, vc)
        b_last = b[:, :, -1:, :]
        S_new = jnp.exp(b_last[:, :, 0, :])[..., None] * S + jnp.einsum(
            'bhsk,bhsv->bhkv', kc * jnp.exp(b_last - b), vc)
        return S_new, o

    S_fin, o = lax.scan(step, s0.astype(jnp.float32),
                        (to_chunks(q), to_chunks(logf), to_chunks(k), to_chunks(v)))
    o = o.transpose(1, 0, 3, 2, 4).reshape(B_, L, H, dv)
    return o, S_fin


def hgrn_mix(q, f_fwd, f_bwd, i, g, lb_fwd, lb_bwd, gnorm_w, s0_fwd, s0_bwd):
    B_, L, _ = q.shape

    def heads(a):
        return a.astype(jnp.float32).reshape(B_, L, N_HGRN_HEADS, HGRN_HEAD)

    def forget(z, lb):
        one_minus_f = (1.0 - lb) * jax.nn.sigmoid(-z.astype(jnp.float32))
        return heads(jnp.log1p(-one_minus_f)), heads(one_minus_f)

    qh = heads(jax.nn.silu(q))
    vh = heads(i)
    lf_f, k_f = forget(f_fwd, lb_fwd)
    lf_b, k_b = forget(f_bwd, lb_bwd)
    o_f, s_f = hgrn_scan(qh, lf_f, k_f, vh, s0_fwd)
    o_b, s_b = hgrn_scan(qh[:, ::-1], lf_b[:, ::-1], k_b[:, ::-1], vh[:, ::-1], s0_bwd)
    o = o_f + o_b[:, ::-1]
    o = o * lax.rsqrt(jnp.mean(o * o, axis=-1, keepdims=True) + RMS_EPS) * gnorm_w.astype(jnp.float32)
    o = o.reshape(B_, L, D_HGRN) * jax.nn.silu(g.astype(jnp.float32))
    return o.astype(q.dtype), s_f, s_b


def rope_2d(x):
    L = x.shape[1]
    t = jnp.arange(L)
    rows = (t // GRID_W).astype(jnp.float32)
    cols = (t % GRID_W).astype(jnp.float32)
    n_freq = HEAD_DIM // 4
    inv = ROPE_BASE ** (-jnp.arange(n_freq, dtype=jnp.float32) / n_freq)

    def rot(xa, pos):
        ang = pos[:, None] * inv
        cos = jnp.cos(ang)[None, :, None, :]
        sin = jnp.sin(ang)[None, :, None, :]
        x1, x2 = xa[..., :n_freq], xa[..., n_freq:]
        return jnp.concatenate([x1 * cos - x2 * sin, x1 * sin + x2 * cos], axis=-1)

    half = HEAD_DIM // 2
    xf = x.astype(jnp.float32)
    return jnp.concatenate([rot(xf[..., :half], rows), rot(xf[..., half:], cols)], axis=-1).astype(x.dtype)


def sink_softmax(s, sink):
    sk = jnp.broadcast_to(sink.astype(jnp.float32).reshape(1, N_KV_HEADS, GQA, 1, 1), s.shape[:-1] + (1,))
    p = jax.nn.softmax(jnp.concatenate([sk, s], axis=-1), axis=-1)
    return p[..., 1:]


def context_attention(q, k, v, sink):
    B_, L = q.shape[:2]
    nb = L // ATTN_BLOCK
    qb = q.reshape(B_, nb, ATTN_BLOCK, N_KV_HEADS, GQA, HEAD_DIM).transpose(1, 0, 2, 3, 4, 5)

    def blk(qi):
        s = jnp.einsum('bqhgd,bkhd->bhgqk', qi, k).astype(jnp.float32) * ATTN_SCALE
        p = sink_softmax(s, sink).astype(v.dtype)
        return jnp.einsum('bhgqk,bkhd->bqhgd', p, v)

    o = lax.map(blk, qb)
    return o.transpose(1, 0, 2, 3, 4, 5).reshape(B_, L, D_ATTN)


def latent_attention(q, k, v, k_ctx, v_ctx, sink):
    B_, L = q.shape[:2]
    nb = L // ATTN_BLOCK
    band = 3 * ATTN_BLOCK
    pad = ((0, 0), (ATTN_BLOCK, ATTN_BLOCK), (0, 0), (0, 0))
    kp = jnp.pad(k, pad)
    vp = jnp.pad(v, pad)
    qg = q.reshape(B_, L, N_KV_HEADS, GQA, HEAD_DIM)
    rel = jnp.arange(ATTN_BLOCK)[:, None] - (jnp.arange(band)[None, :] - ATTN_BLOCK)

    def blk(j):
        start = j * ATTN_BLOCK
        qi = lax.dynamic_slice_in_dim(qg, start, ATTN_BLOCK, axis=1)
        ki = lax.dynamic_slice_in_dim(kp, start, band, axis=1)
        vi = lax.dynamic_slice_in_dim(vp, start, band, axis=1)
        s_pos = start - ATTN_BLOCK + jnp.arange(band)
        valid = (jnp.abs(rel) <= WINDOW) & ((s_pos >= 0) & (s_pos < L))[None, :]
        s_loc = jnp.einsum('bqhgd,bkhd->bhgqk', qi, ki).astype(jnp.float32) * ATTN_SCALE
        s_loc = jnp.where(valid, s_loc, NEG_BIG)
        s_ctx = jnp.einsum('bqhgd,bkhd->bhgqk', qi, k_ctx).astype(jnp.float32) * ATTN_SCALE
        p = sink_softmax(jnp.concatenate([s_loc, s_ctx], axis=-1), sink).astype(v.dtype)
        return (jnp.einsum('bhgqk,bkhd->bqhgd', p[..., :band], vi)
                + jnp.einsum('bhgqk,bkhd->bqhgd', p[..., band:], v_ctx))

    o = lax.map(blk, jnp.arange(nb))
    return o.transpose(1, 0, 2, 3, 4, 5).reshape(B_, L, D_ATTN)


def token_mixer(h, w_in, w_fourier, lb_fwd, lb_bwd, gnorm_w, sink, w_out, ctx):
    B_, L, _ = h.shape
    u, hq, hff, hfb, hi, hg, aq, ak, av = jnp.split(h @ w_in, IN_SPLITS, axis=-1)
    out_a = fourier_mix(u, w_fourier)
    aq = aq.reshape(B_, L, N_Q_HEADS, HEAD_DIM)
    ak = ak.reshape(B_, L, N_KV_HEADS, HEAD_DIM)
    av = av.reshape(B_, L, N_KV_HEADS, HEAD_DIM)
    if ctx is None:
        s0 = jnp.zeros((B_, N_HGRN_HEADS, HGRN_HEAD, HGRN_HEAD), jnp.float32)
        out_b, s_f, s_b = hgrn_mix(hq, hff, hfb, hi, hg, lb_fwd, lb_bwd, gnorm_w, s0, s0)
        out_c = context_attention(aq, ak, av, sink)
        new_ctx = (ak, av, jnp.stack([s_f, s_b], axis=1).astype(h.dtype))
    else:
        k_ctx, v_ctx, s0_f, s0_b = ctx
        out_b, _, _ = hgrn_mix(hq, hff, hfb, hi, hg, lb_fwd, lb_bwd, gnorm_w, s0_f, s0_b)
        out_c = latent_attention(rope_2d(aq), rope_2d(ak), av, k_ctx, v_ctx, sink)
        new_ctx = None
    y = jnp.concatenate([out_a, out_b, out_c], axis=-1) @ w_out
    return y, new_ctx


def swiglu(t, w_gate, w_up, w_down):
    return (jax.nn.silu(t @ w_gate) * (t @ w_up)) @ w_down


def moe_swiglu(h, w_router, b_router, w_gate, w_up, w_down):
    B_, L, D = h.shape
    t = h.reshape(B_ * L, D)
    logits = (t @ w_router + b_router).astype(jnp.float32)
    top_val, top_idx = lax.top_k(logits, TOP_K)
    top_p = jax.nn.softmax(top_val, axis=-1)
    gates = jnp.sum(jax.nn.one_hot(top_idx, N_EXPERTS, dtype=jnp.float32) * top_p[..., None], axis=1)
    out = jnp.zeros_like(t)
    for e in range(N_EXPERTS):
        out = out + gates[:, e:e + 1].astype(t.dtype) * swiglu(t, w_gate[e], w_up[e], w_down[e])
    return out.reshape(B_, L, D)


def setup_inputs(seed: int = 0) -> dict:
    key = jax.random.key(seed)
    ks = iter(jax.random.split(key, 40))

    def nrm(shape, scale):
        return scale * jax.random.normal(next(ks), shape, jnp.float32)

    return {
        "x_prompt": nrm((BATCH, SEQ, D_MODEL), 1.0),
        "x_sample": nrm((DEC_BATCH, DEC_SEQ, D_MODEL), 1.0),
        "c": nrm((DEC_BATCH, D_MODEL), 1.0),
        "cache_k": nrm((DEC_BATCH, DEPTH, PAST_LEN, N_KV_HEADS, HEAD_DIM), 1.0),
        "cache_v": nrm((DEC_BATCH, DEPTH, PAST_LEN, N_KV_HEADS, HEAD_DIM), 1.0),
        "state_hgrn": nrm((DEC_BATCH, DEPTH, 2, N_HGRN_HEADS, HGRN_HEAD, HGRN_HEAD), 0.5),
        "c_ctx": nrm((D_MODEL,), 1.0),
        "w_mod": nrm((DEPTH, D_MODEL, N_MOD * D_MODEL), 0.5 * D_MODEL ** -0.5),
        "b_mod": nrm((DEPTH, N_MOD * D_MODEL), 0.02),
        "w_in": nrm((DEPTH, D_MODEL, D_IN), D_MODEL ** -0.5),
        "w_fourier": nrm((DEPTH, D_FOURIER, D_FOURIER), D_FOURIER ** -0.5),
        "lb_logits": nrm((DEPTH, 2 * D_HGRN), 0.5),
        "hgrn_norm": 1.0 + nrm((DEPTH, HGRN_HEAD), 0.02),
        "attn_sink": nrm((DEPTH, N_Q_HEADS), 0.5),
        "w_out": nrm((DEPTH, D_MIX, D_MODEL), DEEPNORM_BETA * D_MIX ** -0.5),
        "ln1_g": 1.0 + nrm((DEPTH, D_MODEL), 0.02),
        "ln1_b": nrm((DEPTH, D_MODEL), 0.02),
        "ln2_g": 1.0 + nrm((DEPTH, D_MODEL), 0.02),
        "ln2_b": nrm((DEPTH, D_MODEL), 0.02),
        "ffn_w_gate": nrm((N_DENSE, D_MODEL, D_FF), D_MODEL ** -0.5),
        "ffn_w_up": nrm((N_DENSE, D_MODEL, D_FF), D_MODEL ** -0.5),
        "ffn_w_down": nrm((N_DENSE, D_FF, D_MODEL), DEEPNORM_BETA * D_FF ** -0.5),
        "router_w": nrm((N_MOE, D_MODEL, N_EXPERTS), D_MODEL ** -0.5),
        "router_b": nrm((N_MOE, N_EXPERTS), 0.01),
        "moe_w_gate": nrm((N_MOE, N_EXPERTS, D_MODEL, D_FF_EXPERT), D_MODEL ** -0.5),
        "moe_w_up": nrm((N_MOE, N_EXPERTS, D_MODEL, D_FF_EXPERT), D_MODEL ** -0.5),
        "moe_w_down": nrm((N_MOE, N_EXPERTS, D_FF_EXPERT, D_MODEL), DEEPNORM_BETA * D_FF_EXPERT ** -0.5),
    }


def reference(x_prompt, x_sample, c, cache_k, cache_v, state_hgrn, c_ctx, w_mod, b_mod, w_in,
              w_fourier, lb_logits, hgrn_norm, attn_sink, w_out, ln1_g, ln1_b, ln2_g, ln2_b,
              ffn_w_gate, ffn_w_up, ffn_w_down, router_w, router_b, moe_w_gate, moe_w_up, moe_w_down):
    lb_sm = jax.nn.softmax(lb_logits.astype(jnp.float32), axis=0)
    lower_bounds = jnp.clip(jnp.cumsum(lb_sm, axis=0) - lb_sm[0], 0.0, 1.0)
    xp, xs = x_prompt, x_sample
    new_k, new_v, new_s = [], [], []
    for l in range(DEPTH):
        sh1_p, sc1_p, g1_p, sh2_p, sc2_p, g2_p = modulation(c_ctx[None, :], w_mod[l], b_mod[l])
        sh1_s, sc1_s, g1_s, sh2_s, sc2_s, g2_s = modulation(c, w_mod[l], b_mod[l])
        mixer_w = (w_in[l], w_fourier[l], lower_bounds[l, :D_HGRN], lower_bounds[l, D_HGRN:],
                   hgrn_norm[l], attn_sink[l], w_out[l])
        yp, (kc, vc, sc) = token_mixer(adaln(xp, sh1_p, sc1_p), *mixer_w, None)
        new_k.append(kc)
        new_v.append(vc)
        new_s.append(sc)
        xp = layer_norm(DEEPNORM_ALPHA * xp + g1_p * yp, ln1_g[l], ln1_b[l])
        ctx = (cache_k[:, l], cache_v[:, l], state_hgrn[:, l, 0], state_hgrn[:, l, 1])
        ys, _ = token_mixer(adaln(xs, sh1_s, sc1_s), *mixer_w, ctx)
        xs = layer_norm(DEEPNORM_ALPHA * xs + g1_s * ys, ln1_g[l], ln1_b[l])
        hp = adaln(xp, sh2_p, sc2_p)
        hs = adaln(xs, sh2_s, sc2_s)
        i = l // 2
        if l % 2 == 0:
            fp = swiglu(hp, ffn_w_gate[i], ffn_w_up[i], ffn_w_down[i])
            fs = swiglu(hs, ffn_w_gate[i], ffn_w_up[i], ffn_w_down[i])
        else:
            fp = moe_swiglu(hp, router_w[i], router_b[i], moe_w_gate[i], moe_w_up[i], moe_w_down[i])
            fs = moe_swiglu(hs, router_w[i], router_b[i], moe_w_gate[i], moe_w_up[i], moe_w_down[i])
        xp = layer_norm(DEEPNORM_ALPHA * xp + g2_p * fp, ln2_g[l], ln2_b[l])
        xs = layer_norm(DEEPNORM_ALPHA * xs + g2_s * fs, ln2_g[l], ln2_b[l])
    new_cache_k = jnp.stack(new_k, axis=1)
    new_cache_v = jnp.stack(new_v, axis=1)
    new_state_hgrn = jnp.stack(new_s, axis=1)
    return (xp, xs, new_cache_k, new_cache_v, new_state_hgrn)
```

```python
import functools
import math

import jax
import jax.numpy as jnp
import numpy as np
from jax import lax
from jax.experimental import pallas as pl
from jax.experimental.pallas import tpu as pltpu

D_MODEL = 1024
BATCH = 16
SEQ = 256
DEPTH = 4
DEC_BATCH = 2
DEC_SEQ = 2048
PAST_LEN = 512
GRID_W = 64
D_FOURIER = 256
N_FOURIER_GROUPS = 4
FOURIER_GROUP = D_FOURIER // N_FOURIER_GROUPS
D_HGRN = 256
N_HGRN_HEADS = 4
HGRN_HEAD = D_HGRN // N_HGRN_HEADS
HGRN_CHUNK = 32
N_Q_HEADS = 8
N_KV_HEADS = 2
GQA = N_Q_HEADS // N_KV_HEADS
HEAD_DIM = 64
D_ATTN = N_Q_HEADS * HEAD_DIM
D_KV = N_KV_HEADS * HEAD_DIM
D_MIX = D_FOURIER + D_HGRN + D_ATTN
WINDOW = 128
ATTN_BLOCK = 128
ATTN_SCALE = HEAD_DIM ** -0.5
ROPE_BASE = 10000.0
NEG_BIG = -1e30
D_FF = 2816
N_EXPERTS = 8
TOP_K = 2
D_FF_EXPERT = 3584
DEEPNORM_ALPHA = (2 * DEPTH) ** 0.25
LN_EPS = 1e-5
RMS_EPS = 1e-6
N_MOD = 6
D_IN = 6 * 256 + D_ATTN + 2 * D_KV

TP = BATCH * SEQ
TS = DEC_BATCH * DEC_SEQ
T = TP + TS
N_COND = 1 + DEC_BATCH
COND_PAD = 8

TM = 512
TM_FFN = 512
TM_LN = 256
GATHER_ROWS = 512
LANES = 128
VMEM_LIMIT = 56 * 1024 * 1024

F32 = jnp.float32
BF16 = jnp.bfloat16


def _cparams(sem, vmem=VMEM_LIMIT):
    return pltpu.CompilerParams(dimension_semantics=sem, vmem_limit_bytes=vmem)


def _ln(x):
    mu = jnp.mean(x, axis=-1, keepdims=True)
    xc = x - mu
    var = jnp.mean(xc * xc, axis=-1, keepdims=True)
    return xc * lax.rsqrt(var + LN_EPS)


def _silu(x):
    return x * jax.nn.sigmoid(x)


def _split3(a):
    p0 = a.astype(BF16)
    r1 = a - p0.astype(F32)
    p1 = r1.astype(BF16)
    r2 = r1 - p1.astype(F32)
    return p0, p1, r2.astype(BF16)


def _split2(a):
    hi = a.astype(BF16)
    return hi, (a - hi.astype(F32)).astype(BF16)


def _dot(a, b):
    return jnp.dot(a, b, preferred_element_type=F32)


def _dot_nt(a, b):
    return lax.dot_general(a, b, (((1,), (1,)), ((), ())), preferred_element_type=F32)


def _dot_tn(a, b):
    return lax.dot_general(a, b, (((0,), (0,)), ((), ())), preferred_element_type=F32)


def _cond_of_tile(i, tm):
    n_ctx = TP // tm
    return jnp.where(i < n_ctx, 0, 1 + (i - n_ctx) // (DEC_SEQ // tm))


def _mod_kernel(c_ref, w_ref, b_ref, o_ref):
    a = _silu(c_ref[...]).astype(BF16)
    o_ref[...] = _dot(a, w_ref[...].astype(BF16)) + b_ref[...]


def _modulation(cond, w_mod, b_mod):
    tn = 2048
    n_out = N_MOD * D_MODEL
    return pl.pallas_call(
        _mod_kernel,
        grid=(DEPTH, n_out // tn),
        in_specs=[
            pl.BlockSpec((COND_PAD, D_MODEL), lambda l, j: (0, 0)),
            pl.BlockSpec((None, D_MODEL, tn), lambda l, j: (l, 0, j)),
            pl.BlockSpec((None, 1, tn), lambda l, j: (l, 0, j)),
        ],
        out_specs=pl.BlockSpec((None, COND_PAD, tn), lambda l, j: (l, 0, j)),
        out_shape=jax.ShapeDtypeStruct((DEPTH, COND_PAD, n_out), F32),
        compiler_params=_cparams(("parallel", "parallel")),
        name="modulation",
    )(cond, w_mod, b_mod.reshape(DEPTH, 1, n_out))


def _inproj_kernel(x_ref, mod_ref, w_ref, o_ref, wb_ref):
    @pl.when(pl.program_id(0) == 0)
    def _():
        wb_ref[...] = w_ref[...].astype(BF16)

    h = _ln(x_ref[...]) * (1.0 + mod_ref[1:2, :]) + mod_ref[0:1, :]
    o_ref[...] = _dot(h.astype(BF16), wb_ref[...])


def _in_projection(x, mod_l, w_in, l):
    return pl.pallas_call(
        _inproj_kernel,
        grid=(T // TM,),
        in_specs=[
            pl.BlockSpec((TM, D_MODEL), lambda i: (i, 0)),
            pl.BlockSpec((None, N_MOD, D_MODEL), lambda i: (_cond_of_tile(i, TM), 0, 0)),
            pl.BlockSpec((None, D_MODEL, D_IN), lambda i: (l, 0, 0), pipeline_mode=pl.Buffered(1)),
        ],
        out_specs=pl.BlockSpec((TM, D_IN), lambda i: (i, 0)),
        out_shape=jax.ShapeDtypeStruct((T, D_IN), F32),
        scratch_shapes=[pltpu.VMEM((D_MODEL, D_IN), BF16)],
        compiler_params=_cparams(("arbitrary",)),
        name="in_projection",
    )(x, mod_l, w_in)


def _fourier_kernel(u_ref, ch_hi_ref, ch_lo_ref, tab_hi_ref, tab_lo_ref, wf_ref, o_ref,
                    ab_hi_ref, ab_lo_ref, *, scale):
    @pl.when(pl.program_id(1) == 0)
    def _():
        uh, ul = _split2(u_ref[...])
        ch_hi = ch_hi_ref[...]
        ab = _dot(uh, ch_hi) + _dot(uh, ch_lo_ref[...]) + _dot(ul, ch_hi)
        stacked = jnp.concatenate([ab[:, :D_FOURIER], ab[:, D_FOURIER:]], axis=0)
        hi, lo = _split2(stacked)
        ab_hi_ref[...] = hi
        ab_lo_ref[...] = lo

    th = tab_hi_ref[...]
    ab_hi = ab_hi_ref[...]
    z = _dot(th, ab_hi) + _dot(th, ab_lo_ref[...]) + _dot(tab_lo_ref[...], ab_hi)
    z = (z * scale).astype(BF16)
    o_ref[...] = _dot(z, wf_ref[...].astype(BF16)).astype(BF16)


def _dft_tables(n):
    a = jnp.arange(n, dtype=jnp.int32)
    prod = (a[:, None] * a[None, :]) % n
    ang = prod.astype(F32) * (2.0 * math.pi / n)
    tab = jnp.concatenate([jnp.cos(ang), -jnp.sin(ang)], axis=1)
    return _split2(tab)


def _channel_tables():
    a = jnp.arange(D_FOURIER, dtype=jnp.int32)
    same = (a[:, None] // FOURIER_GROUP) == (a[None, :] // FOURIER_GROUP)
    prod = ((a[:, None] % FOURIER_GROUP) * (a[None, :] % FOURIER_GROUP)) % FOURIER_GROUP
    ang = prod.astype(F32) * (2.0 * math.pi / FOURIER_GROUP)
    c = jnp.where(same, jnp.cos(ang), 0.0)
    s = jnp.where(same, jnp.sin(ang), 0.0)
    return _split2(jnp.concatenate([c, s], axis=1))


def _fourier_mix(y_in, row0, nbatch, length, ch_tabs, pos_tabs, w_fourier, l):
    tr = min(length, 256)
    blk0 = row0 // length
    kern = functools.partial(_fourier_kernel, scale=1.0 / math.sqrt(length * FOURIER_GROUP))
    return pl.pallas_call(
        kern,
        grid=(nbatch, length // tr),
        in_specs=[
            pl.BlockSpec((length, D_FOURIER), lambda b, r: (blk0 + b, 0)),
            pl.BlockSpec((D_FOURIER, 2 * D_FOURIER), lambda b, r: (0, 0)),
            pl.BlockSpec((D_FOURIER, 2 * D_FOURIER), lambda b, r: (0, 0)),
            pl.BlockSpec((tr, 2 * length), lambda b, r: (r, 0)),
            pl.BlockSpec((tr, 2 * length), lambda b, r: (r, 0)),
            pl.BlockSpec((None, D_FOURIER, D_FOURIER), lambda b, r: (l, 0, 0)),
        ],
        out_specs=pl.BlockSpec((tr, D_FOURIER), lambda b, r: (b * (length // tr) + r, 0)),
        out_shape=jax.ShapeDtypeStruct((nbatch * length, D_FOURIER), BF16),
        scratch_shapes=[pltpu.VMEM((2 * length, D_FOURIER), BF16),
                        pltpu.VMEM((2 * length, D_FOURIER), BF16)],
        compiler_params=_cparams(("parallel", "arbitrary")),
        name=f"fourier_mix_{length}",
    )(y_in, ch_tabs[0], ch_tabs[1], pos_tabs[0], pos_tabs[1], w_fourier)


def _hgrn_kernel(hq_ref, hff_ref, hfb_ref, hi_ref, hg_ref, lb_ref, gw_ref, s0_ref,
                 o_ref, sfin_ref, q_scr, g_scr, k_scr, o_scr, st_scr, *, length):
    c = HGRN_CHUNK
    nc = length // c
    q_scr[...] = _silu(hq_ref[...])

    row = lax.broadcasted_iota(jnp.int32, (c, c), 0)
    col = lax.broadcasted_iota(jnp.int32, (c, c), 1)
    r256 = lax.broadcasted_iota(jnp.int32, (D_HGRN, D_HGRN), 0) // HGRN_HEAD
    c256 = lax.broadcasted_iota(jnp.int32, (D_HGRN, D_HGRN), 1) // HGRN_HEAD
    head_mask = r256 == c256
    ones_bd = jnp.where(head_mask, 1.0, 0.0).astype(BF16)

    for d in range(2):
        z_ref = hff_ref if d == 0 else hfb_ref
        one_minus_f = (1.0 - lb_ref[d:d + 1, :]) * jax.nn.sigmoid(-z_ref[...])
        g_scr[...] = jnp.log1p(-one_minus_f)
        k_scr[...] = one_minus_f
        st_scr[...] = s0_ref[d]
        last = c - 1 if d == 0 else 0

        def chunk(ci, carry, d=d, last=last):
            tri = jnp.where((col <= row) if d == 0 else (col >= row), 1.0, 0.0).astype(BF16)
            s_idx = lax.broadcasted_iota(jnp.int32, (c, c, D_HGRN), 0)
            t_idx = lax.broadcasted_iota(jnp.int32, (c, c, D_HGRN), 1)
            causal = (s_idx <= t_idx) if d == 0 else (s_idx >= t_idx)
            cc = ci if d == 0 else nc - 1 - ci
            rows = pl.ds(pl.multiple_of(cc * c, c), c)
            g = g_scr[rows, :]
            kk = k_scr[rows, :]
            q = q_scr[rows, :]
            v = hi_ref[rows, :]
            g0, g1, g2 = _split3(g)
            b = _dot(tri, g0) + _dot(tri, g1) + _dot(tri, g2)
            btot = b[last:last + 1, :]
            st = st_scr[...]
            qd = q * jnp.exp(b)
            o_inter = _dot_nt(qd.astype(BF16), st.astype(BF16))
            diff = b[None, :, :] - b[:, None, :]
            pair = q[None, :, :] * jnp.exp(jnp.minimum(diff, 0.0)) * kk[:, None, :]
            pair = jnp.where(causal, pair, 0.0).astype(BF16)
            attn = _dot(pair.reshape(c * c, D_HGRN), ones_bd).reshape(c, c, D_HGRN)
            o_intra = jnp.sum(attn * v[:, None, :], axis=0)
            o = o_inter + o_intra
            if d == 0:
                o_scr[rows, :] = o
            else:
                o_scr[rows, :] = o_scr[rows, :] + o
            kd = kk * jnp.exp(btot - b)
            upd = _dot_tn(v.astype(BF16), kd.astype(BF16))
            st_scr[...] = jnp.exp(btot) * st + jnp.where(head_mask, upd, 0.0)
            return carry

        lax.fori_loop(0, nc, chunk, 0)
        sfin_ref[d] = st_scr[...]

    o = o_scr[...]
    s0p, s1p, s2p = _split3(o * o)
    ms = (_dot(s0p, ones_bd) + _dot(s1p, ones_bd) + _dot(s2p, ones_bd)) * (1.0 / HGRN_HEAD)
    y = o * lax.rsqrt(ms + RMS_EPS) * gw_ref[...] * _silu(hg_ref[...])
    o_ref[...] = y.astype(BF16)


def _hgrn_mix(y_in, row0, nbatch, length, lb2, gw, s0):
    blk0 = row0 // length
    col = lambda j: (lambda b: (blk0 + b, j))
    kern = functools.partial(_hgrn_kernel, length=length)
    return pl.pallas_call(
        kern,
        grid=(nbatch,),
        in_specs=[
            pl.BlockSpec((length, D_HGRN), col(1)),
            pl.BlockSpec((length, D_HGRN), col(2)),
            pl.BlockSpec((length, D_HGRN), col(3)),
            pl.BlockSpec((length, D_HGRN), col(4)),
            pl.BlockSpec((length, D_HGRN), col(5)),
            pl.BlockSpec((2, D_HGRN), lambda b: (0, 0)),
            pl.BlockSpec((1, D_HGRN), lambda b: (0, 0)),
            pl.BlockSpec((None, 2, D_HGRN, D_HGRN), lambda b: (b, 0, 0, 0)),
        ],
        out_specs=[
            pl.BlockSpec((length, D_HGRN), lambda b: (b, 0)),
            pl.BlockSpec((None, 2, D_HGRN, D_HGRN), lambda b: (b, 0, 0, 0)),
        ],
        out_shape=[
            jax.ShapeDtypeStruct((nbatch * length, D_HGRN), BF16),
            jax.ShapeDtypeStruct((nbatch, 2, D_HGRN, D_HGRN), F32),
        ],
        scratch_shapes=[pltpu.VMEM((length, D_HGRN), F32) for _ in range(4)]
        + [pltpu.VMEM((D_HGRN, D_HGRN), F32)],
        compiler_params=_cparams(("parallel",)),
        name=f"hgrn_mix_{length}",
    )(y_in, y_in, y_in, y_in, y_in, lb2, gw, s0)


def _states_to_kernel_layout(s):
    b = s.shape[0]
    st = jnp.swapaxes(s, -1, -2)
    eye = jnp.eye(N_HGRN_HEADS, dtype=s.dtype)
    full = jnp.einsum("bdhvk,hg->bdhvgk", st, eye)
    return full.reshape(b, 2, D_HGRN, D_HGRN)


def _states_from_kernel_layout(st):
    b = st.shape[0]
    full = st.reshape(b, 2, N_HGRN_HEADS, HGRN_HEAD, N_HGRN_HEADS, HGRN_HEAD)
    diag = jnp.stack([full[:, :, h, :, h, :] for h in range(N_HGRN_HEADS)], axis=2)
    return jnp.swapaxes(diag, -1, -2)


def _sink_softmax_pv(parts, sink):
    m = sink
    for s, _ in parts:
        m = jnp.maximum(m, jnp.max(s, axis=1, keepdims=True))
    den = jnp.exp(sink - m)
    es = []
    for s, _ in parts:
        e = jnp.exp(s - m)
        den = den + jnp.sum(e, axis=1, keepdims=True)
        es.append(e)
    inv = 1.0 / den
    out = None
    for e, (_, v) in zip(es, parts):
        o = _dot((e * inv).astype(BF16), v)
        out = o if out is None else out + o
    return out


def _ctx_attn_kernel(sink_ref, q_ref, k_ref, v_ref, o_ref):
    q = q_ref[...]
    k = k_ref[...]
    v = v_ref[...]
    outs = []
    for h in range(N_Q_HEADS):
        g = h // GQA
        kg = k[:, g * HEAD_DIM:(g + 1) * HEAD_DIM].astype(BF16)
        vg = v[:, g * HEAD_DIM:(g + 1) * HEAD_DIM].astype(BF16)
        qh = q[:, h * HEAD_DIM:(h + 1) * HEAD_DIM].astype(BF16)
        s = _dot_nt(qh, kg) * ATTN_SCALE
        outs.append(_sink_softmax_pv([(s, vg)], sink_ref[h]))
    o_ref[...] = jnp.concatenate(outs, axis=1).astype(BF16)


def _context_attention(y_in, sink_l):
    qcol = (6 * 256) // D_ATTN
    kcol = (6 * 256 + D_ATTN) // D_KV
    return pl.pallas_call(
        _ctx_attn_kernel,
        grid=(BATCH,),
        in_specs=[
            pl.BlockSpec(memory_space=pltpu.SMEM),
            pl.BlockSpec((SEQ, D_ATTN), lambda b: (b, qcol)),
            pl.BlockSpec((SEQ, D_KV), lambda b: (b, kcol)),
            pl.BlockSpec((SEQ, D_KV), lambda b: (b, kcol + 1)),
        ],
        out_specs=pl.BlockSpec((SEQ, D_ATTN), lambda b: (b, 0)),
        out_shape=jax.ShapeDtypeStruct((TP, D_ATTN), BF16),
        compiler_params=_cparams(("parallel",)),
        name="context_attention",
    )(sink_l, y_in, y_in, y_in)


def _rope(x, cos, sin):
    lane = lax.broadcasted_iota(jnp.int32, x.shape, 1)
    n_freq = HEAD_DIM // 4
    first = (lane % (2 * n_freq)) < n_freq
    swapped = jnp.where(first, pltpu.roll(x, LANES - n_freq, axis=1), pltpu.roll(x, n_freq, axis=1))
    return x * cos + swapped * sin


def _lat_attn_kernel(sink_ref, q_ref, k_ref, v_ref, kc_ref, vc_ref, cos_ref, sin_ref, o_ref):
    j = pl.program_id(1)
    nb = DEC_SEQ // ATTN_BLOCK
    blk = ATTN_BLOCK
    q0 = pl.multiple_of(j * blk, blk)
    cos_q = cos_ref[pl.ds(q0, blk), :]
    sin_q = sin_ref[pl.ds(q0, blk), :]

    starts = [jnp.maximum(j - 1, 0), j, jnp.minimum(j + 1, nb - 1)]
    k_band, v_band = [], []
    for st in starts:
        r0 = pl.multiple_of(st * blk, blk)
        kb = _rope(k_ref[pl.ds(r0, blk), :], cos_ref[pl.ds(r0, blk), :], sin_ref[pl.ds(r0, blk), :])
        k_band.append(kb)
        v_band.append(v_ref[pl.ds(r0, blk), :])
    k_loc = jnp.concatenate(k_band, axis=0)
    v_loc = jnp.concatenate(v_band, axis=0)
    r = lax.broadcasted_iota(jnp.int32, (blk, 3 * blk), 0)
    cidx = lax.broadcasted_iota(jnp.int32, (blk, 3 * blk), 1)
    s_pos = (j - 1) * blk + cidx
    valid = (cidx >= r) & (cidx <= r + 2 * WINDOW) & (s_pos >= 0) & (s_pos < DEC_SEQ)
    kc = kc_ref[...]
    vc = vc_ref[...]

    heads_per_chunk = LANES // HEAD_DIM
    q_chunks = [_rope(q_ref[:, cg * LANES:(cg + 1) * LANES], cos_q, sin_q)
                for cg in range(N_Q_HEADS // heads_per_chunk)]
    outs = []
    for h in range(N_Q_HEADS):
        g = h // GQA
        off = (h % heads_per_chunk) * HEAD_DIM
        qh = q_chunks[h // heads_per_chunk][:, off:off + HEAD_DIM].astype(BF16)
        sl = slice(g * HEAD_DIM, (g + 1) * HEAD_DIM)
        s_loc = _dot_nt(qh, k_loc[:, sl].astype(BF16)) * ATTN_SCALE
        s_loc = jnp.where(valid, s_loc, NEG_BIG)
        s_ctx = _dot_nt(qh, kc[:, sl].astype(BF16)) * ATTN_SCALE
        outs.append(_sink_softmax_pv(
            [(s_loc, v_loc[:, sl].astype(BF16)), (s_ctx, vc[:, sl].astype(BF16))], sink_ref[h]))
    o_ref[...] = jnp.concatenate(outs, axis=1).astype(BF16)


def _latent_attention(y_in, sink_l, kc, vc, l, cos_t, sin_t):
    nb = DEC_SEQ // ATTN_BLOCK
    qrow0 = TP // ATTN_BLOCK
    krow0 = TP // DEC_SEQ
    qcol = (6 * 256) // D_ATTN
    kcol = (6 * 256 + D_ATTN) // D_KV
    return pl.pallas_call(
        _lat_attn_kernel,
        grid=(DEC_BATCH, nb),
        in_specs=[
            pl.BlockSpec(memory_space=pltpu.SMEM),
            pl.BlockSpec((ATTN_BLOCK, D_ATTN), lambda b, j: (qrow0 + b * nb + j, qcol)),
            pl.BlockSpec((DEC_SEQ, D_KV), lambda b, j: (krow0 + b, kcol)),
            pl.BlockSpec((DEC_SEQ, D_KV), lambda b, j: (krow0 + b, kcol + 1)),
            pl.BlockSpec((None, None, PAST_LEN, D_KV), lambda b, j: (b, l, 0, 0)),
            pl.BlockSpec((None, None, PAST_LEN, D_KV), lambda b, j: (b, l, 0, 0)),
            pl.BlockSpec((DEC_SEQ, LANES), lambda b, j: (0, 0)),
            pl.BlockSpec((DEC_SEQ, LANES), lambda b, j: (0, 0)),
        ],
        out_specs=pl.BlockSpec((ATTN_BLOCK, D_ATTN), lambda b, j: (b * nb + j, 0)),
        out_shape=jax.ShapeDtypeStruct((TS, D_ATTN), BF16),
        compiler_params=_cparams(("parallel", "parallel")),
        name="latent_attention",
    )(sink_l, y_in, y_in, y_in, kc, vc, cos_t, sin_t)


def _rope_tables():
    t = jnp.arange(DEC_SEQ)
    rows = (t // GRID_W).astype(F32)
    cols = (t % GRID_W).astype(F32)
    n_freq = HEAD_DIM // 4
    inv = ROPE_BASE ** (-jnp.arange(n_freq, dtype=F32) / n_freq)
    ar = rows[:, None] * inv
    ac = cols[:, None] * inv
    cos = jnp.concatenate([jnp.cos(ar), jnp.cos(ar), jnp.cos(ac), jnp.cos(ac)], axis=1)
    sin = jnp.concatenate([-jnp.sin(ar), jnp.sin(ar), -jnp.sin(ac), jnp.sin(ac)], axis=1)
    return jnp.tile(cos, (1, N_KV_HEADS)), jnp.tile(sin, (1, N_KV_HEADS))


def _outproj_kernel(a_ref, b_ref, c_ref, x_ref, mod_ref, g_ref, beta_ref, w_ref,
                    x1_ref, h2_ref, wb_ref):
    @pl.when(pl.program_id(0) == 0)
    def _():
        wb_ref[...] = w_ref[...].astype(BF16)

    y = (_dot(a_ref[...], wb_ref[0:D_FOURIER, :])
         + _dot(b_ref[...], wb_ref[D_FOURIER:D_FOURIER + D_HGRN, :])
         + _dot(c_ref[...], wb_ref[D_FOURIER + D_HGRN:D_MIX, :]))
    x1 = _ln(DEEPNORM_ALPHA * x_ref[...] + mod_ref[2:3, :] * y) * g_ref[...] + beta_ref[...]
    x1_ref[...] = x1
    h2_ref[...] = _ln(x1) * (1.0 + mod_ref[4:5, :]) + mod_ref[3:4, :]


def _out_projection(out_a, out_b, out_c, x, mod_l, ln_g, ln_b, w_out, l):
    row = lambda w: pl.BlockSpec((TM, w), lambda i: (i, 0))
    vec = pl.BlockSpec((1, D_MODEL), lambda i: (0, 0))
    return pl.pallas_call(
        _outproj_kernel,
        grid=(T // TM,),
        in_specs=[
            row(D_FOURIER), row(D_HGRN), row(D_ATTN), row(D_MODEL),
            pl.BlockSpec((None, N_MOD, D_MODEL), lambda i: (_cond_of_tile(i, TM), 0, 0)),
            vec, vec,
            pl.BlockSpec((None, D_MIX, D_MODEL), lambda i: (l, 0, 0), pipeline_mode=pl.Buffered(1)),
        ],
        out_specs=[row(D_MODEL), row(D_MODEL)],
        out_shape=[jax.ShapeDtypeStruct((T, D_MODEL), F32), jax.ShapeDtypeStruct((T, D_MODEL), F32)],
        scratch_shapes=[pltpu.VMEM((D_MIX, D_MODEL), BF16)],
        compiler_params=_cparams(("arbitrary",)),
        name="out_projection",
    )(out_a, out_b, out_c, x, mod_l, ln_g.reshape(1, D_MODEL), ln_b.reshape(1, D_MODEL), w_out)


def _expert_changed(te_ref, i):
    return (i == 0) | (te_ref[i] != te_ref[jnp.maximum(i - 1, 0)])


def _ffn_up_kernel(te_ref, nv_ref, x_ref, wg_ref, wu_ref, h_ref, wgb_ref, wub_ref):
    i = pl.program_id(1)

    @pl.when(_expert_changed(te_ref, i))
    def _():
        wgb_ref[...] = wg_ref[...].astype(BF16)
        wub_ref[...] = wu_ref[...].astype(BF16)

    @pl.when(i < nv_ref[0])
    def _():
        x = x_ref[...].astype(BF16)
        a = _dot(x, wgb_ref[...])
        b = _dot(x, wub_ref[...])
        h_ref[...] = (_silu(a) * b).astype(BF16)

    @pl.when(i >= nv_ref[0])
    def _():
        h_ref[...] = jnp.zeros_like(h_ref)


def _ffn_down_kernel(te_ref, nv_ref, h_ref, gate_ref, wd_ref, y_ref, wdb_ref):
    i = pl.program_id(1)

    @pl.when(_expert_changed(te_ref, i))
    def _():
        wdb_ref[...] = wd_ref[...].astype(BF16)

    @pl.when(i < nv_ref[0])
    def _():
        y_ref[...] = _dot(h_ref[...], wdb_ref[...]) * gate_ref[...]

    @pl.when(i >= nv_ref[0])
    def _():
        y_ref[...] = jnp.zeros_like(y_ref)


def _grouped_ffn(x_rows, row_gate, tile_expert, n_valid, w_gate, w_up, w_down, f_splits):
    r = x_rows.shape[0]
    nt = r // TM_FFN
    f = w_gate.shape[-1]
    fh = f // f_splits
    assert fh * f_splits == f and fh % LANES == 0
    dh = D_MODEL // 2
    h = pl.pallas_call(
        _ffn_up_kernel,
        grid_spec=pltpu.PrefetchScalarGridSpec(
            num_scalar_prefetch=2,
            grid=(f_splits, nt),
            in_specs=[
                pl.BlockSpec((TM_FFN, D_MODEL), lambda j, i, te, nv: (i, 0)),
                pl.BlockSpec((None, D_MODEL, fh), lambda j, i, te, nv: (te[i], 0, j)),
                pl.BlockSpec((None, D_MODEL, fh), lambda j, i, te, nv: (te[i], 0, j)),
            ],
            out_specs=pl.BlockSpec((TM_FFN, fh), lambda j, i, te, nv: (i, j)),
            scratch_shapes=[pltpu.VMEM((D_MODEL, fh), BF16), pltpu.VMEM((D_MODEL, fh), BF16)],
        ),
        out_shape=jax.ShapeDtypeStruct((r, f), BF16),
        compiler_params=_cparams(("arbitrary", "arbitrary")),
        name="ffn_up",
    )(tile_expert, n_valid, x_rows, w_gate, w_up)
    return pl.pallas_call(
        _ffn_down_kernel,
        grid_spec=pltpu.PrefetchScalarGridSpec(
            num_scalar_prefetch=2,
            grid=(2, nt),
            in_specs=[
                pl.BlockSpec((TM_FFN, f), lambda j, i, te, nv: (i, 0)),
                pl.BlockSpec((TM_FFN, 1), lambda j, i, te, nv: (i, 0)),
                pl.BlockSpec((None, f, dh), lambda j, i, te, nv: (te[i], 0, j)),
            ],
            out_specs=pl.BlockSpec((TM_FFN, dh), lambda j, i, te, nv: (i, j)),
            scratch_shapes=[pltpu.VMEM((f, dh), BF16)],
        ),
        out_shape=jax.ShapeDtypeStruct((r, D_MODEL), F32),
        compiler_params=_cparams(("arbitrary", "arbitrary")),
        name="ffn_down",
    )(tile_expert, n_valid, h, row_gate, w_down)


def _post_ffn(x1, y, g2, ln_g, ln_b):
    return _ln(DEEPNORM_ALPHA * x1 + g2 * y) * ln_g + ln_b


def _ln2_kernel(x1_ref, y_ref, mod_ref, g_ref, beta_ref, o_ref):
    o_ref[...] = _post_ffn(x1_ref[...], y_ref[...], mod_ref[5:6, :], g_ref[...], beta_ref[...])


def _post_ffn_norm(x1, y, mod_l, ln_g, ln_b):
    row = pl.BlockSpec((TM_LN, D_MODEL), lambda i: (i, 0))
    vec = pl.BlockSpec((1, D_MODEL), lambda i: (0, 0))
    return pl.pallas_call(
        _ln2_kernel,
        grid=(T // TM_LN,),
        in_specs=[row, row,
                  pl.BlockSpec((None, N_MOD, D_MODEL), lambda i: (_cond_of_tile(i, TM_LN), 0, 0)),
                  vec, vec],
        out_specs=row,
        out_shape=jax.ShapeDtypeStruct((T, D_MODEL), F32),
        compiler_params=_cparams(("parallel",)),
        name="post_ffn_norm",
    )(x1, y, mod_l, ln_g.reshape(1, D_MODEL), ln_b.reshape(1, D_MODEL))


def _router_kernel(h_ref, w_ref, b_ref, idx_ref, p_ref):
    logits = _dot(h_ref[...].astype(BF16), w_ref[...].astype(BF16)) + b_ref[...]
    lane = lax.broadcasted_iota(jnp.int32, logits.shape, 1)
    m1 = jnp.max(logits, axis=1, keepdims=True)
    i1 = jnp.min(jnp.where(logits == m1, lane, LANES), axis=1, keepdims=True)
    rest = jnp.where(lane == i1, -jnp.inf, logits)
    m2 = jnp.max(rest, axis=1, keepdims=True)
    i2 = jnp.min(jnp.where(rest == m2, lane, LANES), axis=1, keepdims=True)
    e = jnp.exp(m2 - m1)
    inv = 1.0 / (1.0 + e)
    idx_ref[...] = jnp.where(lane == 0, i1, jnp.where(lane == 1, i2, 0))
    p_ref[...] = jnp.where(lane == 0, inv, jnp.where(lane == 1, e * inv, 0.0))


def _router(h2, w_router, b_router):
    w = jnp.zeros((D_MODEL, LANES), F32).at[:, :N_EXPERTS].set(w_router)
    b = jnp.full((1, LANES), NEG_BIG, F32).at[0, :N_EXPERTS].set(b_router)
    row = lambda w_: pl.BlockSpec((TM, w_), lambda i: (i, 0))
    idx, p = pl.pallas_call(
        _router_kernel,
        grid=(T // TM,),
        in_specs=[row(D_MODEL), pl.BlockSpec((D_MODEL, LANES), lambda i: (0, 0)),
                  pl.BlockSpec((1, LANES), lambda i: (0, 0))],
        out_specs=[row(LANES), row(LANES)],
        out_shape=[jax.ShapeDtypeStruct((T, LANES), jnp.int32), jax.ShapeDtypeStruct((T, LANES), F32)],
        compiler_params=_cparams(("parallel",)),
        name="router",
    )(h2, w, b)
    return idx[:, :TOP_K], p[:, :TOP_K]


def _row_copy(src, dst, s, d, sem):
    return pltpu.make_async_copy(src.at[pl.ds(s, 1), :], dst.at[pl.ds(d, 1), :], sem)


def _gather_kernel(tok_ref, x_hbm, o_hbm, sem):
    i = pl.program_id(0)
    n = pl.num_programs(0)
    base = i * GATHER_ROWS
    slot = i % 2

    def issue(r, carry):
        _row_copy(x_hbm, o_hbm, tok_ref[base + r], base + r, sem.at[slot]).start()
        return carry

    lax.fori_loop(0, GATHER_ROWS, issue, 0, unroll=8)

    def wait_rows(s):
        pltpu.make_async_copy(x_hbm.at[pl.ds(0, GATHER_ROWS), :], o_hbm.at[pl.ds(0, GATHER_ROWS), :],
                              sem.at[s]).wait()

    @pl.when(i > 0)
    def _():
        wait_rows(1 - slot)

    @pl.when(i == n - 1)
    def _():
        wait_rows(slot)


def _dispatch_gather(h2, row_token):
    r = row_token.shape[0]
    return pl.pallas_call(
        _gather_kernel,
        grid_spec=pltpu.PrefetchScalarGridSpec(
            num_scalar_prefetch=1,
            grid=(r // GATHER_ROWS,),
            in_specs=[pl.BlockSpec(memory_space=pl.ANY)],
            out_specs=pl.BlockSpec(memory_space=pl.ANY),
            scratch_shapes=[pltpu.SemaphoreType.DMA((2,))],
        ),
        out_shape=jax.ShapeDtypeStruct((r, D_MODEL), F32),
        compiler_params=_cparams(("arbitrary",)),
        name="dispatch_gather",
    )(row_token, h2)


def _combine_kernel(pos_ref, x1_ref, y_hbm, mod_ref, g_ref, beta_ref, o_ref, buf, sem):
    i = pl.program_id(0)
    base = i * TM_LN

    def issue(r, carry):
        for k in range(TOP_K):
            _row_copy(y_hbm, buf.at[k], pos_ref[(base + r) * TOP_K + k], r, sem.at[k]).start()
        return carry

    lax.fori_loop(0, TM_LN, issue, 0, unroll=8)
    for k in range(TOP_K):
        pltpu.make_async_copy(y_hbm.at[pl.ds(0, TM_LN), :], buf.at[k], sem.at[k]).wait()
    y = buf[0]
    for k in range(1, TOP_K):
        y = y + buf[k]
    o_ref[...] = _post_ffn(x1_ref[...], y, mod_ref[5:6, :], g_ref[...], beta_ref[...])


def _combine_norm(x1, y_rows, pos, mod_l, ln_g, ln_b):
    row = pl.BlockSpec((TM_LN, D_MODEL), lambda i, p: (i, 0))
    vec = pl.BlockSpec((1, D_MODEL), lambda i, p: (0, 0))
    return pl.pallas_call(
        _combine_kernel,
        grid_spec=pltpu.PrefetchScalarGridSpec(
            num_scalar_prefetch=1,
            grid=(T // TM_LN,),
            in_specs=[row, pl.BlockSpec(memory_space=pl.ANY),
                      pl.BlockSpec((None, N_MOD, D_MODEL), lambda i, p: (_cond_of_tile(i, TM_LN), 0, 0)),
                      vec, vec],
            out_specs=row,
            scratch_shapes=[pltpu.VMEM((TOP_K, TM_LN, D_MODEL), F32), pltpu.SemaphoreType.DMA((TOP_K,))],
        ),
        out_shape=jax.ShapeDtypeStruct((T, D_MODEL), F32),
        compiler_params=_cparams(("arbitrary",)),
        name="combine_norm",
    )(pos, x1, y_rows, mod_l, ln_g.reshape(1, D_MODEL), ln_b.reshape(1, D_MODEL))


def _moe_plan(idx, p):
    n_assign = T * TOP_K
    n_tiles = n_assign // TM_FFN + N_EXPERTS
    e = idx.reshape(n_assign)
    onehot = (e[:, None] == jnp.arange(N_EXPERTS, dtype=jnp.int32)[None, :]).astype(jnp.int32)
    csum = jnp.cumsum(onehot, axis=0)
    rank = jnp.sum(csum * onehot, axis=1) - 1
    counts = csum[-1]
    tiles_e = (counts + TM_FFN - 1) // TM_FFN
    tile_end = jnp.cumsum(tiles_e)
    tile_start = tile_end - tiles_e
    dest = (tile_start * TM_FFN)[e] + rank
    token = jnp.arange(n_assign, dtype=jnp.int32) // TOP_K
    row_token = jnp.zeros((n_tiles * TM_FFN,), jnp.int32).at[dest].set(token)
    row_gate = jnp.zeros((n_tiles * TM_FFN,), F32).at[dest].set(p.reshape(n_assign))
    n_valid = tile_end[-1]
    tile_id = jnp.minimum(jnp.arange(n_tiles, dtype=jnp.int32), n_valid - 1)
    tile_expert = jnp.sum((tile_id[:, None] >= tile_end[None, :]).astype(jnp.int32), axis=1)
    return row_token, row_gate.reshape(-1, 1), tile_expert.astype(jnp.int32), n_valid.reshape(1).astype(jnp.int32), dest.astype(jnp.int32)


def _moe_ffn(h2, x1, mod_l, ln_g, ln_b, w_router, b_router, w_gate, w_up, w_down, expert0):
    idx, p = _router(h2, w_router, b_router)
    row_token, row_gate, tile_expert, n_valid, pos = _moe_plan(idx, p)
    x_rows = _dispatch_gather(h2, row_token)
    y_rows = _grouped_ffn(x_rows, row_gate, tile_expert + expert0, n_valid, w_gate, w_up, w_down, 4)
    return _combine_norm(x1, y_rows, pos, mod_l, ln_g, ln_b)


def _dense_ffn(h2, x1, mod_l, ln_g, ln_b, w_gate, w_up, w_down, index):
    nt = T // TM_FFN
    y = _grouped_ffn(h2, jnp.ones((T, 1), F32), jnp.full((nt,), index, jnp.int32),
                     jnp.full((1,), nt, jnp.int32), w_gate, w_up, w_down, 2)
    return _post_ffn_norm(x1, y, mod_l, ln_g, ln_b)


def kernel(x_prompt, x_sample, c, cache_k, cache_v, state_hgrn, c_ctx, w_mod, b_mod, w_in, w_fourier, lb_logits, hgrn_norm, attn_sink, w_out, ln1_g, ln1_b, ln2_g, ln2_b, ffn_w_gate, ffn_w_up, ffn_w_down, router_w, router_b, moe_w_gate, moe_w_up, moe_w_down):
    lb_sm = jax.nn.softmax(lb_logits.astype(F32), axis=0)
    lower_bounds = jnp.clip(jnp.cumsum(lb_sm, axis=0) - lb_sm[0], 0.0, 1.0)

    cond = jnp.zeros((COND_PAD, D_MODEL), F32).at[0].set(c_ctx).at[1:N_COND].set(c)
    mod = _modulation(cond, w_mod, b_mod).reshape(DEPTH, COND_PAD, N_MOD, D_MODEL)

    ch_tabs = _channel_tables()
    pos_tabs_ctx = _dft_tables(SEQ)
    pos_tabs_lat = _dft_tables(DEC_SEQ)
    cos_t, sin_t = _rope_tables()
    kc = cache_k.reshape(DEC_BATCH, DEPTH, PAST_LEN, D_KV)
    vc = cache_v.reshape(DEC_BATCH, DEPTH, PAST_LEN, D_KV)
    n_moe = moe_w_gate.shape[0]
    moe_wg = moe_w_gate.reshape(n_moe * N_EXPERTS, D_MODEL, D_FF_EXPERT)
    moe_wu = moe_w_up.reshape(n_moe * N_EXPERTS, D_MODEL, D_FF_EXPERT)
    moe_wd = moe_w_down.reshape(n_moe * N_EXPERTS, D_FF_EXPERT, D_MODEL)
    zero_state = jnp.zeros((BATCH, 2, D_HGRN, D_HGRN), F32)

    x = jnp.concatenate([x_prompt.reshape(TP, D_MODEL), x_sample.reshape(TS, D_MODEL)], axis=0)
    new_k, new_v, new_s = [], [], []
    for l in range(DEPTH):
        mod_l = mod[l]
        y_in = _in_projection(x, mod_l, w_in, l)

        lb2 = lower_bounds[l].reshape(2, D_HGRN)
        gw = jnp.tile(hgrn_norm[l], N_HGRN_HEADS).reshape(1, D_HGRN)
        a_ctx = _fourier_mix(y_in, 0, BATCH, SEQ, ch_tabs, pos_tabs_ctx, w_fourier, l)
        a_lat = _fourier_mix(y_in, TP, DEC_BATCH, DEC_SEQ, ch_tabs, pos_tabs_lat, w_fourier, l)
        b_ctx, s_ctx = _hgrn_mix(y_in, 0, BATCH, SEQ, lb2, gw, zero_state)
        b_lat, _ = _hgrn_mix(y_in, TP, DEC_BATCH, DEC_SEQ, lb2, gw,
                             _states_to_kernel_layout(state_hgrn[:, l]))
        c_ctx_out = _context_attention(y_in, attn_sink[l])
        c_lat = _latent_attention(y_in, attn_sink[l], kc, vc, l, cos_t, sin_t)

        k0 = 6 * 256 + D_ATTN
        new_k.append(y_in[:TP, k0:k0 + D_KV].reshape(BATCH, SEQ, N_KV_HEADS, HEAD_DIM))
        new_v.append(y_in[:TP, k0 + D_KV:k0 + 2 * D_KV].reshape(BATCH, SEQ, N_KV_HEADS, HEAD_DIM))
        new_s.append(_states_from_kernel_layout(s_ctx))

        out_a = jnp.concatenate([a_ctx, a_lat], axis=0)
        out_b = jnp.concatenate([b_ctx, b_lat], axis=0)
        out_c = jnp.concatenate([c_ctx_out, c_lat], axis=0)
        x1, h2 = _out_projection(out_a, out_b, out_c, x, mod_l, ln1_g[l], ln1_b[l], w_out, l)

        i = l // 2
        if l % 2 == 0:
            x = _dense_ffn(h2, x1, mod_l, ln2_g[l], ln2_b[l], ffn_w_gate, ffn_w_up, ffn_w_down, i)
        else:
            x = _moe_ffn(h2, x1, mod_l, ln2_g[l], ln2_b[l], router_w[i], router_b[i],
                         moe_wg, moe_wu, moe_wd, i * N_EXPERTS)

    xp = x[:TP].reshape(BATCH, SEQ, D_MODEL)
    xs = x[TP:].reshape(DEC_BATCH, DEC_SEQ, D_MODEL)
    return (xp, xs, jnp.stack(new_k, axis=1), jnp.stack(new_v, axis=1), jnp.stack(new_s, axis=1))
```

```python
import functools
import math

import jax
import jax.numpy as jnp
import numpy as np
from jax import lax
from jax.experimental import pallas as pl
from jax.experimental.pallas import tpu as pltpu

D_MODEL = 1024
BATCH = 16
SEQ = 256
DEPTH = 4
DEC_BATCH = 2
DEC_SEQ = 2048
PAST_LEN = 512
GRID_W = 64
D_FOURIER = 256
N_FOURIER_GROUPS = 4
FOURIER_GROUP = D_FOURIER // N_FOURIER_GROUPS
D_HGRN = 256
N_HGRN_HEADS = 4
HGRN_HEAD = D_HGRN // N_HGRN_HEADS
HGRN_CHUNK = 32
HGRN_GROUP = 128
HGRN_SAFE_DECAY = 120.0
N_Q_HEADS = 8
N_KV_HEADS = 2
GQA = N_Q_HEADS // N_KV_HEADS
HEAD_DIM = 64
D_ATTN = N_Q_HEADS * HEAD_DIM
D_KV = N_KV_HEADS * HEAD_DIM
D_MIX = D_FOURIER + D_HGRN + D_ATTN
WINDOW = 128
ATTN_BLOCK = 128
ATTN_SCALE = HEAD_DIM ** -0.5
ROPE_BASE = 10000.0
NEG_BIG = -1e30
D_FF = 2816
N_EXPERTS = 8
TOP_K = 2
D_FF_EXPERT = 3584
DEEPNORM_ALPHA = (2 * DEPTH) ** 0.25
LN_EPS = 1e-5
RMS_EPS = 1e-6
N_MOD = 6
D_IN = 6 * 256 + D_ATTN + 2 * D_KV

TP = BATCH * SEQ
TS = DEC_BATCH * DEC_SEQ
T = TP + TS
N_COND = 1 + DEC_BATCH
COND_PAD = 8

TM = 512
TM_FFN = 512
TM_LN = 256
LANES = 128
VMEM_LIMIT = 56 * 1024 * 1024

F32 = jnp.float32
BF16 = jnp.bfloat16


def _cparams(sem, vmem=VMEM_LIMIT):
    return pltpu.CompilerParams(dimension_semantics=sem, vmem_limit_bytes=vmem)


def _ln(x):
    mu = jnp.mean(x, axis=-1, keepdims=True)
    xc = x - mu
    var = jnp.mean(xc * xc, axis=-1, keepdims=True)
    return xc * lax.rsqrt(var + LN_EPS)


def _silu(x):
    return x * jax.nn.sigmoid(x)


def _split3(a):
    p0 = a.astype(BF16)
    r1 = a - p0.astype(F32)
    p1 = r1.astype(BF16)
    r2 = r1 - p1.astype(F32)
    return p0, p1, r2.astype(BF16)


def _split2(a):
    hi = a.astype(BF16)
    return hi, (a - hi.astype(F32)).astype(BF16)


def _dot(a, b):
    return jnp.dot(a, b, preferred_element_type=F32)


def _dot_nt(a, b):
    return lax.dot_general(a, b, (((1,), (1,)), ((), ())), preferred_element_type=F32)


def _dot_tn(a, b):
    return lax.dot_general(a, b, (((0,), (0,)), ((), ())), preferred_element_type=F32)


def _cond_of_tile(i, tm):
    n_ctx = TP // tm
    return jnp.where(i < n_ctx, 0, 1 + (i - n_ctx) // (DEC_SEQ // tm))


def _mod_kernel(c_ref, w_ref, b_ref, o_ref):
    a = _silu(c_ref[...]).astype(BF16)
    o_ref[...] = _dot(a, w_ref[...].astype(BF16)) + b_ref[...]


def _modulation(cond, w_mod, b_mod):
    tn = 2048
    n_out = N_MOD * D_MODEL
    return pl.pallas_call(
        _mod_kernel,
        grid=(DEPTH, n_out // tn),
        in_specs=[
            pl.BlockSpec((COND_PAD, D_MODEL), lambda l, j: (0, 0)),
            pl.BlockSpec((None, D_MODEL, tn), lambda l, j: (l, 0, j)),
            pl.BlockSpec((None, 1, tn), lambda l, j: (l, 0, j)),
        ],
        out_specs=pl.BlockSpec((None, COND_PAD, tn), lambda l, j: (l, 0, j)),
        out_shape=jax.ShapeDtypeStruct((DEPTH, COND_PAD, n_out), F32),
        compiler_params=_cparams(("parallel", "parallel")),
        name="modulation",
    )(cond, w_mod, b_mod.reshape(DEPTH, 1, n_out))


def _inproj_kernel(x_ref, mod_ref, w_ref, o_ref, wb_ref):
    @pl.when(pl.program_id(0) == 0)
    def _():
        wb_ref[...] = w_ref[...].astype(BF16)

    h = _ln(x_ref[...]) * (1.0 + mod_ref[1:2, :]) + mod_ref[0:1, :]
    o_ref[...] = _dot(h.astype(BF16), wb_ref[...])


def _in_projection(x, mod_l, w_in, l):
    return pl.pallas_call(
        _inproj_kernel,
        grid=(T // TM,),
        in_specs=[
            pl.BlockSpec((TM, D_MODEL), lambda i: (i, 0)),
            pl.BlockSpec((None, N_MOD, D_MODEL), lambda i: (_cond_of_tile(i, TM), 0, 0)),
            pl.BlockSpec((None, D_MODEL, D_IN), lambda i: (l, 0, 0), pipeline_mode=pl.Buffered(1)),
        ],
        out_specs=pl.BlockSpec((TM, D_IN), lambda i: (i, 0)),
        out_shape=jax.ShapeDtypeStruct((T, D_IN), F32),
        scratch_shapes=[pltpu.VMEM((D_MODEL, D_IN), BF16)],
        compiler_params=_cparams(("arbitrary",)),
        name="in_projection",
    )(x, mod_l, w_in)


def _fourier_kernel(u_ref, ch_hi_ref, ch_lo_ref, tab_hi_ref, tab_lo_ref, wf_ref, o_ref,
                    ab_hi_ref, ab_lo_ref, *, scale):
    @pl.when(pl.program_id(1) == 0)
    def _():
        uh, ul = _split2(u_ref[...])
        ch_hi = ch_hi_ref[...]
        ab = _dot(uh, ch_hi) + _dot(uh, ch_lo_ref[...]) + _dot(ul, ch_hi)
        stacked = jnp.concatenate([ab[:, :D_FOURIER], ab[:, D_FOURIER:]], axis=0)
        hi, lo = _split2(stacked)
        ab_hi_ref[...] = hi
        ab_lo_ref[...] = lo

    th = tab_hi_ref[...]
    ab_hi = ab_hi_ref[...]
    z = _dot(th, ab_hi) + _dot(th, ab_lo_ref[...]) + _dot(tab_lo_ref[...], ab_hi)
    z = (z * scale).astype(BF16)
    o_ref[...] = _dot(z, wf_ref[...].astype(BF16)).astype(BF16)


def _dft_tables(n):
    a = jnp.arange(n, dtype=jnp.int32)
    prod = (a[:, None] * a[None, :]) % n
    ang = prod.astype(F32) * (2.0 * math.pi / n)
    tab = jnp.concatenate([jnp.cos(ang), -jnp.sin(ang)], axis=1)
    return _split2(tab)


def _channel_tables():
    a = jnp.arange(D_FOURIER, dtype=jnp.int32)
    same = (a[:, None] // FOURIER_GROUP) == (a[None, :] // FOURIER_GROUP)
    prod = ((a[:, None] % FOURIER_GROUP) * (a[None, :] % FOURIER_GROUP)) % FOURIER_GROUP
    ang = prod.astype(F32) * (2.0 * math.pi / FOURIER_GROUP)
    c = jnp.where(same, jnp.cos(ang), 0.0)
    s = jnp.where(same, jnp.sin(ang), 0.0)
    return _split2(jnp.concatenate([c, s], axis=1))


def _fourier_mix(y_in, row0, nbatch, length, ch_tabs, pos_tabs, w_fourier, l):
    tr = min(length, 256)
    blk0 = row0 // length
    kern = functools.partial(_fourier_kernel, scale=1.0 / math.sqrt(length * FOURIER_GROUP))
    return pl.pallas_call(
        kern,
        grid=(nbatch, length // tr),
        in_specs=[
            pl.BlockSpec((length, D_FOURIER), lambda b, r: (blk0 + b, 0)),
            pl.BlockSpec((D_FOURIER, 2 * D_FOURIER), lambda b, r: (0, 0)),
            pl.BlockSpec((D_FOURIER, 2 * D_FOURIER), lambda b, r: (0, 0)),
            pl.BlockSpec((tr, 2 * length), lambda b, r: (r, 0)),
            pl.BlockSpec((tr, 2 * length), lambda b, r: (r, 0)),
            pl.BlockSpec((None, D_FOURIER, D_FOURIER), lambda b, r: (l, 0, 0)),
        ],
        out_specs=pl.BlockSpec((tr, D_FOURIER), lambda b, r: (b * (length // tr) + r, 0)),
        out_shape=jax.ShapeDtypeStruct((nbatch * length, D_FOURIER), BF16),
        scratch_shapes=[pltpu.VMEM((2 * length, D_FOURIER), BF16),
                        pltpu.VMEM((2 * length, D_FOURIER), BF16)],
        compiler_params=_cparams(("parallel", "arbitrary")),
        name=f"fourier_mix_{length}",
    )(y_in, ch_tabs[0], ch_tabs[1], pos_tabs[0], pos_tabs[1], w_fourier)


def _hgrn_pairwise_scan(d, nc, g_scr, k_scr, q_scr, v_ref, o_scr, st_scr, head_mask, ones_bd):
    c = HGRN_CHUNK
    last = c - 1 if d == 0 else 0

    def chunk(ci, carry):
        row = lax.broadcasted_iota(jnp.int32, (c, c), 0)
        col = lax.broadcasted_iota(jnp.int32, (c, c), 1)
        tri = jnp.where((col <= row) if d == 0 else (col >= row), 1.0, 0.0).astype(BF16)
        cc = ci if d == 0 else nc - 1 - ci
        rows = pl.ds(pl.multiple_of(cc * c, c), c)
        g = g_scr[rows, :]
        kk = k_scr[rows, :]
        q = q_scr[rows, :]
        v = v_ref[rows, :]
        g0, g1, g2 = _split3(g)
        b = _dot(tri, g0) + _dot(tri, g1) + _dot(tri, g2)
        btot = b[last:last + 1, :]
        st = st_scr[...]
        o_inter = _dot_nt((q * jnp.exp(b)).astype(BF16), st.astype(BF16))
        s_idx = lax.broadcasted_iota(jnp.int32, (c, c, D_HGRN), 0)
        t_idx = lax.broadcasted_iota(jnp.int32, (c, c, D_HGRN), 1)
        causal = (s_idx <= t_idx) if d == 0 else (s_idx >= t_idx)
        diff = b[None, :, :] - b[:, None, :]
        pair = q[None, :, :] * jnp.exp(jnp.minimum(diff, 0.0)) * kk[:, None, :]
        pair = jnp.where(causal, pair, 0.0).astype(BF16)
        attn = _dot(pair.reshape(c * c, D_HGRN), ones_bd).reshape(c, c, D_HGRN)
        o = o_inter + jnp.sum(attn * v[:, None, :], axis=0)
        if d == 0:
            o_scr[rows, :] = o
        else:
            o_scr[rows, :] = o_scr[rows, :] + o
        kd = kk * jnp.exp(btot - b)
        upd = _dot_tn(v.astype(BF16), kd.astype(BF16))
        st_scr[...] = jnp.exp(btot) * st + jnp.where(head_mask, upd, 0.0)
        return carry

    lax.fori_loop(0, nc, chunk, 0)


def _hgrn_factored_scan(d, nc, g_scr, k_scr, q_scr, v_ref, o_scr, st_scr, qd_scr, kd_scr, dec_scr, stb_scr,
                        head_mask):
    c = HGRN_CHUNK
    grp = HGRN_GROUP
    length = nc * c

    def group(gi, carry):
        rows = pl.ds(pl.multiple_of(gi * grp, grp), grp)
        r = lax.broadcasted_iota(jnp.int32, (grp, grp), 0)
        s = lax.broadcasted_iota(jnp.int32, (grp, grp), 1)
        same_chunk = (r // c) == (s // c)
        ordered = (s <= r) if d == 0 else (s >= r)
        tri = jnp.where(same_chunk & ordered, 1.0, 0.0).astype(BF16)
        ones_chunk = jnp.where(same_chunk, 1.0, 0.0).astype(BF16)
        g0, g1, g2 = _split3(g_scr[rows, :])
        b = _dot(tri, g0) + _dot(tri, g1) + _dot(tri, g2)
        tot = _dot(ones_chunk, g0) + _dot(ones_chunk, g1) + _dot(ones_chunk, g2)
        kk = k_scr[rows, :]
        q = q_scr[rows, :]
        half = 0.5 * tot
        qc = (q * jnp.exp(b - half)).astype(BF16)
        kh = (kk * jnp.exp(half - b)).astype(BF16)
        qd_scr[rows, :] = (q * jnp.exp(b)).astype(BF16)
        kd_scr[rows, :] = (kk * jnp.exp(tot - b)).astype(BF16)
        dec_scr[rows, :] = jnp.exp(tot)
        hs_row = lax.broadcasted_iota(jnp.int32, (N_HGRN_HEADS * grp, D_HGRN), 0) // grp
        hs_col = lax.broadcasted_iota(jnp.int32, (N_HGRN_HEADS * grp, D_HGRN), 1) // HGRN_HEAD
        same_head = hs_row == hs_col
        k_bd = jnp.where(same_head, jnp.concatenate([kh] * N_HGRN_HEADS, axis=0), 0.0)
        v_bd = jnp.where(same_head, jnp.concatenate([v_ref[rows, :].astype(BF16)] * N_HGRN_HEADS, axis=0), 0.0)
        t_idx = lax.broadcasted_iota(jnp.int32, (grp, N_HGRN_HEADS * grp), 0)
        s_idx = lax.broadcasted_iota(jnp.int32, (grp, N_HGRN_HEADS * grp), 1) % grp
        keep = ((t_idx // c) == (s_idx // c)) & ((s_idx <= t_idx) if d == 0 else (s_idx >= t_idx))
        attn = jnp.where(keep, _dot_nt(qc, k_bd), 0.0)
        o_intra = _dot(attn.astype(BF16), v_bd)
        if d == 0:
            o_scr[rows, :] = o_intra
        else:
            o_scr[rows, :] = o_scr[rows, :] + o_intra
        return carry

    lax.fori_loop(0, length // grp, group, 0, unroll=2)

    def state_step(ci, carry):
        cc = ci if d == 0 else nc - 1 - ci
        r0 = pl.multiple_of(cc * c, c)
        rows = pl.ds(r0, c)
        st = st_scr[...]
        stb_scr[cc] = st.astype(BF16)
        upd = _dot_tn(v_ref[rows, :].astype(BF16), kd_scr[rows, :])
        st_scr[...] = dec_scr[pl.ds(r0, 1), :] * st + jnp.where(head_mask, upd, 0.0)
        return carry

    lax.fori_loop(0, nc, state_step, 0, unroll=4)

    def inter(ci, carry):
        rows = pl.ds(pl.multiple_of(ci * c, c), c)
        o_scr[rows, :] = o_scr[rows, :] + _dot_nt(qd_scr[rows, :], stb_scr[ci])
        return carry

    lax.fori_loop(0, nc, inter, 0, unroll=4)


def _hgrn_kernel(hq_ref, hff_ref, hfb_ref, hi_ref, hg_ref, lb_ref, gw_ref, s0_ref,
                 o_ref, sfin_ref, q_scr, g_scr, k_scr, o_scr, dec_scr, qd_scr, kd_scr, st_scr, stb_scr,
                 *, length):
    c = HGRN_CHUNK
    nc = length // c
    q_scr[...] = _silu(hq_ref[...])
    r256 = lax.broadcasted_iota(jnp.int32, (D_HGRN, D_HGRN), 0) // HGRN_HEAD
    c256 = lax.broadcasted_iota(jnp.int32, (D_HGRN, D_HGRN), 1) // HGRN_HEAD
    head_mask = r256 == c256
    ones_bd = jnp.where(head_mask, 1.0, 0.0).astype(BF16)

    for d in range(2):
        z_ref = hff_ref if d == 0 else hfb_ref
        one_minus_f = (1.0 - lb_ref[d:d + 1, :]) * jax.nn.sigmoid(-z_ref[...])
        g_scr[...] = jnp.log1p(-one_minus_f)
        k_scr[...] = one_minus_f
        st_scr[...] = s0_ref[d]
        chunk_decay = jnp.sum(g_scr[...].reshape(nc, c, D_HGRN), axis=1)
        safe = jnp.min(chunk_decay) >= -HGRN_SAFE_DECAY
        lax.cond(
            safe,
            functools.partial(_hgrn_factored_scan, d, nc, g_scr, k_scr, q_scr, hi_ref, o_scr, st_scr,
                              qd_scr, kd_scr, dec_scr, stb_scr, head_mask),
            functools.partial(_hgrn_pairwise_scan, d, nc, g_scr, k_scr, q_scr, hi_ref, o_scr, st_scr,
                              head_mask, ones_bd))
        sfin_ref[d] = st_scr[...]

    o = o_scr[...]
    s0p, s1p, s2p = _split3(o * o)
    ms = (_dot(s0p, ones_bd) + _dot(s1p, ones_bd) + _dot(s2p, ones_bd)) * (1.0 / HGRN_HEAD)
    y = o * lax.rsqrt(ms + RMS_EPS) * gw_ref[...] * _silu(hg_ref[...])
    o_ref[...] = y.astype(BF16)


def _hgrn_mix(y_in, row0, nbatch, length, lb2, gw, s0):
    blk0 = row0 // length
    col = lambda j: (lambda b: (blk0 + b, j))
    kern = functools.partial(_hgrn_kernel, length=length)
    return pl.pallas_call(
        kern,
        grid=(nbatch,),
        in_specs=[
            pl.BlockSpec((length, D_HGRN), col(1)),
            pl.BlockSpec((length, D_HGRN), col(2)),
            pl.BlockSpec((length, D_HGRN), col(3)),
            pl.BlockSpec((length, D_HGRN), col(4)),
            pl.BlockSpec((length, D_HGRN), col(5)),
            pl.BlockSpec((2, D_HGRN), lambda b: (0, 0)),
            pl.BlockSpec((1, D_HGRN), lambda b: (0, 0)),
            pl.BlockSpec((None, 2, D_HGRN, D_HGRN), lambda b: (b, 0, 0, 0)),
        ],
        out_specs=[
            pl.BlockSpec((length, D_HGRN), lambda b: (b, 0)),
            pl.BlockSpec((None, 2, D_HGRN, D_HGRN), lambda b: (b, 0, 0, 0)),
        ],
        out_shape=[
            jax.ShapeDtypeStruct((nbatch * length, D_HGRN), BF16),
            jax.ShapeDtypeStruct((nbatch, 2, D_HGRN, D_HGRN), F32),
        ],
        scratch_shapes=[pltpu.VMEM((length, D_HGRN), F32) for _ in range(5)]
        + [pltpu.VMEM((length, D_HGRN), BF16) for _ in range(2)]
        + [pltpu.VMEM((D_HGRN, D_HGRN), F32), pltpu.VMEM((length // HGRN_CHUNK, D_HGRN, D_HGRN), BF16)],
        compiler_params=_cparams(("parallel",)),
        name=f"hgrn_mix_{length}",
    )(y_in, y_in, y_in, y_in, y_in, lb2, gw, s0)


def _states_to_kernel_layout(s):
    b = s.shape[0]
    st = jnp.swapaxes(s, -1, -2)
    eye = jnp.eye(N_HGRN_HEADS, dtype=s.dtype)
    full = jnp.einsum("bdhvk,hg->bdhvgk", st, eye)
    return full.reshape(b, 2, D_HGRN, D_HGRN)


def _states_from_kernel_layout(st):
    b = st.shape[0]
    full = st.reshape(b, 2, N_HGRN_HEADS, HGRN_HEAD, N_HGRN_HEADS, HGRN_HEAD)
    diag = jnp.stack([full[:, :, h, :, h, :] for h in range(N_HGRN_HEADS)], axis=2)
    return jnp.swapaxes(diag, -1, -2)


def _group_attention(q_heads, sinks, kv_parts):
    rows = q_heads[0].shape[0]
    qg = jnp.concatenate([(q * ATTN_SCALE).astype(BF16) for q in q_heads], axis=0)
    head = lax.broadcasted_iota(jnp.int32, (len(q_heads) * rows, 1), 0) // rows
    sink = jnp.full(head.shape, sinks[0], F32)
    for i in range(1, len(q_heads)):
        sink = jnp.where(head == i, sinks[i], sink)
    m = sink
    scores = []
    for k, _, mask in kv_parts:
        s = _dot_nt(qg, k)
        if mask is not None:
            s = jnp.where(mask, s, NEG_BIG)
        m = jnp.maximum(m, jnp.max(s, axis=1, keepdims=True))
        scores.append(s)
    den = jnp.exp(sink - m)
    out = None
    for s, (_, v, _) in zip(scores, kv_parts):
        e = jnp.exp(s - m)
        den = den + jnp.sum(e, axis=1, keepdims=True)
        o = _dot(e.astype(BF16), v)
        out = o if out is None else out + o
    out = out * (1.0 / den)
    return [out[i * rows:(i + 1) * rows, :] for i in range(len(q_heads))]


def _ctx_attn_kernel(sink_ref, q_ref, k_ref, v_ref, o_ref):
    outs = []
    for g in range(N_KV_HEADS):
        sl = slice(g * HEAD_DIM, (g + 1) * HEAD_DIM)
        heads = range(g * GQA, (g + 1) * GQA)
        outs += _group_attention(
            [q_ref[:, h * HEAD_DIM:(h + 1) * HEAD_DIM] for h in heads],
            [sink_ref[h] for h in heads],
            [(k_ref[:, sl].astype(BF16), v_ref[:, sl].astype(BF16), None)])
    o_ref[...] = jnp.concatenate(outs, axis=1).astype(BF16)


def _context_attention(y_in, sink_l):
    qcol = (6 * 256) // D_ATTN
    kcol = (6 * 256 + D_ATTN) // D_KV
    return pl.pallas_call(
        _ctx_attn_kernel,
        grid=(BATCH,),
        in_specs=[
            pl.BlockSpec(memory_space=pltpu.SMEM),
            pl.BlockSpec((SEQ, D_ATTN), lambda b: (b, qcol)),
            pl.BlockSpec((SEQ, D_KV), lambda b: (b, kcol)),
            pl.BlockSpec((SEQ, D_KV), lambda b: (b, kcol + 1)),
        ],
        out_specs=pl.BlockSpec((SEQ, D_ATTN), lambda b: (b, 0)),
        out_shape=jax.ShapeDtypeStruct((TP, D_ATTN), BF16),
        compiler_params=_cparams(("parallel",)),
        name="context_attention",
    )(sink_l, y_in, y_in, y_in)


def _rope(x, cos, sin):
    lane = lax.broadcasted_iota(jnp.int32, x.shape, 1)
    n_freq = HEAD_DIM // 4
    first = (lane % (2 * n_freq)) < n_freq
    swapped = jnp.where(first, pltpu.roll(x, LANES - n_freq, axis=1), pltpu.roll(x, n_freq, axis=1))
    return x * cos + swapped * sin


def _lat_attn_kernel(sink_ref, q_ref, k_ref, v_ref, kc_ref, vc_ref, cos_ref, sin_ref, o_ref):
    j = pl.program_id(1)
    nb = DEC_SEQ // ATTN_BLOCK
    blk = ATTN_BLOCK
    q0 = pl.multiple_of(j * blk, blk)
    cos_q = cos_ref[pl.ds(q0, blk), :]
    sin_q = sin_ref[pl.ds(q0, blk), :]

    starts = [jnp.maximum(j - 1, 0), j, jnp.minimum(j + 1, nb - 1)]
    k_band, v_band = [], []
    for st in starts:
        r0 = pl.multiple_of(st * blk, blk)
        kb = _rope(k_ref[pl.ds(r0, blk), :], cos_ref[pl.ds(r0, blk), :], sin_ref[pl.ds(r0, blk), :])
        k_band.append(kb)
        v_band.append(v_ref[pl.ds(r0, blk), :])
    k_loc = jnp.concatenate(k_band, axis=0)
    v_loc = jnp.concatenate(v_band, axis=0)
    r = lax.broadcasted_iota(jnp.int32, (GQA * blk, 3 * blk), 0) % blk
    cidx = lax.broadcasted_iota(jnp.int32, (GQA * blk, 3 * blk), 1)
    s_pos = (j - 1) * blk + cidx
    valid = (cidx >= r) & (cidx <= r + 2 * WINDOW) & (s_pos >= 0) & (s_pos < DEC_SEQ)

    heads_per_chunk = LANES // HEAD_DIM
    q_chunks = [_rope(q_ref[:, cg * LANES:(cg + 1) * LANES], cos_q, sin_q)
                for cg in range(N_Q_HEADS // heads_per_chunk)]

    def q_head(h):
        off = (h % heads_per_chunk) * HEAD_DIM
        return q_chunks[h // heads_per_chunk][:, off:off + HEAD_DIM]

    outs = []
    for g in range(N_KV_HEADS):
        sl = slice(g * HEAD_DIM, (g + 1) * HEAD_DIM)
        heads = range(g * GQA, (g + 1) * GQA)
        outs += _group_attention(
            [q_head(h) for h in heads],
            [sink_ref[h] for h in heads],
            [(k_loc[:, sl].astype(BF16), v_loc[:, sl].astype(BF16), valid),
             (kc_ref[:, sl].astype(BF16), vc_ref[:, sl].astype(BF16), None)])
    o_ref[...] = jnp.concatenate(outs, axis=1).astype(BF16)


def _latent_attention(y_in, sink_l, kc, vc, l, cos_t, sin_t):
    nb = DEC_SEQ // ATTN_BLOCK
    qrow0 = TP // ATTN_BLOCK
    krow0 = TP // DEC_SEQ
    qcol = (6 * 256) // D_ATTN
    kcol = (6 * 256 + D_ATTN) // D_KV
    return pl.pallas_call(
        _lat_attn_kernel,
        grid=(DEC_BATCH, nb),
        in_specs=[
            pl.BlockSpec(memory_space=pltpu.SMEM),
            pl.BlockSpec((ATTN_BLOCK, D_ATTN), lambda b, j: (qrow0 + b * nb + j, qcol)),
            pl.BlockSpec((DEC_SEQ, D_KV), lambda b, j: (krow0 + b, kcol)),
            pl.BlockSpec((DEC_SEQ, D_KV), lambda b, j: (krow0 + b, kcol + 1)),
            pl.BlockSpec((None, None, PAST_LEN, D_KV), lambda b, j: (b, l, 0, 0)),
            pl.BlockSpec((None, None, PAST_LEN, D_KV), lambda b, j: (b, l, 0, 0)),
            pl.BlockSpec((DEC_SEQ, LANES), lambda b, j: (0, 0)),
            pl.BlockSpec((DEC_SEQ, LANES), lambda b, j: (0, 0)),
        ],
        out_specs=pl.BlockSpec((ATTN_BLOCK, D_ATTN), lambda b, j: (b * nb + j, 0)),
        out_shape=jax.ShapeDtypeStruct((TS, D_ATTN), BF16),
        compiler_params=_cparams(("parallel", "parallel")),
        name="latent_attention",
    )(sink_l, y_in, y_in, y_in, kc, vc, cos_t, sin_t)


def _rope_tables():
    t = jnp.arange(DEC_SEQ)
    rows = (t // GRID_W).astype(F32)
    cols = (t % GRID_W).astype(F32)
    n_freq = HEAD_DIM // 4
    inv = ROPE_BASE ** (-jnp.arange(n_freq, dtype=F32) / n_freq)
    ar = rows[:, None] * inv
    ac = cols[:, None] * inv
    cos = jnp.concatenate([jnp.cos(ar), jnp.cos(ar), jnp.cos(ac), jnp.cos(ac)], axis=1)
    sin = jnp.concatenate([-jnp.sin(ar), jnp.sin(ar), -jnp.sin(ac), jnp.sin(ac)], axis=1)
    return jnp.tile(cos, (1, N_KV_HEADS)), jnp.tile(sin, (1, N_KV_HEADS))


def _outproj_kernel(a_ref, b_ref, c_ref, x_ref, mod_ref, g_ref, beta_ref, w_ref,
                    x1_ref, h2_ref, wb_ref):
    @pl.when(pl.program_id(0) == 0)
    def _():
        wb_ref[...] = w_ref[...].astype(BF16)

    y = (_dot(a_ref[...], wb_ref[0:D_FOURIER, :])
         + _dot(b_ref[...], wb_ref[D_FOURIER:D_FOURIER + D_HGRN, :])
         + _dot(c_ref[...], wb_ref[D_FOURIER + D_HGRN:D_MIX, :]))
    x1 = _ln(DEEPNORM_ALPHA * x_ref[...] + mod_ref[2:3, :] * y) * g_ref[...] + beta_ref[...]
    x1_ref[...] = x1
    h2_ref[...] = _ln(x1) * (1.0 + mod_ref[4:5, :]) + mod_ref[3:4, :]


def _out_projection(out_a, out_b, out_c, x, mod_l, ln_g, ln_b, w_out, l):
    row = lambda w: pl.BlockSpec((TM, w), lambda i: (i, 0))
    vec = pl.BlockSpec((1, D_MODEL), lambda i: (0, 0))
    return pl.pallas_call(
        _outproj_kernel,
        grid=(T // TM,),
        in_specs=[
            row(D_FOURIER), row(D_HGRN), row(D_ATTN), row(D_MODEL),
            pl.BlockSpec((None, N_MOD, D_MODEL), lambda i: (_cond_of_tile(i, TM), 0, 0)),
            vec, vec,
            pl.BlockSpec((None, D_MIX, D_MODEL), lambda i: (l, 0, 0), pipeline_mode=pl.Buffered(1)),
        ],
        out_specs=[row(D_MODEL), row(D_MODEL)],
        out_shape=[jax.ShapeDtypeStruct((T, D_MODEL), F32), jax.ShapeDtypeStruct((T, D_MODEL), F32)],
        scratch_shapes=[pltpu.VMEM((D_MIX, D_MODEL), BF16)],
        compiler_params=_cparams(("arbitrary",)),
        name="out_projection",
    )(out_a, out_b, out_c, x, mod_l, ln_g.reshape(1, D_MODEL), ln_b.reshape(1, D_MODEL), w_out)


def _expert_changed(te_ref, i):
    return (i == 0) | (te_ref[i] != te_ref[jnp.maximum(i - 1, 0)])


def _ffn_up_kernel(te_ref, nv_ref, x_ref, wg_ref, wu_ref, h_ref, wgb_ref, wub_ref):
    i = pl.program_id(1)

    @pl.when(_expert_changed(te_ref, i))
    def _():
        wgb_ref[...] = wg_ref[...].astype(BF16)
        wub_ref[...] = wu_ref[...].astype(BF16)

    @pl.when(i < nv_ref[0])
    def _():
        x = x_ref[...].astype(BF16)
        a = _dot(x, wgb_ref[...])
        b = _dot(x, wub_ref[...])
        h_ref[...] = (_silu(a) * b).astype(BF16)

    @pl.when(i >= nv_ref[0])
    def _():
        h_ref[...] = jnp.zeros_like(h_ref)


def _ffn_down_kernel(te_ref, nv_ref, h_ref, wd_ref, y_ref, wdb_ref):
    i = pl.program_id(1)

    @pl.when(_expert_changed(te_ref, i))
    def _():
        wdb_ref[...] = wd_ref[...].astype(BF16)

    @pl.when(i < nv_ref[0])
    def _():
        y_ref[...] = _dot(h_ref[...], wdb_ref[...])

    @pl.when(i >= nv_ref[0])
    def _():
        y_ref[...] = jnp.zeros_like(y_ref)


def _grouped_ffn(x_rows, tile_expert, n_valid, w_gate, w_up, w_down, f_splits):
    r = x_rows.shape[0]
    nt = r // TM_FFN
    f = w_gate.shape[-1]
    fh = f // f_splits
    assert fh * f_splits == f and fh % LANES == 0
    dh = D_MODEL // 2
    used = lambda i, nv: jnp.minimum(i, nv[0] - 1)
    h = pl.pallas_call(
        _ffn_up_kernel,
        grid_spec=pltpu.PrefetchScalarGridSpec(
            num_scalar_prefetch=2,
            grid=(f_splits, nt),
            in_specs=[
                pl.BlockSpec((TM_FFN, D_MODEL), lambda j, i, te, nv: (used(i, nv), 0)),
                pl.BlockSpec((None, D_MODEL, fh), lambda j, i, te, nv: (te[i], 0, j)),
                pl.BlockSpec((None, D_MODEL, fh), lambda j, i, te, nv: (te[i], 0, j)),
            ],
            out_specs=pl.BlockSpec((TM_FFN, fh), lambda j, i, te, nv: (i, j)),
            scratch_shapes=[pltpu.VMEM((D_MODEL, fh), BF16), pltpu.VMEM((D_MODEL, fh), BF16)],
        ),
        out_shape=jax.ShapeDtypeStruct((r, f), BF16),
        compiler_params=_cparams(("arbitrary", "arbitrary")),
        name="ffn_up",
    )(tile_expert, n_valid, x_rows, w_gate, w_up)
    return pl.pallas_call(
        _ffn_down_kernel,
        grid_spec=pltpu.PrefetchScalarGridSpec(
            num_scalar_prefetch=2,
            grid=(2, nt),
            in_specs=[
                pl.BlockSpec((TM_FFN, f), lambda j, i, te, nv: (used(i, nv), 0)),
                pl.BlockSpec((None, f, dh), lambda j, i, te, nv: (te[i], 0, j)),
            ],
            out_specs=pl.BlockSpec((TM_FFN, dh), lambda j, i, te, nv: (i, j)),
            scratch_shapes=[pltpu.VMEM((f, dh), BF16)],
        ),
        out_shape=jax.ShapeDtypeStruct((r, D_MODEL), F32),
        compiler_params=_cparams(("arbitrary", "arbitrary")),
        name="ffn_down",
    )(tile_expert, n_valid, h, w_down)


def _post_ffn(x1, y, g2, ln_g, ln_b):
    return _ln(DEEPNORM_ALPHA * x1 + g2 * y) * ln_g + ln_b


def _ln2_kernel(x1_ref, y_ref, mod_ref, g_ref, beta_ref, o_ref):
    o_ref[...] = _post_ffn(x1_ref[...], y_ref[...], mod_ref[5:6, :], g_ref[...], beta_ref[...])


def _post_ffn_norm(x1, y, mod_l, ln_g, ln_b):
    row = pl.BlockSpec((TM_LN, D_MODEL), lambda i: (i, 0))
    vec = pl.BlockSpec((1, D_MODEL), lambda i: (0, 0))
    return pl.pallas_call(
        _ln2_kernel,
        grid=(T // TM_LN,),
        in_specs=[row, row,
                  pl.BlockSpec((None, N_MOD, D_MODEL), lambda i: (_cond_of_tile(i, TM_LN), 0, 0)),
                  vec, vec],
        out_specs=row,
        out_shape=jax.ShapeDtypeStruct((T, D_MODEL), F32),
        compiler_params=_cparams(("parallel",)),
        name="post_ffn_norm",
    )(x1, y, mod_l, ln_g.reshape(1, D_MODEL), ln_b.reshape(1, D_MODEL))


def _router_kernel(h_ref, w_ref, b_ref, idx_ref, p_ref):
    logits = _dot(h_ref[...].astype(BF16), w_ref[...].astype(BF16)) + b_ref[...]
    lane = lax.broadcasted_iota(jnp.int32, logits.shape, 1)
    m1 = jnp.max(logits, axis=1, keepdims=True)
    i1 = jnp.min(jnp.where(logits == m1, lane, LANES), axis=1, keepdims=True)
    rest = jnp.where(lane == i1, -jnp.inf, logits)
    m2 = jnp.max(rest, axis=1, keepdims=True)
    i2 = jnp.min(jnp.where(rest == m2, lane, LANES), axis=1, keepdims=True)
    e = jnp.exp(m2 - m1)
    inv = 1.0 / (1.0 + e)
    idx_ref[...] = jnp.where(lane == 0, i1, jnp.where(lane == 1, i2, 0))
    p_ref[...] = jnp.where(lane == 0, inv, jnp.where(lane == 1, e * inv, 0.0))


def _router(h2, w_router, b_router):
    w = jnp.zeros((D_MODEL, LANES), F32).at[:, :N_EXPERTS].set(w_router)
    b = jnp.full((1, LANES), NEG_BIG, F32).at[0, :N_EXPERTS].set(b_router)
    row = lambda w_: pl.BlockSpec((TM, w_), lambda i: (i, 0))
    idx, p = pl.pallas_call(
        _router_kernel,
        grid=(T // TM,),
        in_specs=[row(D_MODEL), pl.BlockSpec((D_MODEL, LANES), lambda i: (0, 0)),
                  pl.BlockSpec((1, LANES), lambda i: (0, 0))],
        out_specs=[row(LANES), row(LANES)],
        out_shape=[jax.ShapeDtypeStruct((T, LANES), jnp.int32), jax.ShapeDtypeStruct((T, LANES), F32)],
        compiler_params=_cparams(("parallel",)),
        name="router",
    )(h2, w, b)
    return idx, p


def _row_copy(src, dst, s, d, sem):
    return pltpu.make_async_copy(src.at[pl.ds(s, 1), :], dst.at[pl.ds(d, 1), :], sem)


def _dispatch_kernel(dest_ref, last_ref, x_ref, o_hbm, zero_scr, sem, zsem):
    i = pl.program_id(0)

    @pl.when(i == 0)
    def _():
        zero_scr[...] = jnp.zeros_like(zero_scr)

        def fill(tile):
            r0 = pl.multiple_of(tile * TM_FFN, TM_FFN)
            return pltpu.make_async_copy(zero_scr, o_hbm.at[pl.ds(r0, TM_FFN), :], zsem)

        n_tiles = o_hbm.shape[0] // TM_FFN
        min_tiles = (T * TOP_K) // TM_FFN
        jobs = [(last_ref[e] >= 0, last_ref[e]) for e in range(N_EXPERTS)]
        jobs += [(t >= last_ref[N_EXPERTS], t) for t in range(min_tiles, n_tiles)]
        for go, tile in jobs:
            @pl.when(go)
            def _(tile=tile):
                fill(tile).start()
        for go, tile in jobs:
            @pl.when(go)
            def _(tile=tile):
                fill(tile).wait()

    base = i * TM_LN

    def issue(r, carry):
        for k in range(TOP_K):
            _row_copy(x_ref, o_hbm, r, dest_ref[(base + r) * TOP_K + k], sem).start()
        return carry

    lax.fori_loop(0, TM_LN, issue, 0, unroll=8)
    for k in range(TOP_K):
        pltpu.make_async_copy(x_ref, o_hbm.at[pl.ds(0, TM_LN), :], sem).wait()


def _dispatch(h2, dest, last_tile, n_rows):
    return pl.pallas_call(
        _dispatch_kernel,
        grid_spec=pltpu.PrefetchScalarGridSpec(
            num_scalar_prefetch=2,
            grid=(T // TM_LN,),
            in_specs=[pl.BlockSpec((TM_LN, D_MODEL), lambda i, d, l: (i, 0))],
            out_specs=pl.BlockSpec(memory_space=pl.ANY),
            scratch_shapes=[pltpu.VMEM((TM_FFN, D_MODEL), F32), pltpu.SemaphoreType.DMA(()),
                            pltpu.SemaphoreType.DMA(())],
        ),
        out_shape=jax.ShapeDtypeStruct((n_rows, D_MODEL), F32),
        compiler_params=_cparams(("arbitrary",)),
        name="dispatch",
    )(dest, last_tile, h2)


def _combine_kernel(pos_ref, x1_ref, p_ref, y_hbm, mod_ref, g_ref, beta_ref, o_ref, buf, sem):
    i = pl.program_id(0)
    base = i * TM_LN

    def issue(r, carry):
        for k in range(TOP_K):
            _row_copy(y_hbm, buf.at[k], pos_ref[(base + r) * TOP_K + k], r, sem.at[k]).start()
        return carry

    lax.fori_loop(0, TM_LN, issue, 0, unroll=8)
    for k in range(TOP_K):
        pltpu.make_async_copy(y_hbm.at[pl.ds(0, TM_LN), :], buf.at[k], sem.at[k]).wait()
    y = p_ref[:, 0:1] * buf[0]
    for k in range(1, TOP_K):
        y = y + p_ref[:, k:k + 1] * buf[k]
    o_ref[...] = _post_ffn(x1_ref[...], y, mod_ref[5:6, :], g_ref[...], beta_ref[...])


def _combine_norm(x1, p, y_rows, pos, mod_l, ln_g, ln_b):
    row = pl.BlockSpec((TM_LN, D_MODEL), lambda i, s: (i, 0))
    vec = pl.BlockSpec((1, D_MODEL), lambda i, s: (0, 0))
    return pl.pallas_call(
        _combine_kernel,
        grid_spec=pltpu.PrefetchScalarGridSpec(
            num_scalar_prefetch=1,
            grid=(T // TM_LN,),
            in_specs=[row, pl.BlockSpec((TM_LN, LANES), lambda i, s: (i, 0)),
                      pl.BlockSpec(memory_space=pl.ANY),
                      pl.BlockSpec((None, N_MOD, D_MODEL), lambda i, s: (_cond_of_tile(i, TM_LN), 0, 0)),
                      vec, vec],
            out_specs=row,
            scratch_shapes=[pltpu.VMEM((TOP_K, TM_LN, D_MODEL), F32), pltpu.SemaphoreType.DMA((TOP_K,))],
        ),
        out_shape=jax.ShapeDtypeStruct((T, D_MODEL), F32),
        compiler_params=_cparams(("arbitrary",)),
        name="combine_norm",
    )(pos, x1, p, y_rows, mod_l, ln_g.reshape(1, D_MODEL), ln_b.reshape(1, D_MODEL))


def _moe_plan(idx):
    n_assign = T * TOP_K
    n_tiles = n_assign // TM_FFN + N_EXPERTS
    e = idx[:, :TOP_K].reshape(n_assign)
    onehot = (e[:, None] == jnp.arange(N_EXPERTS, dtype=jnp.int32)[None, :]).astype(jnp.int32)
    csum = jnp.cumsum(onehot, axis=0)
    counts = csum[-1]
    tiles_e = (counts + TM_FFN - 1) // TM_FFN
    tile_end = jnp.cumsum(tiles_e)
    row0 = (tile_end - tiles_e) * TM_FFN
    dest = jnp.sum((csum - 1 + row0[None, :]) * onehot, axis=1)
    n_valid = tile_end[-1]
    tile_id = jnp.minimum(jnp.arange(n_tiles, dtype=jnp.int32), n_valid - 1)
    tile_expert = jnp.sum((tile_id[:, None] >= tile_end[None, :]).astype(jnp.int32), axis=1)
    last_tile = jnp.concatenate([jnp.where(tiles_e > 0, tile_end - 1, -1), n_valid.reshape(1)])
    return (dest.astype(jnp.int32), tile_expert.astype(jnp.int32), n_valid.reshape(1).astype(jnp.int32),
            last_tile.astype(jnp.int32), n_tiles * TM_FFN)


def _moe_ffn(h2, x1, mod_l, ln_g, ln_b, w_router, b_router, w_gate, w_up, w_down, expert0):
    idx, p = _router(h2, w_router, b_router)
    dest, tile_expert, n_valid, last_tile, n_rows = _moe_plan(idx)
    x_rows = _dispatch(h2, dest, last_tile, n_rows)
    y_rows = _grouped_ffn(x_rows, tile_expert + expert0, n_valid, w_gate, w_up, w_down, 4)
    return _combine_norm(x1, p, y_rows, dest, mod_l, ln_g, ln_b)


def _dense_ffn(h2, x1, mod_l, ln_g, ln_b, w_gate, w_up, w_down, index):
    nt = T // TM_FFN
    y = _grouped_ffn(h2, jnp.full((nt,), index, jnp.int32), jnp.full((1,), nt, jnp.int32),
                     w_gate, w_up, w_down, 2)
    return _post_ffn_norm(x1, y, mod_l, ln_g, ln_b)


def kernel(x_prompt, x_sample, c, cache_k, cache_v, state_hgrn, c_ctx, w_mod, b_mod, w_in, w_fourier, lb_logits, hgrn_norm, attn_sink, w_out, ln1_g, ln1_b, ln2_g, ln2_b, ffn_w_gate, ffn_w_up, ffn_w_down, router_w, router_b, moe_w_gate, moe_w_up, moe_w_down):
    lb_sm = jax.nn.softmax(lb_logits.astype(F32), axis=0)
    lower_bounds = jnp.clip(jnp.cumsum(lb_sm, axis=0) - lb_sm[0], 0.0, 1.0)

    cond = jnp.zeros((COND_PAD, D_MODEL), F32).at[0].set(c_ctx).at[1:N_COND].set(c)
    mod = _modulation(cond, w_mod, b_mod).reshape(DEPTH, COND_PAD, N_MOD, D_MODEL)

    ch_tabs = _channel_tables()
    pos_tabs_ctx = _dft_tables(SEQ)
    pos_tabs_lat = _dft_tables(DEC_SEQ)
    cos_t, sin_t = _rope_tables()
    kc = cache_k.reshape(DEC_BATCH, DEPTH, PAST_LEN, D_KV)
    vc = cache_v.reshape(DEC_BATCH, DEPTH, PAST_LEN, D_KV)
    n_moe = moe_w_gate.shape[0]
    moe_wg = moe_w_gate.reshape(n_moe * N_EXPERTS, D_MODEL, D_FF_EXPERT)
    moe_wu = moe_w_up.reshape(n_moe * N_EXPERTS, D_MODEL, D_FF_EXPERT)
    moe_wd = moe_w_down.reshape(n_moe * N_EXPERTS, D_FF_EXPERT, D_MODEL)
    zero_state = jnp.zeros((BATCH, 2, D_HGRN, D_HGRN), F32)

    x = jnp.concatenate([x_prompt.reshape(TP, D_MODEL), x_sample.reshape(TS, D_MODEL)], axis=0)
    new_k, new_v, new_s = [], [], []
    for l in range(DEPTH):
        mod_l = mod[l]
        y_in = _in_projection(x, mod_l, w_in, l)

        lb2 = lower_bounds[l].reshape(2, D_HGRN)
        gw = jnp.tile(hgrn_norm[l], N_HGRN_HEADS).reshape(1, D_HGRN)
        a_ctx = _fourier_mix(y_in, 0, BATCH, SEQ, ch_tabs, pos_tabs_ctx, w_fourier, l)
        a_lat = _fourier_mix(y_in, TP, DEC_BATCH, DEC_SEQ, ch_tabs, pos_tabs_lat, w_fourier, l)
        b_ctx, s_ctx = _hgrn_mix(y_in, 0, BATCH, SEQ, lb2, gw, zero_state)
        b_lat, _ = _hgrn_mix(y_in, TP, DEC_BATCH, DEC_SEQ, lb2, gw,
                             _states_to_kernel_layout(state_hgrn[:, l]))
        c_ctx_out = _context_attention(y_in, attn_sink[l])
        c_lat = _latent_attention(y_in, attn_sink[l], kc, vc, l, cos_t, sin_t)

        k0 = 6 * 256 + D_ATTN
        new_k.append(y_in[:TP, k0:k0 + D_KV].reshape(BATCH, SEQ, N_KV_HEADS, HEAD_DIM))
        new_v.append(y_in[:TP, k0 + D_KV:k0 + 2 * D_KV].reshape(BATCH, SEQ, N_KV_HEADS, HEAD_DIM))
        new_s.append(_states_from_kernel_layout(s_ctx))

        out_a = jnp.concatenate([a_ctx, a_lat], axis=0)
        out_b = jnp.concatenate([b_ctx, b_lat], axis=0)
        out_c = jnp.concatenate([c_ctx_out, c_lat], axis=0)
        x1, h2 = _out_projection(out_a, out_b, out_c, x, mod_l, ln1_g[l], ln1_b[l], w_out, l)

        i = l // 2
        if l % 2 == 0:
            x = _dense_ffn(h2, x1, mod_l, ln2_g[l], ln2_b[l], ffn_w_gate, ffn_w_up, ffn_w_down, i)
        else:
            x = _moe_ffn(h2, x1, mod_l, ln2_g[l], ln2_b[l], router_w[i], router_b[i],
                         moe_wg, moe_wu, moe_wd, i * N_EXPERTS)

    xp = x[:TP].reshape(BATCH, SEQ, D_MODEL)
    xs = x[TP:].reshape(DEC_BATCH, DEC_SEQ, D_MODEL)
    return (xp, xs, jnp.stack(new_k, axis=1), jnp.stack(new_v, axis=1), jnp.stack(new_s, axis=1))
```

```python
import functools
import math

import jax
import jax.numpy as jnp
import numpy as np
from jax import lax
from jax.experimental import pallas as pl
from jax.experimental.pallas import tpu as pltpu

D_MODEL = 1024
BATCH = 16
SEQ = 256
DEPTH = 4
DEC_BATCH = 2
DEC_SEQ = 2048
PAST_LEN = 512
GRID_W = 64
D_FOURIER = 256
N_FOURIER_GROUPS = 4
FOURIER_GROUP = D_FOURIER // N_FOURIER_GROUPS
D_HGRN = 256
N_HGRN_HEADS = 4
HGRN_HEAD = D_HGRN // N_HGRN_HEADS
HGRN_CHUNK = 32
HGRN_GROUP = 128
HGRN_SAFE_DECAY = 120.0
N_Q_HEADS = 8
N_KV_HEADS = 2
GQA = N_Q_HEADS // N_KV_HEADS
HEAD_DIM = 64
D_ATTN = N_Q_HEADS * HEAD_DIM
D_KV = N_KV_HEADS * HEAD_DIM
D_MIX = D_FOURIER + D_HGRN + D_ATTN
WINDOW = 128
ATTN_BLOCK = 128
ATTN_SCALE = HEAD_DIM ** -0.5
ROPE_BASE = 10000.0
NEG_BIG = -1e30
D_FF = 2816
N_EXPERTS = 8
TOP_K = 2
D_FF_EXPERT = 3584
DEEPNORM_ALPHA = (2 * DEPTH) ** 0.25
LN_EPS = 1e-5
RMS_EPS = 1e-6
N_MOD = 6
D_IN = 6 * 256 + D_ATTN + 2 * D_KV

TP = BATCH * SEQ
TS = DEC_BATCH * DEC_SEQ
T = TP + TS
N_COND = 1 + DEC_BATCH
COND_PAD = 8

TM = 512
TM_FFN = 512
TM_LN = 256
LANES = 128
VMEM_LIMIT = 56 * 1024 * 1024

F32 = jnp.float32
BF16 = jnp.bfloat16


def _cparams(sem, vmem=VMEM_LIMIT):
    return pltpu.CompilerParams(dimension_semantics=sem, vmem_limit_bytes=vmem)


def _ln(x):
    mu = jnp.mean(x, axis=-1, keepdims=True)
    xc = x - mu
    var = jnp.mean(xc * xc, axis=-1, keepdims=True)
    return xc * lax.rsqrt(var + LN_EPS)


def _silu(x):
    return x * jax.nn.sigmoid(x)


def _split3(a):
    p0 = a.astype(BF16)
    r1 = a - p0.astype(F32)
    p1 = r1.astype(BF16)
    r2 = r1 - p1.astype(F32)
    return p0, p1, r2.astype(BF16)


def _split2(a):
    hi = a.astype(BF16)
    return hi, (a - hi.astype(F32)).astype(BF16)


def _dot(a, b):
    return jnp.dot(a, b, preferred_element_type=F32)


def _dot_nt(a, b):
    return lax.dot_general(a, b, (((1,), (1,)), ((), ())), preferred_element_type=F32)


def _dot_tn(a, b):
    return lax.dot_general(a, b, (((0,), (0,)), ((), ())), preferred_element_type=F32)


def _cond_of_tile(i, tm):
    n_ctx = TP // tm
    return jnp.where(i < n_ctx, 0, 1 + (i - n_ctx) // (DEC_SEQ // tm))


def _mod_kernel(c_ref, w_ref, b_ref, o_ref):
    a = _silu(c_ref[...]).astype(BF16)
    o_ref[...] = _dot(a, w_ref[...].astype(BF16)) + b_ref[...]


def _modulation(cond, w_mod, b_mod):
    tn = 2048
    n_out = N_MOD * D_MODEL
    return pl.pallas_call(
        _mod_kernel,
        grid=(DEPTH, n_out // tn),
        in_specs=[
            pl.BlockSpec((COND_PAD, D_MODEL), lambda l, j: (0, 0)),
            pl.BlockSpec((None, D_MODEL, tn), lambda l, j: (l, 0, j)),
            pl.BlockSpec((None, 1, tn), lambda l, j: (l, 0, j)),
        ],
        out_specs=pl.BlockSpec((None, COND_PAD, tn), lambda l, j: (l, 0, j)),
        out_shape=jax.ShapeDtypeStruct((DEPTH, COND_PAD, n_out), F32),
        compiler_params=_cparams(("parallel", "parallel")),
        name="modulation",
    )(cond, w_mod, b_mod.reshape(DEPTH, 1, n_out))


def _inproj_kernel(x_ref, mod_ref, w_ref, o_ref, wb_ref):
    @pl.when(pl.program_id(0) == 0)
    def _():
        wb_ref[...] = w_ref[...].astype(BF16)

    h = _ln(x_ref[...]) * (1.0 + mod_ref[1:2, :]) + mod_ref[0:1, :]
    o_ref[...] = _dot(h.astype(BF16), wb_ref[...])


def _in_projection(x, mod_l, w_in, l):
    return pl.pallas_call(
        _inproj_kernel,
        grid=(T // TM,),
        in_specs=[
            pl.BlockSpec((TM, D_MODEL), lambda i: (i, 0)),
            pl.BlockSpec((None, N_MOD, D_MODEL), lambda i: (_cond_of_tile(i, TM), 0, 0)),
            pl.BlockSpec((None, D_MODEL, D_IN), lambda i: (l, 0, 0), pipeline_mode=pl.Buffered(1)),
        ],
        out_specs=pl.BlockSpec((TM, D_IN), lambda i: (i, 0)),
        out_shape=jax.ShapeDtypeStruct((T, D_IN), F32),
        scratch_shapes=[pltpu.VMEM((D_MODEL, D_IN), BF16)],
        compiler_params=_cparams(("arbitrary",)),
        name="in_projection",
    )(x, mod_l, w_in)


def _fourier_kernel(u_ref, ch_hi_ref, ch_lo_ref, tab_hi_ref, tab_lo_ref, wf_ref, o_ref,
                    ab_hi_ref, ab_lo_ref, *, scale):
    @pl.when(pl.program_id(1) == 0)
    def _():
        uh, ul = _split2(u_ref[...])
        ch_hi = ch_hi_ref[...]
        ab = _dot(uh, ch_hi) + _dot(uh, ch_lo_ref[...]) + _dot(ul, ch_hi)
        stacked = jnp.concatenate([ab[:, :D_FOURIER], ab[:, D_FOURIER:]], axis=0)
        hi, lo = _split2(stacked)
        ab_hi_ref[...] = hi
        ab_lo_ref[...] = lo

    th = tab_hi_ref[...]
    ab_hi = ab_hi_ref[...]
    z = _dot(th, ab_hi) + _dot(th, ab_lo_ref[...]) + _dot(tab_lo_ref[...], ab_hi)
    z = (z * scale).astype(BF16)
    o_ref[...] = _dot(z, wf_ref[...].astype(BF16)).astype(BF16)


def _dft_tables(n):
    a = jnp.arange(n, dtype=jnp.int32)
    prod = (a[:, None] * a[None, :]) % n
    ang = prod.astype(F32) * (2.0 * math.pi / n)
    tab = jnp.concatenate([jnp.cos(ang), -jnp.sin(ang)], axis=1)
    return _split2(tab)


def _channel_tables():
    a = jnp.arange(D_FOURIER, dtype=jnp.int32)
    same = (a[:, None] // FOURIER_GROUP) == (a[None, :] // FOURIER_GROUP)
    prod = ((a[:, None] % FOURIER_GROUP) * (a[None, :] % FOURIER_GROUP)) % FOURIER_GROUP
    ang = prod.astype(F32) * (2.0 * math.pi / FOURIER_GROUP)
    c = jnp.where(same, jnp.cos(ang), 0.0)
    s = jnp.where(same, jnp.sin(ang), 0.0)
    return _split2(jnp.concatenate([c, s], axis=1))


def _fourier_mix(y_in, row0, nbatch, length, ch_tabs, pos_tabs, w_fourier, l):
    tr = min(length, 256)
    blk0 = row0 // length
    kern = functools.partial(_fourier_kernel, scale=1.0 / math.sqrt(length * FOURIER_GROUP))
    return pl.pallas_call(
        kern,
        grid=(nbatch, length // tr),
        in_specs=[
            pl.BlockSpec((length, D_FOURIER), lambda b, r: (blk0 + b, 0)),
            pl.BlockSpec((D_FOURIER, 2 * D_FOURIER), lambda b, r: (0, 0)),
            pl.BlockSpec((D_FOURIER, 2 * D_FOURIER), lambda b, r: (0, 0)),
            pl.BlockSpec((tr, 2 * length), lambda b, r: (r, 0)),
            pl.BlockSpec((tr, 2 * length), lambda b, r: (r, 0)),
            pl.BlockSpec((None, D_FOURIER, D_FOURIER), lambda b, r: (l, 0, 0)),
        ],
        out_specs=pl.BlockSpec((tr, D_FOURIER), lambda b, r: (b * (length // tr) + r, 0)),
        out_shape=jax.ShapeDtypeStruct((nbatch * length, D_FOURIER), BF16),
        scratch_shapes=[pltpu.VMEM((2 * length, D_FOURIER), BF16),
                        pltpu.VMEM((2 * length, D_FOURIER), BF16)],
        compiler_params=_cparams(("parallel", "arbitrary")),
        name=f"fourier_mix_{length}",
    )(y_in, ch_tabs[0], ch_tabs[1], pos_tabs[0], pos_tabs[1], w_fourier)


def _hgrn_pairwise_scan(d, nc, g_scr, k_scr, q_scr, v_ref, o_scr, st_scr, head_mask, ones_bd):
    c = HGRN_CHUNK
    last = c - 1 if d == 0 else 0

    def chunk(ci, carry):
        row = lax.broadcasted_iota(jnp.int32, (c, c), 0)
        col = lax.broadcasted_iota(jnp.int32, (c, c), 1)
        tri = jnp.where((col <= row) if d == 0 else (col >= row), 1.0, 0.0).astype(BF16)
        cc = ci if d == 0 else nc - 1 - ci
        rows = pl.ds(pl.multiple_of(cc * c, c), c)
        g = g_scr[rows, :]
        kk = k_scr[rows, :]
        q = q_scr[rows, :]
        v = v_ref[rows, :]
        g0, g1, g2 = _split3(g)
        b = _dot(tri, g0) + _dot(tri, g1) + _dot(tri, g2)
        btot = b[last:last + 1, :]
        st = st_scr[...]
        o_inter = _dot_nt((q * jnp.exp(b)).astype(BF16), st.astype(BF16))
        s_idx = lax.broadcasted_iota(jnp.int32, (c, c, D_HGRN), 0)
        t_idx = lax.broadcasted_iota(jnp.int32, (c, c, D_HGRN), 1)
        causal = (s_idx <= t_idx) if d == 0 else (s_idx >= t_idx)
        diff = b[None, :, :] - b[:, None, :]
        pair = q[None, :, :] * jnp.exp(jnp.minimum(diff, 0.0)) * kk[:, None, :]
        pair = jnp.where(causal, pair, 0.0).astype(BF16)
        attn = _dot(pair.reshape(c * c, D_HGRN), ones_bd).reshape(c, c, D_HGRN)
        o = o_inter + jnp.sum(attn * v[:, None, :], axis=0)
        if d == 0:
            o_scr[rows, :] = o
        else:
            o_scr[rows, :] = o_scr[rows, :] + o
        kd = kk * jnp.exp(btot - b)
        upd = _dot_tn(v.astype(BF16), kd.astype(BF16))
        st_scr[...] = jnp.exp(btot) * st + jnp.where(head_mask, upd, 0.0)
        return carry

    lax.fori_loop(0, nc, chunk, 0)


def _hgrn_factored_scan(d, nc, g_scr, k_scr, q_scr, v_ref, o_scr, st_scr, qd_scr, kd_scr, dec_scr, stb_scr,
                        head_mask):
    c = HGRN_CHUNK
    grp = HGRN_GROUP
    length = nc * c

    def group(gi, carry):
        rows = pl.ds(pl.multiple_of(gi * grp, grp), grp)
        r = lax.broadcasted_iota(jnp.int32, (grp, grp), 0)
        s = lax.broadcasted_iota(jnp.int32, (grp, grp), 1)
        same_chunk = (r // c) == (s // c)
        ordered = (s <= r) if d == 0 else (s >= r)
        tri = jnp.where(same_chunk & ordered, 1.0, 0.0).astype(BF16)
        ones_chunk = jnp.where(same_chunk, 1.0, 0.0).astype(BF16)
        g0, g1, g2 = _split3(g_scr[rows, :])
        b = _dot(tri, g0) + _dot(tri, g1) + _dot(tri, g2)
        tot = _dot(ones_chunk, g0) + _dot(ones_chunk, g1) + _dot(ones_chunk, g2)
        kk = k_scr[rows, :]
        q = q_scr[rows, :]
        half = 0.5 * tot
        qc = (q * jnp.exp(b - half)).astype(BF16)
        kh = (kk * jnp.exp(half - b)).astype(BF16)
        qd_scr[rows, :] = (q * jnp.exp(b)).astype(BF16)
        kd_scr[rows, :] = (kk * jnp.exp(tot - b)).astype(BF16)
        dec_scr[rows, :] = jnp.exp(tot)
        hs_row = lax.broadcasted_iota(jnp.int32, (N_HGRN_HEADS * grp, D_HGRN), 0) // grp
        hs_col = lax.broadcasted_iota(jnp.int32, (N_HGRN_HEADS * grp, D_HGRN), 1) // HGRN_HEAD
        same_head = hs_row == hs_col
        k_bd = jnp.where(same_head, jnp.concatenate([kh] * N_HGRN_HEADS, axis=0), 0.0)
        v_bd = jnp.where(same_head, jnp.concatenate([v_ref[rows, :].astype(BF16)] * N_HGRN_HEADS, axis=0), 0.0)
        t_idx = lax.broadcasted_iota(jnp.int32, (grp, N_HGRN_HEADS * grp), 0)
        s_idx = lax.broadcasted_iota(jnp.int32, (grp, N_HGRN_HEADS * grp), 1) % grp
        keep = ((t_idx // c) == (s_idx // c)) & ((s_idx <= t_idx) if d == 0 else (s_idx >= t_idx))
        attn = jnp.where(keep, _dot_nt(qc, k_bd), 0.0)
        o_intra = _dot(attn.astype(BF16), v_bd)
        if d == 0:
            o_scr[rows, :] = o_intra
        else:
            o_scr[rows, :] = o_scr[rows, :] + o_intra
        return carry

    lax.fori_loop(0, length // grp, group, 0, unroll=2)

    def state_step(ci, carry):
        cc = ci if d == 0 else nc - 1 - ci
        r0 = pl.multiple_of(cc * c, c)
        rows = pl.ds(r0, c)
        st = st_scr[...]
        stb_scr[cc] = st.astype(BF16)
        upd = _dot_tn(v_ref[rows, :].astype(BF16), kd_scr[rows, :])
        st_scr[...] = dec_scr[pl.ds(r0, 1), :] * st + jnp.where(head_mask, upd, 0.0)
        return carry

    lax.fori_loop(0, nc, state_step, 0, unroll=4)

    def inter(ci, carry):
        rows = pl.ds(pl.multiple_of(ci * c, c), c)
        o_scr[rows, :] = o_scr[rows, :] + _dot_nt(qd_scr[rows, :], stb_scr[ci])
        return carry

    lax.fori_loop(0, nc, inter, 0, unroll=4)


def _hgrn_kernel(hq_ref, hff_ref, hfb_ref, hi_ref, hg_ref, lb_ref, gw_ref, s0_ref,
                 o_ref, sfin_ref, q_scr, g_scr, k_scr, o_scr, dec_scr, qd_scr, kd_scr, st_scr, stb_scr,
                 *, length):
    c = HGRN_CHUNK
    nc = length // c
    q_scr[...] = _silu(hq_ref[...])
    r256 = lax.broadcasted_iota(jnp.int32, (D_HGRN, D_HGRN), 0) // HGRN_HEAD
    c256 = lax.broadcasted_iota(jnp.int32, (D_HGRN, D_HGRN), 1) // HGRN_HEAD
    head_mask = r256 == c256
    ones_bd = jnp.where(head_mask, 1.0, 0.0).astype(BF16)

    for d in range(2):
        z_ref = hff_ref if d == 0 else hfb_ref
        one_minus_f = (1.0 - lb_ref[d:d + 1, :]) * jax.nn.sigmoid(-z_ref[...])
        g_scr[...] = jnp.log1p(-one_minus_f)
        k_scr[...] = one_minus_f
        st_scr[...] = s0_ref[d]
        chunk_decay = jnp.sum(g_scr[...].reshape(nc, c, D_HGRN), axis=1)
        safe = jnp.min(chunk_decay) >= -HGRN_SAFE_DECAY
        lax.cond(
            safe,
            functools.partial(_hgrn_factored_scan, d, nc, g_scr, k_scr, q_scr, hi_ref, o_scr, st_scr,
                              qd_scr, kd_scr, dec_scr, stb_scr, head_mask),
            functools.partial(_hgrn_pairwise_scan, d, nc, g_scr, k_scr, q_scr, hi_ref, o_scr, st_scr,
                              head_mask, ones_bd))
        sfin_ref[d] = st_scr[...]

    o = o_scr[...]
    s0p, s1p, s2p = _split3(o * o)
    ms = (_dot(s0p, ones_bd) + _dot(s1p, ones_bd) + _dot(s2p, ones_bd)) * (1.0 / HGRN_HEAD)
    y = o * lax.rsqrt(ms + RMS_EPS) * gw_ref[...] * _silu(hg_ref[...])
    o_ref[...] = y.astype(BF16)


def _hgrn_mix(y_in, row0, nbatch, length, lb2, gw, s0):
    blk0 = row0 // length
    col = lambda j: (lambda b: (blk0 + b, j))
    kern = functools.partial(_hgrn_kernel, length=length)
    return pl.pallas_call(
        kern,
        grid=(nbatch,),
        in_specs=[
            pl.BlockSpec((length, D_HGRN), col(1)),
            pl.BlockSpec((length, D_HGRN), col(2)),
            pl.BlockSpec((length, D_HGRN), col(3)),
            pl.BlockSpec((length, D_HGRN), col(4)),
            pl.BlockSpec((length, D_HGRN), col(5)),
            pl.BlockSpec((2, D_HGRN), lambda b: (0, 0)),
            pl.BlockSpec((1, D_HGRN), lambda b: (0, 0)),
            pl.BlockSpec((None, 2, D_HGRN, D_HGRN), lambda b: (b, 0, 0, 0)),
        ],
        out_specs=[
            pl.BlockSpec((length, D_HGRN), lambda b: (b, 0)),
            pl.BlockSpec((None, 2, D_HGRN, D_HGRN), lambda b: (b, 0, 0, 0)),
        ],
        out_shape=[
            jax.ShapeDtypeStruct((nbatch * length, D_HGRN), BF16),
            jax.ShapeDtypeStruct((nbatch, 2, D_HGRN, D_HGRN), F32),
        ],
        scratch_shapes=[pltpu.VMEM((length, D_HGRN), F32) for _ in range(5)]
        + [pltpu.VMEM((length, D_HGRN), BF16) for _ in range(2)]
        + [pltpu.VMEM((D_HGRN, D_HGRN), F32), pltpu.VMEM((length // HGRN_CHUNK, D_HGRN, D_HGRN), BF16)],
        compiler_params=_cparams(("parallel",)),
        name=f"hgrn_mix_{length}",
    )(y_in, y_in, y_in, y_in, y_in, lb2, gw, s0)


def _states_to_kernel_layout(s):
    b = s.shape[0]
    st = jnp.swapaxes(s, -1, -2)
    eye = jnp.eye(N_HGRN_HEADS, dtype=s.dtype)
    full = jnp.einsum("bdhvk,hg->bdhvgk", st, eye)
    return full.reshape(b, 2, D_HGRN, D_HGRN)


def _states_from_kernel_layout(st):
    b = st.shape[0]
    full = st.reshape(b, 2, N_HGRN_HEADS, HGRN_HEAD, N_HGRN_HEADS, HGRN_HEAD)
    diag = jnp.stack([full[:, :, h, :, h, :] for h in range(N_HGRN_HEADS)], axis=2)
    return jnp.swapaxes(diag, -1, -2)


def _group_attention(q_heads, sinks, kv_parts):
    rows = q_heads[0].shape[0]
    qg = jnp.concatenate([(q * ATTN_SCALE).astype(BF16) for q in q_heads], axis=0)
    head = lax.broadcasted_iota(jnp.int32, (len(q_heads) * rows, 1), 0) // rows
    sink = jnp.full(head.shape, sinks[0], F32)
    for i in range(1, len(q_heads)):
        sink = jnp.where(head == i, sinks[i], sink)
    m = sink
    scores = []
    for k, _, mask in kv_parts:
        s = _dot_nt(qg, k)
        if mask is not None:
            s = jnp.where(mask, s, NEG_BIG)
        m = jnp.maximum(m, jnp.max(s, axis=1, keepdims=True))
        scores.append(s)
    den = jnp.exp(sink - m)
    out = None
    for s, (_, v, _) in zip(scores, kv_parts):
        e = jnp.exp(s - m)
        den = den + jnp.sum(e, axis=1, keepdims=True)
        o = _dot(e.astype(BF16), v)
        out = o if out is None else out + o
    out = out * (1.0 / den)
    return [out[i * rows:(i + 1) * rows, :] for i in range(len(q_heads))]


def _ctx_attn_kernel(sink_ref, q_ref, k_ref, v_ref, o_ref):
    outs = []
    for g in range(N_KV_HEADS):
        sl = slice(g * HEAD_DIM, (g + 1) * HEAD_DIM)
        heads = range(g * GQA, (g + 1) * GQA)
        outs += _group_attention(
            [q_ref[:, h * HEAD_DIM:(h + 1) * HEAD_DIM] for h in heads],
            [sink_ref[h] for h in heads],
            [(k_ref[:, sl].astype(BF16), v_ref[:, sl].astype(BF16), None)])
    o_ref[...] = jnp.concatenate(outs, axis=1).astype(BF16)


def _context_attention(y_in, sink_l):
    qcol = (6 * 256) // D_ATTN
    kcol = (6 * 256 + D_ATTN) // D_KV
    return pl.pallas_call(
        _ctx_attn_kernel,
        grid=(BATCH,),
        in_specs=[
            pl.BlockSpec(memory_space=pltpu.SMEM),
            pl.BlockSpec((SEQ, D_ATTN), lambda b: (b, qcol)),
            pl.BlockSpec((SEQ, D_KV), lambda b: (b, kcol)),
            pl.BlockSpec((SEQ, D_KV), lambda b: (b, kcol + 1)),
        ],
        out_specs=pl.BlockSpec((SEQ, D_ATTN), lambda b: (b, 0)),
        out_shape=jax.ShapeDtypeStruct((TP, D_ATTN), BF16),
        compiler_params=_cparams(("parallel",)),
        name="context_attention",
    )(sink_l, y_in, y_in, y_in)


def _rope(x, cos, sin):
    lane = lax.broadcasted_iota(jnp.int32, x.shape, 1)
    n_freq = HEAD_DIM // 4
    first = (lane % (2 * n_freq)) < n_freq
    swapped = jnp.where(first, pltpu.roll(x, LANES - n_freq, axis=1), pltpu.roll(x, n_freq, axis=1))
    return x * cos + swapped * sin


def _lat_attn_kernel(sink_ref, q_ref, k_ref, v_ref, kc_ref, vc_ref, cos_ref, sin_ref, o_ref):
    j = pl.program_id(1)
    nb = DEC_SEQ // ATTN_BLOCK
    blk = ATTN_BLOCK
    q0 = pl.multiple_of(j * blk, blk)
    cos_q = cos_ref[pl.ds(q0, blk), :]
    sin_q = sin_ref[pl.ds(q0, blk), :]

    starts = [jnp.maximum(j - 1, 0), j, jnp.minimum(j + 1, nb - 1)]
    k_band, v_band = [], []
    for st in starts:
        r0 = pl.multiple_of(st * blk, blk)
        kb = _rope(k_ref[pl.ds(r0, blk), :], cos_ref[pl.ds(r0, blk), :], sin_ref[pl.ds(r0, blk), :])
        k_band.append(kb)
        v_band.append(v_ref[pl.ds(r0, blk), :])
    k_loc = jnp.concatenate(k_band, axis=0)
    v_loc = jnp.concatenate(v_band, axis=0)
    r = lax.broadcasted_iota(jnp.int32, (GQA * blk, 3 * blk), 0) % blk
    cidx = lax.broadcasted_iota(jnp.int32, (GQA * blk, 3 * blk), 1)
    s_pos = (j - 1) * blk + cidx
    valid = (cidx >= r) & (cidx <= r + 2 * WINDOW) & (s_pos >= 0) & (s_pos < DEC_SEQ)

    heads_per_chunk = LANES // HEAD_DIM
    q_chunks = [_rope(q_ref[:, cg * LANES:(cg + 1) * LANES], cos_q, sin_q)
                for cg in range(N_Q_HEADS // heads_per_chunk)]

    def q_head(h):
        off = (h % heads_per_chunk) * HEAD_DIM
        return q_chunks[h // heads_per_chunk][:, off:off + HEAD_DIM]

    outs = []
    for g in range(N_KV_HEADS):
        sl = slice(g * HEAD_DIM, (g + 1) * HEAD_DIM)
        heads = range(g * GQA, (g + 1) * GQA)
        outs += _group_attention(
            [q_head(h) for h in heads],
            [sink_ref[h] for h in heads],
            [(k_loc[:, sl].astype(BF16), v_loc[:, sl].astype(BF16), valid),
             (kc_ref[:, sl].astype(BF16), vc_ref[:, sl].astype(BF16), None)])
    o_ref[...] = jnp.concatenate(outs, axis=1).astype(BF16)


def _latent_attention(y_in, sink_l, kc, vc, l, cos_t, sin_t):
    nb = DEC_SEQ // ATTN_BLOCK
    qrow0 = TP // ATTN_BLOCK
    krow0 = TP // DEC_SEQ
    qcol = (6 * 256) // D_ATTN
    kcol = (6 * 256 + D_ATTN) // D_KV
    return pl.pallas_call(
        _lat_attn_kernel,
        grid=(DEC_BATCH, nb),
        in_specs=[
            pl.BlockSpec(memory_space=pltpu.SMEM),
            pl.BlockSpec((ATTN_BLOCK, D_ATTN), lambda b, j: (qrow0 + b * nb + j, qcol)),
            pl.BlockSpec((DEC_SEQ, D_KV), lambda b, j: (krow0 + b, kcol)),
            pl.BlockSpec((DEC_SEQ, D_KV), lambda b, j: (krow0 + b, kcol + 1)),
            pl.BlockSpec((None, None, PAST_LEN, D_KV), lambda b, j: (b, l, 0, 0)),
            pl.BlockSpec((None, None, PAST_LEN, D_KV), lambda b, j: (b, l, 0, 0)),
            pl.BlockSpec((DEC_SEQ, LANES), lambda b, j: (0, 0)),
            pl.BlockSpec((DEC_SEQ, LANES), lambda b, j: (0, 0)),
        ],
        out_specs=pl.BlockSpec((ATTN_BLOCK, D_ATTN), lambda b, j: (b * nb + j, 0)),
        out_shape=jax.ShapeDtypeStruct((TS, D_ATTN), BF16),
        compiler_params=_cparams(("parallel", "parallel")),
        name="latent_attention",
    )(sink_l, y_in, y_in, y_in, kc, vc, cos_t, sin_t)


def _rope_tables():
    t = jnp.arange(DEC_SEQ)
    rows = (t // GRID_W).astype(F32)
    cols = (t % GRID_W).astype(F32)
    n_freq = HEAD_DIM // 4
    inv = ROPE_BASE ** (-jnp.arange(n_freq, dtype=F32) / n_freq)
    ar = rows[:, None] * inv
    ac = cols[:, None] * inv
    cos = jnp.concatenate([jnp.cos(ar), jnp.cos(ar), jnp.cos(ac), jnp.cos(ac)], axis=1)
    sin = jnp.concatenate([-jnp.sin(ar), jnp.sin(ar), -jnp.sin(ac), jnp.sin(ac)], axis=1)
    return jnp.tile(cos, (1, N_KV_HEADS)), jnp.tile(sin, (1, N_KV_HEADS))


def _top2_route(h2, w_ref, b_ref, idx_ref, p_ref):
    logits = _dot(h2.astype(BF16), w_ref[...].astype(BF16)) + b_ref[...]
    lane = lax.broadcasted_iota(jnp.int32, logits.shape, 1)
    m1 = jnp.max(logits, axis=1, keepdims=True)
    i1 = jnp.min(jnp.where(logits == m1, lane, LANES), axis=1, keepdims=True)
    rest = jnp.where(lane == i1, -jnp.inf, logits)
    m2 = jnp.max(rest, axis=1, keepdims=True)
    i2 = jnp.min(jnp.where(rest == m2, lane, LANES), axis=1, keepdims=True)
    e = jnp.exp(m2 - m1)
    inv = 1.0 / (1.0 + e)
    idx_ref[...] = jnp.where(lane == 0, i1, jnp.where(lane == 1, i2, 0))
    p_ref[...] = jnp.where(lane == 0, inv, jnp.where(lane == 1, e * inv, 0.0))


def _outproj_kernel(a0_ref, a1_ref, b0_ref, b1_ref, c0_ref, c1_ref, x_ref, mod_ref, g_ref, beta_ref, w_ref,
                    *rest, route):
    if route:
        wr_ref, br_ref, x1_ref, h2_ref, idx_ref, p_ref, wb_ref = rest
    else:
        x1_ref, h2_ref, wb_ref = rest

    @pl.when(pl.program_id(0) == 0)
    def _():
        wb_ref[...] = w_ref[...].astype(BF16)

    is_ctx = pl.program_id(0) < TP // TM
    pick = lambda r0, r1: jnp.where(is_ctx, r0[...], r1[...])
    y = (_dot(pick(a0_ref, a1_ref), wb_ref[0:D_FOURIER, :])
         + _dot(pick(b0_ref, b1_ref), wb_ref[D_FOURIER:D_FOURIER + D_HGRN, :])
         + _dot(pick(c0_ref, c1_ref), wb_ref[D_FOURIER + D_HGRN:D_MIX, :]))
    x1 = _ln(DEEPNORM_ALPHA * x_ref[...] + mod_ref[2:3, :] * y) * g_ref[...] + beta_ref[...]
    x1_ref[...] = x1
    h2 = _ln(x1) * (1.0 + mod_ref[4:5, :]) + mod_ref[3:4, :]
    h2_ref[...] = h2
    if route:
        _top2_route(h2, wr_ref, br_ref, idx_ref, p_ref)


def _out_projection(mixed, x, mod_l, ln_g, ln_b, w_out, l, router=None):
    n_ctx = TP // TM
    row = lambda w: pl.BlockSpec((TM, w), lambda i: (i, 0))
    ctx = lambda w: pl.BlockSpec((TM, w), lambda i: (jnp.minimum(i, n_ctx - 1), 0))
    lat = lambda w: pl.BlockSpec((TM, w), lambda i: (jnp.maximum(i - n_ctx, 0), 0))
    vec = pl.BlockSpec((1, D_MODEL), lambda i: (0, 0))
    in_specs = [
        ctx(D_FOURIER), lat(D_FOURIER), ctx(D_HGRN), lat(D_HGRN), ctx(D_ATTN), lat(D_ATTN), row(D_MODEL),
        pl.BlockSpec((None, N_MOD, D_MODEL), lambda i: (_cond_of_tile(i, TM), 0, 0)),
        vec, vec,
        pl.BlockSpec((None, D_MIX, D_MODEL), lambda i: (l, 0, 0), pipeline_mode=pl.Buffered(1)),
    ]
    args = [mixed[0][0], mixed[0][1], mixed[1][0], mixed[1][1], mixed[2][0], mixed[2][1], x, mod_l,
            ln_g.reshape(1, D_MODEL), ln_b.reshape(1, D_MODEL), w_out]
    out_specs = [row(D_MODEL), row(D_MODEL)]
    out_shape = [jax.ShapeDtypeStruct((T, D_MODEL), F32), jax.ShapeDtypeStruct((T, D_MODEL), F32)]
    if router is not None:
        w = jnp.zeros((D_MODEL, LANES), F32).at[:, :N_EXPERTS].set(router[0])
        b = jnp.full((1, LANES), NEG_BIG, F32).at[0, :N_EXPERTS].set(router[1])
        in_specs += [pl.BlockSpec((D_MODEL, LANES), lambda i: (0, 0)), pl.BlockSpec((1, LANES), lambda i: (0, 0))]
        args += [w, b]
        out_specs += [row(LANES), row(LANES)]
        out_shape += [jax.ShapeDtypeStruct((T, LANES), jnp.int32), jax.ShapeDtypeStruct((T, LANES), F32)]
    return pl.pallas_call(
        functools.partial(_outproj_kernel, route=router is not None),
        grid=(T // TM,),
        in_specs=in_specs,
        out_specs=out_specs,
        out_shape=out_shape,
        scratch_shapes=[pltpu.VMEM((D_MIX, D_MODEL), BF16)],
        compiler_params=_cparams(("arbitrary",)),
        name="out_projection_route" if router is not None else "out_projection",
    )(*args)


def _expert_changed(te_ref, i):
    return (i == 0) | (te_ref[i] != te_ref[jnp.maximum(i - 1, 0)])


def _post_ffn(x1, y, g2, ln_g, ln_b):
    return _ln(DEEPNORM_ALPHA * x1 + g2 * y) * ln_g + ln_b


def _ffn_up_kernel(te_ref, nv_ref, x_ref, wg_ref, wu_ref, h_ref, wb_ref):
    i = pl.program_id(1)
    fh = wg_ref.shape[-1]

    @pl.when(_expert_changed(te_ref, i))
    def _():
        wb_ref[:, :fh] = wg_ref[...].astype(BF16)
        wb_ref[:, fh:] = wu_ref[...].astype(BF16)

    @pl.when(i < nv_ref[0])
    def _():
        ab = _dot(x_ref[...].astype(BF16), wb_ref[...])
        h_ref[...] = (_silu(ab[:, :fh]) * ab[:, fh:]).astype(BF16)

    @pl.when(i >= nv_ref[0])
    def _():
        h_ref[...] = jnp.zeros_like(h_ref)


def _ffn_down_kernel(te_ref, nv_ref, h_ref, wd_ref, *rest, norm):
    if norm:
        x1_ref, mod_ref, g_ref, beta_ref, y_ref, wdb_ref = rest
    else:
        y_ref, wdb_ref = rest
    i = pl.program_id(1)

    @pl.when(_expert_changed(te_ref, i))
    def _():
        wdb_ref[...] = wd_ref[...].astype(BF16)

    if norm:
        y = _dot(h_ref[...], wdb_ref[...])
        y_ref[...] = _post_ffn(x1_ref[...], y, mod_ref[5:6, :], g_ref[...], beta_ref[...])
    else:
        @pl.when(i < nv_ref[0])
        def _():
            y_ref[...] = _dot(h_ref[...], wdb_ref[...])

        @pl.when(i >= nv_ref[0])
        def _():
            y_ref[...] = jnp.zeros_like(y_ref)


def _grouped_ffn(x_rows, tile_expert, n_valid, w_gate, w_up, w_down, f_splits, norm_args=None):
    r = x_rows.shape[0]
    nt = r // TM_FFN
    f = w_gate.shape[-1]
    fh = f // f_splits
    assert fh * f_splits == f and fh % LANES == 0
    used = lambda i, nv: jnp.minimum(i, nv[0] - 1)
    h = pl.pallas_call(
        _ffn_up_kernel,
        grid_spec=pltpu.PrefetchScalarGridSpec(
            num_scalar_prefetch=2,
            grid=(f_splits, nt),
            in_specs=[
                pl.BlockSpec((TM_FFN, D_MODEL), lambda j, i, te, nv: (used(i, nv), 0)),
                pl.BlockSpec((None, D_MODEL, fh), lambda j, i, te, nv: (te[i], 0, j)),
                pl.BlockSpec((None, D_MODEL, fh), lambda j, i, te, nv: (te[i], 0, j)),
            ],
            out_specs=pl.BlockSpec((TM_FFN, fh), lambda j, i, te, nv: (i, j)),
            scratch_shapes=[pltpu.VMEM((D_MODEL, 2 * fh), BF16)],
        ),
        out_shape=jax.ShapeDtypeStruct((r, f), BF16),
        compiler_params=_cparams(("arbitrary", "arbitrary")),
        name="ffn_up",
    )(tile_expert, n_valid, x_rows, w_gate, w_up)

    d_splits = 1 if norm_args is not None else 2
    dh = D_MODEL // d_splits
    in_specs = [
        pl.BlockSpec((TM_FFN, f), lambda j, i, te, nv: (used(i, nv), 0)),
        pl.BlockSpec((None, f, dh), lambda j, i, te, nv: (te[i], 0, j)),
    ]
    args = [tile_expert, n_valid, h, w_down]
    if norm_args is not None:
        assert r == T
        x1, mod_l, ln_g, ln_b = norm_args
        vec = pl.BlockSpec((1, D_MODEL), lambda j, i, te, nv: (0, 0))
        in_specs += [
            pl.BlockSpec((TM_FFN, D_MODEL), lambda j, i, te, nv: (i, 0)),
            pl.BlockSpec((None, N_MOD, D_MODEL), lambda j, i, te, nv: (_cond_of_tile(i, TM_FFN), 0, 0)),
            vec, vec]
        args += [x1, mod_l, ln_g.reshape(1, D_MODEL), ln_b.reshape(1, D_MODEL)]
    return pl.pallas_call(
        functools.partial(_ffn_down_kernel, norm=norm_args is not None),
        grid_spec=pltpu.PrefetchScalarGridSpec(
            num_scalar_prefetch=2,
            grid=(d_splits, nt),
            in_specs=in_specs,
            out_specs=pl.BlockSpec((TM_FFN, dh), lambda j, i, te, nv: (i, j)),
            scratch_shapes=[pltpu.VMEM((f, dh), BF16)],
        ),
        out_shape=jax.ShapeDtypeStruct((r, D_MODEL), F32),
        compiler_params=_cparams(("arbitrary", "arbitrary")),
        name="ffn_down_norm" if norm_args is not None else "ffn_down",
    )(*args)


def _row_copy(src, dst, s, d, sem):
    return pltpu.make_async_copy(src.at[pl.ds(s, 1), :], dst.at[pl.ds(d, 1), :], sem)


def _dispatch_kernel(dest_ref, last_ref, x_ref, o_hbm, zero_scr, sem, zsem):
    i = pl.program_id(0)

    @pl.when(i == 0)
    def _():
        zero_scr[...] = jnp.zeros_like(zero_scr)

        def fill(tile):
            r0 = pl.multiple_of(tile * TM_FFN, TM_FFN)
            return pltpu.make_async_copy(zero_scr, o_hbm.at[pl.ds(r0, TM_FFN), :], zsem)

        n_tiles = o_hbm.shape[0] // TM_FFN
        min_tiles = (T * TOP_K) // TM_FFN
        jobs = [(last_ref[e] >= 0, last_ref[e]) for e in range(N_EXPERTS)]
        jobs += [(t >= last_ref[N_EXPERTS], t) for t in range(min_tiles, n_tiles)]
        for go, tile in jobs:
            @pl.when(go)
            def _(tile=tile):
                fill(tile).start()
        for go, tile in jobs:
            @pl.when(go)
            def _(tile=tile):
                fill(tile).wait()

    base = i * TM_LN

    def issue(r, carry):
        for k in range(TOP_K):
            _row_copy(x_ref, o_hbm, r, dest_ref[(base + r) * TOP_K + k], sem).start(priority=k % 2)
        return carry

    lax.fori_loop(0, TM_LN, issue, 0, unroll=8)
    for k in range(TOP_K):
        pltpu.make_async_copy(x_ref, o_hbm.at[pl.ds(0, TM_LN), :], sem).wait()


def _dispatch(h2, dest, last_tile, n_rows):
    return pl.pallas_call(
        _dispatch_kernel,
        grid_spec=pltpu.PrefetchScalarGridSpec(
            num_scalar_prefetch=2,
            grid=(T // TM_LN,),
            in_specs=[pl.BlockSpec((TM_LN, D_MODEL), lambda i, d, l: (i, 0))],
            out_specs=pl.BlockSpec(memory_space=pl.ANY),
            scratch_shapes=[pltpu.VMEM((TM_FFN, D_MODEL), F32), pltpu.SemaphoreType.DMA(()),
                            pltpu.SemaphoreType.DMA(())],
        ),
        out_shape=jax.ShapeDtypeStruct((n_rows, D_MODEL), F32),
        compiler_params=_cparams(("arbitrary",)),
        name="dispatch",
    )(dest, last_tile, h2)


def _combine_kernel(pos_ref, x1_ref, p_ref, y_hbm, mod_ref, g_ref, beta_ref, o_ref, buf, sem):
    i = pl.program_id(0)
    n = pl.num_programs(0)

    def fetch(step, slot):
        def issue(r, carry):
            for k in range(TOP_K):
                src = pos_ref[(step * TM_LN + r) * TOP_K + k]
                _row_copy(y_hbm, buf.at[slot, k], src, r, sem.at[slot, k]).start(priority=k % 2)
            return carry

        lax.fori_loop(0, TM_LN, issue, 0, unroll=8)

    @pl.when(i == 0)
    def _():
        fetch(0, 0)

    slot = i % 2

    @pl.when(i + 1 < n)
    def _():
        fetch(i + 1, 1 - slot)

    for k in range(TOP_K):
        pltpu.make_async_copy(y_hbm.at[pl.ds(0, TM_LN), :], buf.at[slot, k], sem.at[slot, k]).wait()
    y = p_ref[:, 0:1] * buf[slot, 0]
    for k in range(1, TOP_K):
        y = y + p_ref[:, k:k + 1] * buf[slot, k]
    o_ref[...] = _post_ffn(x1_ref[...], y, mod_ref[5:6, :], g_ref[...], beta_ref[...])


def _combine_norm(x1, p, y_rows, pos, mod_l, ln_g, ln_b):
    row = pl.BlockSpec((TM_LN, D_MODEL), lambda i, s: (i, 0))
    vec = pl.BlockSpec((1, D_MODEL), lambda i, s: (0, 0))
    return pl.pallas_call(
        _combine_kernel,
        grid_spec=pltpu.PrefetchScalarGridSpec(
            num_scalar_prefetch=1,
            grid=(T // TM_LN,),
            in_specs=[row, pl.BlockSpec((TM_LN, LANES), lambda i, s: (i, 0)),
                      pl.BlockSpec(memory_space=pl.ANY),
                      pl.BlockSpec((None, N_MOD, D_MODEL), lambda i, s: (_cond_of_tile(i, TM_LN), 0, 0)),
                      vec, vec],
            out_specs=row,
            scratch_shapes=[pltpu.VMEM((2, TOP_K, TM_LN, D_MODEL), F32),
                            pltpu.SemaphoreType.DMA((2, TOP_K))],
        ),
        out_shape=jax.ShapeDtypeStruct((T, D_MODEL), F32),
        compiler_params=_cparams(("arbitrary",)),
        name="combine_norm",
    )(pos, x1, p, y_rows, mod_l, ln_g.reshape(1, D_MODEL), ln_b.reshape(1, D_MODEL))


def _moe_plan(idx):
    n_assign = T * TOP_K
    n_tiles = n_assign // TM_FFN + N_EXPERTS
    e = idx[:, :TOP_K].reshape(n_assign)
    onehot = (e[:, None] == jnp.arange(N_EXPERTS, dtype=jnp.int32)[None, :]).astype(jnp.int32)
    csum = jnp.cumsum(onehot, axis=0)
    counts = csum[-1]
    tiles_e = (counts + TM_FFN - 1) // TM_FFN
    tile_end = jnp.cumsum(tiles_e)
    row0 = (tile_end - tiles_e) * TM_FFN
    dest = jnp.sum((csum - 1 + row0[None, :]) * onehot, axis=1)
    n_valid = tile_end[-1]
    tile_id = jnp.minimum(jnp.arange(n_tiles, dtype=jnp.int32), n_valid - 1)
    tile_expert = jnp.sum((tile_id[:, None] >= tile_end[None, :]).astype(jnp.int32), axis=1)
    last_tile = jnp.concatenate([jnp.where(tiles_e > 0, tile_end - 1, -1), n_valid.reshape(1)])
    return (dest.astype(jnp.int32), tile_expert.astype(jnp.int32), n_valid.reshape(1).astype(jnp.int32),
            last_tile.astype(jnp.int32), n_tiles * TM_FFN)


def _moe_ffn(h2, x1, idx, p, mod_l, ln_g, ln_b, w_gate, w_up, w_down, expert0):
    dest, tile_expert, n_valid, last_tile, n_rows = _moe_plan(idx)
    x_rows = _dispatch(h2, dest, last_tile, n_rows)
    y_rows = _grouped_ffn(x_rows, tile_expert + expert0, n_valid, w_gate, w_up, w_down, 4)
    return _combine_norm(x1, p, y_rows, dest, mod_l, ln_g, ln_b)


def _dense_ffn(h2, x1, mod_l, ln_g, ln_b, w_gate, w_up, w_down, index):
    nt = T // TM_FFN
    return _grouped_ffn(h2, jnp.full((nt,), index, jnp.int32), jnp.full((1,), nt, jnp.int32),
                        w_gate, w_up, w_down, 2, norm_args=(x1, mod_l, ln_g, ln_b))


def kernel(x_prompt, x_sample, c, cache_k, cache_v, state_hgrn, c_ctx, w_mod, b_mod, w_in, w_fourier, lb_logits, hgrn_norm, attn_sink, w_out, ln1_g, ln1_b, ln2_g, ln2_b, ffn_w_gate, ffn_w_up, ffn_w_down, router_w, router_b, moe_w_gate, moe_w_up, moe_w_down):
    lb_sm = jax.nn.softmax(lb_logits.astype(F32), axis=0)
    lower_bounds = jnp.clip(jnp.cumsum(lb_sm, axis=0) - lb_sm[0], 0.0, 1.0)

    cond = jnp.zeros((COND_PAD, D_MODEL), F32).at[0].set(c_ctx).at[1:N_COND].set(c)
    mod = _modulation(cond, w_mod, b_mod).reshape(DEPTH, COND_PAD, N_MOD, D_MODEL)

    ch_tabs = _channel_tables()
    pos_tabs_ctx = _dft_tables(SEQ)
    pos_tabs_lat = _dft_tables(DEC_SEQ)
    cos_t, sin_t = _rope_tables()
    kc = cache_k.reshape(DEC_BATCH, DEPTH, PAST_LEN, D_KV)
    vc = cache_v.reshape(DEC_BATCH, DEPTH, PAST_LEN, D_KV)
    n_moe = moe_w_gate.shape[0]
    moe_wg = moe_w_gate.reshape(n_moe * N_EXPERTS, D_MODEL, D_FF_EXPERT)
    moe_wu = moe_w_up.reshape(n_moe * N_EXPERTS, D_MODEL, D_FF_EXPERT)
    moe_wd = moe_w_down.reshape(n_moe * N_EXPERTS, D_FF_EXPERT, D_MODEL)
    zero_state = jnp.zeros((BATCH, 2, D_HGRN, D_HGRN), F32)

    x = jnp.concatenate([x_prompt.reshape(TP, D_MODEL), x_sample.reshape(TS, D_MODEL)], axis=0)
    new_k, new_v, new_s = [], [], []
    for l in range(DEPTH):
        mod_l = mod[l]
        y_in = _in_projection(x, mod_l, w_in, l)

        lb2 = lower_bounds[l].reshape(2, D_HGRN)
        gw = jnp.tile(hgrn_norm[l], N_HGRN_HEADS).reshape(1, D_HGRN)
        a_ctx = _fourier_mix(y_in, 0, BATCH, SEQ, ch_tabs, pos_tabs_ctx, w_fourier, l)
        a_lat = _fourier_mix(y_in, TP, DEC_BATCH, DEC_SEQ, ch_tabs, pos_tabs_lat, w_fourier, l)
        b_ctx, s_ctx = _hgrn_mix(y_in, 0, BATCH, SEQ, lb2, gw, zero_state)
        b_lat, _ = _hgrn_mix(y_in, TP, DEC_BATCH, DEC_SEQ, lb2, gw,
                             _states_to_kernel_layout(state_hgrn[:, l]))
        c_ctx_out = _context_attention(y_in, attn_sink[l])
        c_lat = _latent_attention(y_in, attn_sink[l], kc, vc, l, cos_t, sin_t)

        k0 = 6 * 256 + D_ATTN
        new_k.append(y_in[:TP, k0:k0 + D_KV].reshape(BATCH, SEQ, N_KV_HEADS, HEAD_DIM))
        new_v.append(y_in[:TP, k0 + D_KV:k0 + 2 * D_KV].reshape(BATCH, SEQ, N_KV_HEADS, HEAD_DIM))
        new_s.append(_states_from_kernel_layout(s_ctx))

        mixed = ((a_ctx, a_lat), (b_ctx, b_lat), (c_ctx_out, c_lat))
        i = l // 2
        if l % 2 == 0:
            x1, h2 = _out_projection(mixed, x, mod_l, ln1_g[l], ln1_b[l], w_out, l)
            x = _dense_ffn(h2, x1, mod_l, ln2_g[l], ln2_b[l], ffn_w_gate, ffn_w_up, ffn_w_down, i)
        else:
            x1, h2, idx, p = _out_projection(mixed, x, mod_l, ln1_g[l], ln1_b[l], w_out, l,
                                             router=(router_w[i], router_b[i]))
            x = _moe_ffn(h2, x1, idx, p, mod_l, ln2_g[l], ln2_b[l], moe_wg, moe_wu, moe_wd, i * N_EXPERTS)

    xp = x[:TP].reshape(BATCH, SEQ, D_MODEL)
    xs = x[TP:].reshape(DEC_BATCH, DEC_SEQ, D_MODEL)
    return (xp, xs, jnp.stack(new_k, axis=1), jnp.stack(new_v, axis=1), jnp.stack(new_s, axis=1))
```

```python
import functools
import math

import jax
import jax.numpy as jnp
import numpy as np
from jax import lax
from jax.experimental import pallas as pl
from jax.experimental.pallas import tpu as pltpu

D_MODEL = 1024
BATCH = 16
SEQ = 256
DEPTH = 4
DEC_BATCH = 2
DEC_SEQ = 2048
PAST_LEN = 512
GRID_W = 64
D_FOURIER = 256
N_FOURIER_GROUPS = 4
FOURIER_GROUP = D_FOURIER // N_FOURIER_GROUPS
D_HGRN = 256
N_HGRN_HEADS = 4
HGRN_HEAD = D_HGRN // N_HGRN_HEADS
HGRN_CHUNK = 32
HGRN_GROUP = 128
HGRN_SAFE_DECAY = 120.0
N_Q_HEADS = 8
N_KV_HEADS = 2
GQA = N_Q_HEADS // N_KV_HEADS
HEAD_DIM = 64
D_ATTN = N_Q_HEADS * HEAD_DIM
D_KV = N_KV_HEADS * HEAD_DIM
D_MIX = D_FOURIER + D_HGRN + D_ATTN
WINDOW = 128
ATTN_BLOCK = 128
ATTN_SCALE = HEAD_DIM ** -0.5
ROPE_BASE = 10000.0
NEG_BIG = -1e30
D_FF = 2816
N_EXPERTS = 8
TOP_K = 2
D_FF_EXPERT = 3584
DEEPNORM_ALPHA = (2 * DEPTH) ** 0.25
LN_EPS = 1e-5
RMS_EPS = 1e-6
N_MOD = 6
D_IN = 6 * 256 + D_ATTN + 2 * D_KV

TP = BATCH * SEQ
TS = DEC_BATCH * DEC_SEQ
T = TP + TS
N_COND = 1 + DEC_BATCH
COND_PAD = 8

TM = 512
TM_FFN = 512
TM_LN = 256
LANES = 128
VMEM_LIMIT = 56 * 1024 * 1024

F32 = jnp.float32
BF16 = jnp.bfloat16


def _cparams(sem, vmem=VMEM_LIMIT):
    return pltpu.CompilerParams(dimension_semantics=sem, vmem_limit_bytes=vmem)


def _ln(x):
    mu = jnp.mean(x, axis=-1, keepdims=True)
    xc = x - mu
    var = jnp.mean(xc * xc, axis=-1, keepdims=True)
    return xc * lax.rsqrt(var + LN_EPS)


def _silu(x):
    return x * jax.nn.sigmoid(x)


def _split3(a):
    p0 = a.astype(BF16)
    r1 = a - p0.astype(F32)
    p1 = r1.astype(BF16)
    r2 = r1 - p1.astype(F32)
    return p0, p1, r2.astype(BF16)


def _split2(a):
    hi = a.astype(BF16)
    return hi, (a - hi.astype(F32)).astype(BF16)


def _dot(a, b):
    return jnp.dot(a, b, preferred_element_type=F32)


def _dot_nt(a, b):
    return lax.dot_general(a, b, (((1,), (1,)), ((), ())), preferred_element_type=F32)


def _dot_tn(a, b):
    return lax.dot_general(a, b, (((0,), (0,)), ((), ())), preferred_element_type=F32)


def _cond_of_tile(i, tm):
    n_ctx = TP // tm
    return jnp.where(i < n_ctx, 0, 1 + (i - n_ctx) // (DEC_SEQ // tm))


def _mod_spec(l, tm, tile_of=lambda i, *_: i):
    return pl.BlockSpec((None, None, N_MOD, D_MODEL), lambda *a: (l, _cond_of_tile(tile_of(*a), tm), 0, 0))


def _layer_row_spec(l, width):
    return pl.BlockSpec((None, 1, width), lambda *_: (l, 0, 0))


def _mod_kernel(c_ref, w_ref, b_ref, o_ref):
    a = _silu(c_ref[...]).astype(BF16)
    o_ref[...] = _dot(a, w_ref[...].astype(BF16)) + b_ref[...]


def _modulation(cond, w_mod, b_mod):
    tn = 2048
    n_out = N_MOD * D_MODEL
    return pl.pallas_call(
        _mod_kernel,
        grid=(DEPTH, n_out // tn),
        in_specs=[
            pl.BlockSpec((COND_PAD, D_MODEL), lambda l, j: (0, 0)),
            pl.BlockSpec((None, D_MODEL, tn), lambda l, j: (l, 0, j)),
            pl.BlockSpec((None, 1, tn), lambda l, j: (l, 0, j)),
        ],
        out_specs=pl.BlockSpec((None, COND_PAD, tn), lambda l, j: (l, 0, j)),
        out_shape=jax.ShapeDtypeStruct((DEPTH, COND_PAD, n_out), F32),
        compiler_params=_cparams(("parallel", "parallel")),
        name="modulation",
    )(cond, w_mod, b_mod.reshape(DEPTH, 1, n_out))


def _inproj_kernel(x_ref, mod_ref, w_ref, o_ref, wb_ref):
    @pl.when(pl.program_id(0) == 0)
    def _():
        wb_ref[...] = w_ref[...].astype(BF16)

    h = _ln(x_ref[...]) * (1.0 + mod_ref[1:2, :]) + mod_ref[0:1, :]
    o_ref[...] = _dot(h.astype(BF16), wb_ref[...])


def _in_projection(x, mod, w_in, l):
    return pl.pallas_call(
        _inproj_kernel,
        grid=(T // TM,),
        in_specs=[
            pl.BlockSpec((TM, D_MODEL), lambda i: (i, 0)),
            _mod_spec(l, TM),
            pl.BlockSpec((None, D_MODEL, D_IN), lambda i: (l, 0, 0), pipeline_mode=pl.Buffered(1)),
        ],
        out_specs=pl.BlockSpec((TM, D_IN), lambda i: (i, 0)),
        out_shape=jax.ShapeDtypeStruct((T, D_IN), F32),
        scratch_shapes=[pltpu.VMEM((D_MODEL, D_IN), BF16)],
        compiler_params=_cparams(("arbitrary",)),
        name="in_projection",
    )(x, mod, w_in)


def _fourier_kernel(u_ref, ch_hi_ref, ch_lo_ref, tab_hi_ref, tab_lo_ref, wf_ref, o_ref,
                    ab_hi_ref, ab_lo_ref, *, scale):
    @pl.when(pl.program_id(1) == 0)
    def _():
        uh, ul = _split2(u_ref[...])
        ch_hi = ch_hi_ref[...]
        ab = _dot(uh, ch_hi) + _dot(uh, ch_lo_ref[...]) + _dot(ul, ch_hi)
        stacked = jnp.concatenate([ab[:, :D_FOURIER], ab[:, D_FOURIER:]], axis=0)
        hi, lo = _split2(stacked)
        ab_hi_ref[...] = hi
        ab_lo_ref[...] = lo

    th = tab_hi_ref[...]
    ab_hi = ab_hi_ref[...]
    z = _dot(th, ab_hi) + _dot(th, ab_lo_ref[...]) + _dot(tab_lo_ref[...], ab_hi)
    z = (z * scale).astype(BF16)
    o_ref[...] = _dot(z, wf_ref[...].astype(BF16)).astype(BF16)


def _dft_tables(n):
    blk = 32
    a = jnp.arange(n, dtype=jnp.int32)[:, None]
    ang1 = ((a * (jnp.arange(n // blk, dtype=jnp.int32) * blk)[None, :]) % n).astype(F32) * (2.0 * math.pi / n)
    ang0 = ((a * jnp.arange(blk, dtype=jnp.int32)[None, :]) % n).astype(F32) * (2.0 * math.pi / n)
    c1, s1 = jnp.cos(ang1)[:, :, None], jnp.sin(ang1)[:, :, None]
    c0, s0 = jnp.cos(ang0)[:, None, :], jnp.sin(ang0)[:, None, :]
    cos = (c1 * c0 - s1 * s0).reshape(n, n)
    sin = (s1 * c0 + c1 * s0).reshape(n, n)
    return _split2(jnp.concatenate([cos, -sin], axis=1))


def _channel_tables():
    a = jnp.arange(D_FOURIER, dtype=jnp.int32)
    same = (a[:, None] // FOURIER_GROUP) == (a[None, :] // FOURIER_GROUP)
    prod = ((a[:, None] % FOURIER_GROUP) * (a[None, :] % FOURIER_GROUP)) % FOURIER_GROUP
    ang = prod.astype(F32) * (2.0 * math.pi / FOURIER_GROUP)
    c = jnp.where(same, jnp.cos(ang), 0.0)
    s = jnp.where(same, jnp.sin(ang), 0.0)
    return _split2(jnp.concatenate([c, s], axis=1))


def _fourier_mix(y_in, row0, nbatch, length, ch_tabs, pos_tabs, w_fourier, l):
    tr = min(length, 256)
    blk0 = row0 // length
    kern = functools.partial(_fourier_kernel, scale=1.0 / math.sqrt(length * FOURIER_GROUP))
    return pl.pallas_call(
        kern,
        grid=(nbatch, length // tr),
        in_specs=[
            pl.BlockSpec((length, D_FOURIER), lambda b, r: (blk0 + b, 0)),
            pl.BlockSpec((D_FOURIER, 2 * D_FOURIER), lambda b, r: (0, 0)),
            pl.BlockSpec((D_FOURIER, 2 * D_FOURIER), lambda b, r: (0, 0)),
            pl.BlockSpec((tr, 2 * length), lambda b, r: (r, 0)),
            pl.BlockSpec((tr, 2 * length), lambda b, r: (r, 0)),
            pl.BlockSpec((None, D_FOURIER, D_FOURIER), lambda b, r: (l, 0, 0)),
        ],
        out_specs=pl.BlockSpec((tr, D_FOURIER), lambda b, r: (b * (length // tr) + r, 0)),
        out_shape=jax.ShapeDtypeStruct((nbatch * length, D_FOURIER), BF16),
        scratch_shapes=[pltpu.VMEM((2 * length, D_FOURIER), BF16),
                        pltpu.VMEM((2 * length, D_FOURIER), BF16)],
        compiler_params=_cparams(("parallel", "arbitrary")),
        name=f"fourier_mix_{length}",
    )(y_in, ch_tabs[0], ch_tabs[1], pos_tabs[0], pos_tabs[1], w_fourier)


def _hgrn_pairwise_scan(d, nc, g_scr, k_scr, q_scr, v_ref, o_scr, st_scr, head_mask, ones_bd):
    c = HGRN_CHUNK
    last = c - 1 if d == 0 else 0

    def chunk(ci, carry):
        row = lax.broadcasted_iota(jnp.int32, (c, c), 0)
        col = lax.broadcasted_iota(jnp.int32, (c, c), 1)
        tri = jnp.where((col <= row) if d == 0 else (col >= row), 1.0, 0.0).astype(BF16)
        cc = ci if d == 0 else nc - 1 - ci
        rows = pl.ds(pl.multiple_of(cc * c, c), c)
        g = g_scr[rows, :]
        kk = k_scr[rows, :]
        q = q_scr[rows, :]
        v = v_ref[rows, :]
        g0, g1, g2 = _split3(g)
        b = _dot(tri, g0) + _dot(tri, g1) + _dot(tri, g2)
        btot = b[last:last + 1, :]
        st = st_scr[...]
        o_inter = _dot_nt((q * jnp.exp(b)).astype(BF16), st.astype(BF16))
        s_idx = lax.broadcasted_iota(jnp.int32, (c, c, D_HGRN), 0)
        t_idx = lax.broadcasted_iota(jnp.int32, (c, c, D_HGRN), 1)
        causal = (s_idx <= t_idx) if d == 0 else (s_idx >= t_idx)
        diff = b[None, :, :] - b[:, None, :]
        pair = q[None, :, :] * jnp.exp(jnp.minimum(diff, 0.0)) * kk[:, None, :]
        pair = jnp.where(causal, pair, 0.0).astype(BF16)
        attn = _dot(pair.reshape(c * c, D_HGRN), ones_bd).reshape(c, c, D_HGRN)
        o = o_inter + jnp.sum(attn * v[:, None, :], axis=0)
        if d == 0:
            o_scr[rows, :] = o
        else:
            o_scr[rows, :] = o_scr[rows, :] + o
        kd = kk * jnp.exp(btot - b)
        upd = _dot_tn(v.astype(BF16), kd.astype(BF16))
        st_scr[...] = jnp.exp(btot) * st + jnp.where(head_mask, upd, 0.0)
        return carry

    lax.fori_loop(0, nc, chunk, 0)


def _hgrn_factored_scan(d, nc, g_scr, k_scr, q_scr, v_ref, o_scr, st_scr, qd_scr, kd_scr, dec_scr, stb_scr,
                        head_mask):
    c = HGRN_CHUNK
    grp = HGRN_GROUP
    length = nc * c

    def group(gi, carry):
        rows = pl.ds(pl.multiple_of(gi * grp, grp), grp)
        r = lax.broadcasted_iota(jnp.int32, (grp, grp), 0)
        s = lax.broadcasted_iota(jnp.int32, (grp, grp), 1)
        same_chunk = (r // c) == (s // c)
        ordered = (s <= r) if d == 0 else (s >= r)
        tri = jnp.where(same_chunk & ordered, 1.0, 0.0).astype(BF16)
        ones_chunk = jnp.where(same_chunk, 1.0, 0.0).astype(BF16)
        g0, g1, g2 = _split3(g_scr[rows, :])
        b = _dot(tri, g0) + _dot(tri, g1) + _dot(tri, g2)
        tot = _dot(ones_chunk, g0) + _dot(ones_chunk, g1) + _dot(ones_chunk, g2)
        kk = k_scr[rows, :]
        q = q_scr[rows, :]
        half = 0.5 * tot
        qc = (q * jnp.exp(b - half)).astype(BF16)
        kh = (kk * jnp.exp(half - b)).astype(BF16)
        qd_scr[rows, :] = (q * jnp.exp(b)).astype(BF16)
        kd_scr[rows, :] = (kk * jnp.exp(tot - b)).astype(BF16)
        dec_scr[rows, :] = jnp.exp(tot)
        hs_row = lax.broadcasted_iota(jnp.int32, (N_HGRN_HEADS * grp, D_HGRN), 0) // grp
        hs_col = lax.broadcasted_iota(jnp.int32, (N_HGRN_HEADS * grp, D_HGRN), 1) // HGRN_HEAD
        same_head = hs_row == hs_col
        k_bd = jnp.where(same_head, jnp.concatenate([kh] * N_HGRN_HEADS, axis=0), 0.0)
        v_bd = jnp.where(same_head, jnp.concatenate([v_ref[rows, :].astype(BF16)] * N_HGRN_HEADS, axis=0), 0.0)
        t_idx = lax.broadcasted_iota(jnp.int32, (grp, N_HGRN_HEADS * grp), 0)
        s_idx = lax.broadcasted_iota(jnp.int32, (grp, N_HGRN_HEADS * grp), 1) % grp
        keep = ((t_idx // c) == (s_idx // c)) & ((s_idx <= t_idx) if d == 0 else (s_idx >= t_idx))
        attn = jnp.where(keep, _dot_nt(qc, k_bd), 0.0)
        o_intra = _dot(attn.astype(BF16), v_bd)
        if d == 0:
            o_scr[rows, :] = o_intra
        else:
            o_scr[rows, :] = o_scr[rows, :] + o_intra
        return carry

    lax.fori_loop(0, length // grp, group, 0, unroll=2)

    def state_step(ci, carry):
        cc = ci if d == 0 else nc - 1 - ci
        r0 = pl.multiple_of(cc * c, c)
        rows = pl.ds(r0, c)
        st = st_scr[...]
        stb_scr[cc] = st.astype(BF16)
        upd = _dot_tn(v_ref[rows, :].astype(BF16), kd_scr[rows, :])
        st_scr[...] = dec_scr[pl.ds(r0, 1), :] * st + jnp.where(head_mask, upd, 0.0)
        return carry

    lax.fori_loop(0, nc, state_step, 0, unroll=4)

    def inter(ci, carry):
        rows = pl.ds(pl.multiple_of(ci * c, c), c)
        o_scr[rows, :] = o_scr[rows, :] + _dot_nt(qd_scr[rows, :], stb_scr[ci])
        return carry

    lax.fori_loop(0, nc, inter, 0, unroll=4)


def _hgrn_kernel(hq_ref, hff_ref, hfb_ref, hi_ref, hg_ref, lb_ref, gw_ref, s0_ref,
                 o_ref, sfin_ref, q_scr, g_scr, k_scr, o_scr, dec_scr, qd_scr, kd_scr, st_scr, stb_scr,
                 *, length):
    c = HGRN_CHUNK
    nc = length // c
    q_scr[...] = _silu(hq_ref[...])
    r256 = lax.broadcasted_iota(jnp.int32, (D_HGRN, D_HGRN), 0) // HGRN_HEAD
    c256 = lax.broadcasted_iota(jnp.int32, (D_HGRN, D_HGRN), 1) // HGRN_HEAD
    head_mask = r256 == c256
    ones_bd = jnp.where(head_mask, 1.0, 0.0).astype(BF16)

    for d in range(2):
        z_ref = hff_ref if d == 0 else hfb_ref
        one_minus_f = (1.0 - lb_ref[d:d + 1, :]) * jax.nn.sigmoid(-z_ref[...])
        g_scr[...] = jnp.log1p(-one_minus_f)
        k_scr[...] = one_minus_f
        st_scr[...] = s0_ref[d]
        chunk_decay = jnp.sum(g_scr[...].reshape(nc, c, D_HGRN), axis=1)
        safe = jnp.min(chunk_decay) >= -HGRN_SAFE_DECAY
        lax.cond(
            safe,
            functools.partial(_hgrn_factored_scan, d, nc, g_scr, k_scr, q_scr, hi_ref, o_scr, st_scr,
                              qd_scr, kd_scr, dec_scr, stb_scr, head_mask),
            functools.partial(_hgrn_pairwise_scan, d, nc, g_scr, k_scr, q_scr, hi_ref, o_scr, st_scr,
                              head_mask, ones_bd))
        sfin_ref[d] = st_scr[...]

    o = o_scr[...]
    s0p, s1p, s2p = _split3(o * o)
    ms = (_dot(s0p, ones_bd) + _dot(s1p, ones_bd) + _dot(s2p, ones_bd)) * (1.0 / HGRN_HEAD)
    y = o * lax.rsqrt(ms + RMS_EPS) * gw_ref[...] * _silu(hg_ref[...])
    o_ref[...] = y.astype(BF16)


def _hgrn_mix(y_in, row0, nbatch, length, lb, gw, s0, l, s0_layer):
    blk0 = row0 // length
    col = lambda j: (lambda b: (blk0 + b, j))
    kern = functools.partial(_hgrn_kernel, length=length)
    return pl.pallas_call(
        kern,
        grid=(nbatch,),
        in_specs=[
            pl.BlockSpec((length, D_HGRN), col(1)),
            pl.BlockSpec((length, D_HGRN), col(2)),
            pl.BlockSpec((length, D_HGRN), col(3)),
            pl.BlockSpec((length, D_HGRN), col(4)),
            pl.BlockSpec((length, D_HGRN), col(5)),
            pl.BlockSpec((None, 2, D_HGRN), lambda b: (l, 0, 0)),
            _layer_row_spec(l, D_HGRN),
            pl.BlockSpec((None, None, 2, D_HGRN, D_HGRN), lambda b: (b, s0_layer, 0, 0, 0)),
        ],
        out_specs=[
            pl.BlockSpec((length, D_HGRN), lambda b: (b, 0)),
            pl.BlockSpec((None, 2, D_HGRN, D_HGRN), lambda b: (b, 0, 0, 0)),
        ],
        out_shape=[
            jax.ShapeDtypeStruct((nbatch * length, D_HGRN), BF16),
            jax.ShapeDtypeStruct((nbatch, 2, D_HGRN, D_HGRN), F32),
        ],
        scratch_shapes=[pltpu.VMEM((length, D_HGRN), F32) for _ in range(5)]
        + [pltpu.VMEM((length, D_HGRN), BF16) for _ in range(2)]
        + [pltpu.VMEM((D_HGRN, D_HGRN), F32), pltpu.VMEM((length // HGRN_CHUNK, D_HGRN, D_HGRN), BF16)],
        compiler_params=_cparams(("parallel",)),
        name=f"hgrn_mix_{length}",
    )(y_in, y_in, y_in, y_in, y_in, lb, gw, s0)


def _states_to_kernel_layout(s):
    st = jnp.swapaxes(s, -1, -2)
    eye = jnp.eye(N_HGRN_HEADS, dtype=s.dtype)
    full = jnp.einsum("...hvk,hg->...hvgk", st, eye)
    return full.reshape(s.shape[:-3] + (D_HGRN, D_HGRN))


def _states_from_kernel_layout(st):
    full = st.reshape(st.shape[:-2] + (N_HGRN_HEADS, HGRN_HEAD, N_HGRN_HEADS, HGRN_HEAD))
    diag = jnp.stack([full[..., h, :, h, :] for h in range(N_HGRN_HEADS)], axis=-3)
    return jnp.swapaxes(diag, -1, -2)


def _group_attention(q_heads, sinks, kv_parts):
    rows = q_heads[0].shape[0]
    qg = jnp.concatenate([(q * ATTN_SCALE).astype(BF16) for q in q_heads], axis=0)
    head = lax.broadcasted_iota(jnp.int32, (len(q_heads) * rows, 1), 0) // rows
    sink = jnp.full(head.shape, sinks[0], F32)
    for i in range(1, len(q_heads)):
        sink = jnp.where(head == i, sinks[i], sink)
    m = sink
    scores = []
    for k, _, mask in kv_parts:
        s = _dot_nt(qg, k)
        if mask is not None:
            s = jnp.where(mask, s, NEG_BIG)
        m = jnp.maximum(m, jnp.max(s, axis=1, keepdims=True))
        scores.append(s)
    den = jnp.exp(sink - m)
    out = None
    for s, (_, v, _) in zip(scores, kv_parts):
        e = jnp.exp(s - m)
        den = den + jnp.sum(e, axis=1, keepdims=True)
        o = _dot(e.astype(BF16), v)
        out = o if out is None else out + o
    out = out * (1.0 / den)
    return [out[i * rows:(i + 1) * rows, :] for i in range(len(q_heads))]


def _ctx_attn_kernel(sink_ref, q_ref, k_ref, v_ref, o_ref, *, layer):
    outs = []
    for g in range(N_KV_HEADS):
        sl = slice(g * HEAD_DIM, (g + 1) * HEAD_DIM)
        heads = range(g * GQA, (g + 1) * GQA)
        outs += _group_attention(
            [q_ref[:, h * HEAD_DIM:(h + 1) * HEAD_DIM] for h in heads],
            [sink_ref[layer, h] for h in heads],
            [(k_ref[:, sl].astype(BF16), v_ref[:, sl].astype(BF16), None)])
    o_ref[...] = jnp.concatenate(outs, axis=1).astype(BF16)


def _context_attention(y_in, sink, l):
    qcol = (6 * 256) // D_ATTN
    kcol = (6 * 256 + D_ATTN) // D_KV
    return pl.pallas_call(
        functools.partial(_ctx_attn_kernel, layer=l),
        grid=(BATCH,),
        in_specs=[
            pl.BlockSpec(memory_space=pltpu.SMEM),
            pl.BlockSpec((SEQ, D_ATTN), lambda b: (b, qcol)),
            pl.BlockSpec((SEQ, D_KV), lambda b: (b, kcol)),
            pl.BlockSpec((SEQ, D_KV), lambda b: (b, kcol + 1)),
        ],
        out_specs=pl.BlockSpec((SEQ, D_ATTN), lambda b: (b, 0)),
        out_shape=jax.ShapeDtypeStruct((TP, D_ATTN), BF16),
        compiler_params=_cparams(("parallel",)),
        name="context_attention",
    )(sink, y_in, y_in, y_in)


def _rope(x, cos, sin):
    lane = lax.broadcasted_iota(jnp.int32, x.shape, 1)
    n_freq = HEAD_DIM // 4
    first = (lane % (2 * n_freq)) < n_freq
    swapped = jnp.where(first, pltpu.roll(x, LANES - n_freq, axis=1), pltpu.roll(x, n_freq, axis=1))
    return x * cos + swapped * sin


def _lat_attn_kernel(sink_ref, q_ref, k_ref, v_ref, kc_ref, vc_ref, cos_ref, sin_ref, o_ref, *, layer):
    j = pl.program_id(1)
    nb = DEC_SEQ // ATTN_BLOCK
    blk = ATTN_BLOCK
    q0 = pl.multiple_of(j * blk, blk)
    cos_q = cos_ref[pl.ds(q0, blk), :]
    sin_q = sin_ref[pl.ds(q0, blk), :]

    starts = [jnp.maximum(j - 1, 0), j, jnp.minimum(j + 1, nb - 1)]
    k_band, v_band = [], []
    for st in starts:
        r0 = pl.multiple_of(st * blk, blk)
        kb = _rope(k_ref[pl.ds(r0, blk), :], cos_ref[pl.ds(r0, blk), :], sin_ref[pl.ds(r0, blk), :])
        k_band.append(kb)
        v_band.append(v_ref[pl.ds(r0, blk), :])
    k_loc = jnp.concatenate(k_band, axis=0)
    v_loc = jnp.concatenate(v_band, axis=0)
    r = lax.broadcasted_iota(jnp.int32, (GQA * blk, 3 * blk), 0) % blk
    cidx = lax.broadcasted_iota(jnp.int32, (GQA * blk, 3 * blk), 1)
    s_pos = (j - 1) * blk + cidx
    valid = (cidx >= r) & (cidx <= r + 2 * WINDOW) & (s_pos >= 0) & (s_pos < DEC_SEQ)

    heads_per_chunk = LANES // HEAD_DIM
    q_chunks = [_rope(q_ref[:, cg * LANES:(cg + 1) * LANES], cos_q, sin_q)
                for cg in range(N_Q_HEADS // heads_per_chunk)]

    def q_head(h):
        off = (h % heads_per_chunk) * HEAD_DIM
        return q_chunks[h // heads_per_chunk][:, off:off + HEAD_DIM]

    outs = []
    for g in range(N_KV_HEADS):
        sl = slice(g * HEAD_DIM, (g + 1) * HEAD_DIM)
        heads = range(g * GQA, (g + 1) * GQA)
        outs += _group_attention(
            [q_head(h) for h in heads],
            [sink_ref[layer, h] for h in heads],
            [(k_loc[:, sl].astype(BF16), v_loc[:, sl].astype(BF16), valid),
             (kc_ref[:, sl].astype(BF16), vc_ref[:, sl].astype(BF16), None)])
    o_ref[...] = jnp.concatenate(outs, axis=1).astype(BF16)


def _latent_attention(y_in, sink, kc, vc, l, cos_t, sin_t):
    nb = DEC_SEQ // ATTN_BLOCK
    qrow0 = TP // ATTN_BLOCK
    krow0 = TP // DEC_SEQ
    qcol = (6 * 256) // D_ATTN
    kcol = (6 * 256 + D_ATTN) // D_KV
    return pl.pallas_call(
        functools.partial(_lat_attn_kernel, layer=l),
        grid=(DEC_BATCH, nb),
        in_specs=[
            pl.BlockSpec(memory_space=pltpu.SMEM),
            pl.BlockSpec((ATTN_BLOCK, D_ATTN), lambda b, j: (qrow0 + b * nb + j, qcol)),
            pl.BlockSpec((DEC_SEQ, D_KV), lambda b, j: (krow0 + b, kcol)),
            pl.BlockSpec((DEC_SEQ, D_KV), lambda b, j: (krow0 + b, kcol + 1)),
            pl.BlockSpec((None, None, PAST_LEN, D_KV), lambda b, j: (b, l, 0, 0)),
            pl.BlockSpec((None, None, PAST_LEN, D_KV), lambda b, j: (b, l, 0, 0)),
            pl.BlockSpec((DEC_SEQ, LANES), lambda b, j: (0, 0)),
            pl.BlockSpec((DEC_SEQ, LANES), lambda b, j: (0, 0)),
        ],
        out_specs=pl.BlockSpec((ATTN_BLOCK, D_ATTN), lambda b, j: (b * nb + j, 0)),
        out_shape=jax.ShapeDtypeStruct((TS, D_ATTN), BF16),
        compiler_params=_cparams(("parallel", "parallel")),
        name="latent_attention",
    )(sink, y_in, y_in, y_in, kc, vc, cos_t, sin_t)


def _rope_tables():
    t = jnp.arange(DEC_SEQ)
    rows = (t // GRID_W).astype(F32)
    cols = (t % GRID_W).astype(F32)
    n_freq = HEAD_DIM // 4
    inv = ROPE_BASE ** (-jnp.arange(n_freq, dtype=F32) / n_freq)
    ar = rows[:, None] * inv
    ac = cols[:, None] * inv
    cos = jnp.concatenate([jnp.cos(ar), jnp.cos(ar), jnp.cos(ac), jnp.cos(ac)], axis=1)
    sin = jnp.concatenate([-jnp.sin(ar), jnp.sin(ar), -jnp.sin(ac), jnp.sin(ac)], axis=1)
    return jnp.tile(cos, (1, N_KV_HEADS)), jnp.tile(sin, (1, N_KV_HEADS))


def _top2_route(h2, w_ref, b_ref, idx_ref, p_ref):
    logits = _dot(h2.astype(BF16), w_ref[...].astype(BF16)) + b_ref[...]
    lane = lax.broadcasted_iota(jnp.int32, logits.shape, 1)
    m1 = jnp.max(logits, axis=1, keepdims=True)
    i1 = jnp.min(jnp.where(logits == m1, lane, LANES), axis=1, keepdims=True)
    rest = jnp.where(lane == i1, -jnp.inf, logits)
    m2 = jnp.max(rest, axis=1, keepdims=True)
    i2 = jnp.min(jnp.where(rest == m2, lane, LANES), axis=1, keepdims=True)
    e = jnp.exp(m2 - m1)
    inv = 1.0 / (1.0 + e)
    idx_ref[...] = jnp.where(lane == 0, i1, jnp.where(lane == 1, i2, 0))
    p_ref[...] = jnp.where(lane == 0, inv, jnp.where(lane == 1, e * inv, 0.0))


def _outproj_kernel(a0_ref, a1_ref, b0_ref, b1_ref, c0_ref, c1_ref, x_ref, mod_ref, g_ref, beta_ref, w_ref,
                    *rest, route):
    if route:
        wr_ref, br_ref, x1_ref, h2_ref, idx_ref, p_ref, wb_ref = rest
    else:
        x1_ref, h2_ref, wb_ref = rest

    @pl.when(pl.program_id(0) == 0)
    def _():
        wb_ref[...] = w_ref[...].astype(BF16)

    is_ctx = pl.program_id(0) < TP // TM
    pick = lambda r0, r1: jnp.where(is_ctx, r0[...], r1[...])
    y = (_dot(pick(a0_ref, a1_ref), wb_ref[0:D_FOURIER, :])
         + _dot(pick(b0_ref, b1_ref), wb_ref[D_FOURIER:D_FOURIER + D_HGRN, :])
         + _dot(pick(c0_ref, c1_ref), wb_ref[D_FOURIER + D_HGRN:D_MIX, :]))
    x1 = _ln(DEEPNORM_ALPHA * x_ref[...] + mod_ref[2:3, :] * y) * g_ref[...] + beta_ref[...]
    x1_ref[...] = x1
    h2 = _ln(x1) * (1.0 + mod_ref[4:5, :]) + mod_ref[3:4, :]
    h2_ref[...] = h2
    if route:
        _top2_route(h2, wr_ref, br_ref, idx_ref, p_ref)


def _out_projection(mixed, x, mod, ln_g, ln_b, w_out, l, router=None):
    n_ctx = TP // TM
    row = lambda w: pl.BlockSpec((TM, w), lambda i: (i, 0))
    ctx = lambda w: pl.BlockSpec((TM, w), lambda i: (jnp.minimum(i, n_ctx - 1), 0))
    lat = lambda w: pl.BlockSpec((TM, w), lambda i: (jnp.maximum(i - n_ctx, 0), 0))
    vec = _layer_row_spec(l, D_MODEL)
    in_specs = [
        ctx(D_FOURIER), lat(D_FOURIER), ctx(D_HGRN), lat(D_HGRN), ctx(D_ATTN), lat(D_ATTN), row(D_MODEL),
        _mod_spec(l, TM),
        vec, vec,
        pl.BlockSpec((None, D_MIX, D_MODEL), lambda i: (l, 0, 0), pipeline_mode=pl.Buffered(1)),
    ]
    args = [mixed[0][0], mixed[0][1], mixed[1][0], mixed[1][1], mixed[2][0], mixed[2][1], x, mod, ln_g, ln_b,
            w_out]
    out_specs = [row(D_MODEL), row(D_MODEL)]
    out_shape = [jax.ShapeDtypeStruct((T, D_MODEL), F32), jax.ShapeDtypeStruct((T, D_MODEL), F32)]
    if router is not None:
        w, b, ri = router
        in_specs += [pl.BlockSpec((None, D_MODEL, LANES), lambda i: (ri, 0, 0)), _layer_row_spec(ri, LANES)]
        args += [w, b]
        out_specs += [row(LANES), row(LANES)]
        out_shape += [jax.ShapeDtypeStruct((T, LANES), jnp.int32), jax.ShapeDtypeStruct((T, LANES), F32)]
    return pl.pallas_call(
        functools.partial(_outproj_kernel, route=router is not None),
        grid=(T // TM,),
        in_specs=in_specs,
        out_specs=out_specs,
        out_shape=out_shape,
        scratch_shapes=[pltpu.VMEM((D_MIX, D_MODEL), BF16)],
        compiler_params=_cparams(("arbitrary",)),
        name="out_projection_route" if router is not None else "out_projection",
    )(*args)


def _expert_changed(te_ref, i):
    return (i == 0) | (te_ref[i] != te_ref[jnp.maximum(i - 1, 0)])


def _post_ffn(x1, y, g2, ln_g, ln_b):
    return _ln(DEEPNORM_ALPHA * x1 + g2 * y) * ln_g + ln_b


def _ffn_up_kernel(te_ref, nv_ref, x_ref, wg_ref, wu_ref, h_ref, wb_ref):
    i = pl.program_id(1)
    fh = wg_ref.shape[-1]

    @pl.when(_expert_changed(te_ref, i))
    def _():
        wb_ref[:, :fh] = wg_ref[...].astype(BF16)
        wb_ref[:, fh:] = wu_ref[...].astype(BF16)

    @pl.when(i < nv_ref[0])
    def _():
        ab = _dot(x_ref[...].astype(BF16), wb_ref[...])
        h_ref[...] = (_silu(ab[:, :fh]) * ab[:, fh:]).astype(BF16)

    @pl.when(i >= nv_ref[0])
    def _():
        h_ref[...] = jnp.zeros_like(h_ref)


def _ffn_down_kernel(te_ref, nv_ref, h_ref, wd_ref, *rest, norm):
    if norm:
        x1_ref, mod_ref, g_ref, beta_ref, y_ref, wdb_ref = rest
    else:
        y_ref, wdb_ref = rest
    i = pl.program_id(1)

    @pl.when(_expert_changed(te_ref, i))
    def _():
        wdb_ref[...] = wd_ref[...].astype(BF16)

    if norm:
        y = _dot(h_ref[...], wdb_ref[...])
        y_ref[...] = _post_ffn(x1_ref[...], y, mod_ref[5:6, :], g_ref[...], beta_ref[...])
    else:
        @pl.when(i < nv_ref[0])
        def _():
            y_ref[...] = _dot(h_ref[...], wdb_ref[...])

        @pl.when(i >= nv_ref[0])
        def _():
            y_ref[...] = jnp.zeros_like(y_ref)


def _grouped_ffn(x_rows, tile_expert, n_valid, w_gate, w_up, w_down, f_splits, norm_args=None):
    r = x_rows.shape[0]
    nt = r // TM_FFN
    f = w_gate.shape[-1]
    fh = f // f_splits
    assert fh * f_splits == f and fh % LANES == 0
    used = lambda i, nv: jnp.minimum(i, nv[0] - 1)
    h = pl.pallas_call(
        _ffn_up_kernel,
        grid_spec=pltpu.PrefetchScalarGridSpec(
            num_scalar_prefetch=2,
            grid=(f_splits, nt),
            in_specs=[
                pl.BlockSpec((TM_FFN, D_MODEL), lambda j, i, te, nv: (used(i, nv), 0)),
                pl.BlockSpec((None, D_MODEL, fh), lambda j, i, te, nv: (te[i], 0, j)),
                pl.BlockSpec((None, D_MODEL, fh), lambda j, i, te, nv: (te[i], 0, j)),
            ],
            out_specs=pl.BlockSpec((TM_FFN, fh), lambda j, i, te, nv: (i, j)),
            scratch_shapes=[pltpu.VMEM((D_MODEL, 2 * fh), BF16)],
        ),
        out_shape=jax.ShapeDtypeStruct((r, f), BF16),
        compiler_params=_cparams(("arbitrary", "arbitrary")),
        name="ffn_up",
    )(tile_expert, n_valid, x_rows, w_gate, w_up)

    d_splits = 1 if norm_args is not None else 2
    dh = D_MODEL // d_splits
    in_specs = [
        pl.BlockSpec((TM_FFN, f), lambda j, i, te, nv: (used(i, nv), 0)),
        pl.BlockSpec((None, f, dh), lambda j, i, te, nv: (te[i], 0, j)),
    ]
    args = [tile_expert, n_valid, h, w_down]
    if norm_args is not None:
        assert r == T
        x1, mod, ln_g, ln_b, l = norm_args
        vec = _layer_row_spec(l, D_MODEL)
        in_specs += [
            pl.BlockSpec((TM_FFN, D_MODEL), lambda j, i, te, nv: (i, 0)),
            _mod_spec(l, TM_FFN, lambda j, i, te, nv: i),
            vec, vec]
        args += [x1, mod, ln_g, ln_b]
    return pl.pallas_call(
        functools.partial(_ffn_down_kernel, norm=norm_args is not None),
        grid_spec=pltpu.PrefetchScalarGridSpec(
            num_scalar_prefetch=2,
            grid=(d_splits, nt),
            in_specs=in_specs,
            out_specs=pl.BlockSpec((TM_FFN, dh), lambda j, i, te, nv: (i, j)),
            scratch_shapes=[pltpu.VMEM((f, dh), BF16)],
        ),
        out_shape=jax.ShapeDtypeStruct((r, D_MODEL), F32),
        compiler_params=_cparams(("arbitrary", "arbitrary")),
        name="ffn_down_norm" if norm_args is not None else "ffn_down",
    )(*args)


def _row_copy(src, dst, s, d, sem):
    return pltpu.make_async_copy(src.at[pl.ds(s, 1), :], dst.at[pl.ds(d, 1), :], sem)


def _dispatch_kernel(dest_ref, last_ref, x_ref, o_hbm, zero_scr, sem, zsem):
    i = pl.program_id(0)

    @pl.when(i == 0)
    def _():
        zero_scr[...] = jnp.zeros_like(zero_scr)

        def fill(tile):
            r0 = pl.multiple_of(tile * TM_FFN, TM_FFN)
            return pltpu.make_async_copy(zero_scr, o_hbm.at[pl.ds(r0, TM_FFN), :], zsem)

        n_tiles = o_hbm.shape[0] // TM_FFN
        min_tiles = (T * TOP_K) // TM_FFN
        jobs = [(last_ref[e] >= 0, last_ref[e]) for e in range(N_EXPERTS)]
        jobs += [(t >= last_ref[N_EXPERTS], t) for t in range(min_tiles, n_tiles)]
        for go, tile in jobs:
            @pl.when(go)
            def _(tile=tile):
                fill(tile).start()
        for go, tile in jobs:
            @pl.when(go)
            def _(tile=tile):
                fill(tile).wait()

    base = i * TM_LN

    def issue(r, carry):
        for k in range(TOP_K):
            _row_copy(x_ref, o_hbm, r, dest_ref[(base + r) * TOP_K + k], sem).start(priority=k % 2)
        return carry

    lax.fori_loop(0, TM_LN, issue, 0, unroll=8)
    for k in range(TOP_K):
        pltpu.make_async_copy(x_ref, o_hbm.at[pl.ds(0, TM_LN), :], sem).wait()


def _dispatch(h2, dest, last_tile, n_rows):
    return pl.pallas_call(
        _dispatch_kernel,
        grid_spec=pltpu.PrefetchScalarGridSpec(
            num_scalar_prefetch=2,
            grid=(T // TM_LN,),
            in_specs=[pl.BlockSpec((TM_LN, D_MODEL), lambda i, d, l: (i, 0))],
            out_specs=pl.BlockSpec(memory_space=pl.ANY),
            scratch_shapes=[pltpu.VMEM((TM_FFN, D_MODEL), F32), pltpu.SemaphoreType.DMA(()),
                            pltpu.SemaphoreType.DMA(())],
        ),
        out_shape=jax.ShapeDtypeStruct((n_rows, D_MODEL), F32),
        compiler_params=_cparams(("arbitrary",)),
        name="dispatch",
    )(dest, last_tile, h2)


def _combine_kernel(pos_ref, x1_ref, p_ref, y_hbm, mod_ref, g_ref, beta_ref, o_ref, buf, sem):
    i = pl.program_id(0)
    n = pl.num_programs(0)

    def fetch(step, slot):
        def issue(r, carry):
            for k in range(TOP_K):
                src = pos_ref[(step * TM_LN + r) * TOP_K + k]
                _row_copy(y_hbm, buf.at[slot, k], src, r, sem.at[slot, k]).start(priority=k % 2)
            return carry

        lax.fori_loop(0, TM_LN, issue, 0, unroll=8)

    @pl.when(i == 0)
    def _():
        fetch(0, 0)

    slot = i % 2

    @pl.when(i + 1 < n)
    def _():
        fetch(i + 1, 1 - slot)

    for k in range(TOP_K):
        pltpu.make_async_copy(y_hbm.at[pl.ds(0, TM_LN), :], buf.at[slot, k], sem.at[slot, k]).wait()
    y = p_ref[:, 0:1] * buf[slot, 0]
    for k in range(1, TOP_K):
        y = y + p_ref[:, k:k + 1] * buf[slot, k]
    o_ref[...] = _post_ffn(x1_ref[...], y, mod_ref[5:6, :], g_ref[...], beta_ref[...])


def _combine_norm(x1, p, y_rows, pos, mod, ln_g, ln_b, l):
    row = pl.BlockSpec((TM_LN, D_MODEL), lambda i, s: (i, 0))
    vec = _layer_row_spec(l, D_MODEL)
    return pl.pallas_call(
        _combine_kernel,
        grid_spec=pltpu.PrefetchScalarGridSpec(
            num_scalar_prefetch=1,
            grid=(T // TM_LN,),
            in_specs=[row, pl.BlockSpec((TM_LN, LANES), lambda i, s: (i, 0)),
                      pl.BlockSpec(memory_space=pl.ANY),
                      _mod_spec(l, TM_LN),
                      vec, vec],
            out_specs=row,
            scratch_shapes=[pltpu.VMEM((2, TOP_K, TM_LN, D_MODEL), F32),
                            pltpu.SemaphoreType.DMA((2, TOP_K))],
        ),
        out_shape=jax.ShapeDtypeStruct((T, D_MODEL), F32),
        compiler_params=_cparams(("arbitrary",)),
        name="combine_norm",
    )(pos, x1, p, y_rows, mod, ln_g, ln_b)


def _moe_plan(idx):
    n_assign = T * TOP_K
    n_tiles = n_assign // TM_FFN + N_EXPERTS
    e = idx[:, :TOP_K].reshape(n_assign)
    onehot = (e[:, None] == jnp.arange(N_EXPERTS, dtype=jnp.int32)[None, :]).astype(jnp.int32)
    csum = jnp.cumsum(onehot, axis=0)
    counts = csum[-1]
    tiles_e = (counts + TM_FFN - 1) // TM_FFN
    tile_end = jnp.cumsum(tiles_e)
    row0 = (tile_end - tiles_e) * TM_FFN
    dest = jnp.sum((csum - 1 + row0[None, :]) * onehot, axis=1)
    n_valid = tile_end[-1]
    tile_id = jnp.minimum(jnp.arange(n_tiles, dtype=jnp.int32), n_valid - 1)
    tile_expert = jnp.sum((tile_id[:, None] >= tile_end[None, :]).astype(jnp.int32), axis=1)
    last_tile = jnp.concatenate([jnp.where(tiles_e > 0, tile_end - 1, -1), n_valid.reshape(1)])
    return (dest.astype(jnp.int32), tile_expert.astype(jnp.int32), n_valid.reshape(1).astype(jnp.int32),
            last_tile.astype(jnp.int32), n_tiles * TM_FFN)


def _moe_ffn(h2, x1, idx, p, mod, ln_g, ln_b, l, w_gate, w_up, w_down, expert0):
    dest, tile_expert, n_valid, last_tile, n_rows = _moe_plan(idx)
    x_rows = _dispatch(h2, dest, last_tile, n_rows)
    y_rows = _grouped_ffn(x_rows, tile_expert + expert0, n_valid, w_gate, w_up, w_down, 4)
    return _combine_norm(x1, p, y_rows, dest, mod, ln_g, ln_b, l)


def _dense_ffn(h2, x1, mod, ln_g, ln_b, l, w_gate, w_up, w_down, index):
    nt = T // TM_FFN
    return _grouped_ffn(h2, jnp.full((nt,), index, jnp.int32), jnp.full((1,), nt, jnp.int32),
                        w_gate, w_up, w_down, 2, norm_args=(x1, mod, ln_g, ln_b, l))


def kernel(x_prompt, x_sample, c, cache_k, cache_v, state_hgrn, c_ctx, w_mod, b_mod, w_in, w_fourier, lb_logits, hgrn_norm, attn_sink, w_out, ln1_g, ln1_b, ln2_g, ln2_b, ffn_w_gate, ffn_w_up, ffn_w_down, router_w, router_b, moe_w_gate, moe_w_up, moe_w_down):
    lb_sm = jax.nn.softmax(lb_logits.astype(F32), axis=0)
    lower_bounds = jnp.clip(jnp.cumsum(lb_sm, axis=0) - lb_sm[0], 0.0, 1.0).reshape(DEPTH, 2, D_HGRN)
    gw = jnp.tile(hgrn_norm, (1, N_HGRN_HEADS)).reshape(DEPTH, 1, D_HGRN)
    per_layer = lambda v: v.reshape(DEPTH, 1, D_MODEL)
    ln1_g, ln1_b, ln2_g, ln2_b = per_layer(ln1_g), per_layer(ln1_b), per_layer(ln2_g), per_layer(ln2_b)
    n_moe = moe_w_gate.shape[0]
    router_wp = jnp.zeros((n_moe, D_MODEL, LANES), F32).at[:, :, :N_EXPERTS].set(router_w)
    router_bp = jnp.full((n_moe, 1, LANES), NEG_BIG, F32).at[:, 0, :N_EXPERTS].set(router_b)
    s0_lat = _states_to_kernel_layout(state_hgrn)
    s0_ctx = jnp.zeros((BATCH, 1, 2, D_HGRN, D_HGRN), F32)

    cond = jnp.zeros((COND_PAD, D_MODEL), F32).at[0].set(c_ctx).at[1:N_COND].set(c)
    mod = _modulation(cond, w_mod, b_mod).reshape(DEPTH, COND_PAD, N_MOD, D_MODEL)

    ch_tabs = _channel_tables()
    pos_tabs_ctx = _dft_tables(SEQ)
    pos_tabs_lat = _dft_tables(DEC_SEQ)
    cos_t, sin_t = _rope_tables()
    kc = cache_k.reshape(DEC_BATCH, DEPTH, PAST_LEN, D_KV)
    vc = cache_v.reshape(DEC_BATCH, DEPTH, PAST_LEN, D_KV)
    moe_wg = moe_w_gate.reshape(n_moe * N_EXPERTS, D_MODEL, D_FF_EXPERT)
    moe_wu = moe_w_up.reshape(n_moe * N_EXPERTS, D_MODEL, D_FF_EXPERT)
    moe_wd = moe_w_down.reshape(n_moe * N_EXPERTS, D_FF_EXPERT, D_MODEL)

    x = jnp.concatenate([x_prompt.reshape(TP, D_MODEL), x_sample.reshape(TS, D_MODEL)], axis=0)
    new_kv, new_s = [], []
    for l in range(DEPTH):
        y_in = _in_projection(x, mod, w_in, l)

        a_ctx = _fourier_mix(y_in, 0, BATCH, SEQ, ch_tabs, pos_tabs_ctx, w_fourier, l)
        a_lat = _fourier_mix(y_in, TP, DEC_BATCH, DEC_SEQ, ch_tabs, pos_tabs_lat, w_fourier, l)
        b_ctx, s_ctx = _hgrn_mix(y_in, 0, BATCH, SEQ, lower_bounds, gw, s0_ctx, l, 0)
        b_lat, _ = _hgrn_mix(y_in, TP, DEC_BATCH, DEC_SEQ, lower_bounds, gw, s0_lat, l, l)
        c_ctx_out = _context_attention(y_in, attn_sink, l)
        c_lat = _latent_attention(y_in, attn_sink, kc, vc, l, cos_t, sin_t)

        k0 = 6 * 256 + D_ATTN
        new_kv.append(y_in[:TP, k0:k0 + 2 * D_KV])
        new_s.append(s_ctx)

        mixed = ((a_ctx, a_lat), (b_ctx, b_lat), (c_ctx_out, c_lat))
        i = l // 2
        if l % 2 == 0:
            x1, h2 = _out_projection(mixed, x, mod, ln1_g, ln1_b, w_out, l)
            x = _dense_ffn(h2, x1, mod, ln2_g, ln2_b, l, ffn_w_gate, ffn_w_up, ffn_w_down, i)
        else:
            x1, h2, idx, p = _out_projection(mixed, x, mod, ln1_g, ln1_b, w_out, l,
                                             router=(router_wp, router_bp, i))
            x = _moe_ffn(h2, x1, idx, p, mod, ln2_g, ln2_b, l, moe_wg, moe_wu, moe_wd, i * N_EXPERTS)

    xp = x[:TP].reshape(BATCH, SEQ, D_MODEL)
    xs = x[TP:].reshape(DEC_BATCH, DEC_SEQ, D_MODEL)
    kv = jnp.stack(new_kv, axis=0).reshape(DEPTH, BATCH, SEQ, 2, N_KV_HEADS, HEAD_DIM)
    kv = jnp.transpose(kv, (3, 1, 0, 2, 4, 5))
    states = _states_from_kernel_layout(jnp.stack(new_s, axis=1))
    return (xp, xs, kv[0], kv[1], states)
```

```python
import functools
import math

import jax
import jax.numpy as jnp
import numpy as np
from jax import lax
from jax.experimental import pallas as pl
from jax.experimental.pallas import tpu as pltpu

D_MODEL = 1024
BATCH = 16
SEQ = 256
DEPTH = 4
DEC_BATCH = 2
DEC_SEQ = 2048
PAST_LEN = 512
GRID_W = 64
D_FOURIER = 256
N_FOURIER_GROUPS = 4
FOURIER_GROUP = D_FOURIER // N_FOURIER_GROUPS
D_HGRN = 256
N_HGRN_HEADS = 4
HGRN_HEAD = D_HGRN // N_HGRN_HEADS
HGRN_CHUNK = 32
HGRN_GROUP = 128
HGRN_STATE_BLOCK = 8
HGRN_SAFE_DECAY = 120.0
N_Q_HEADS = 8
N_KV_HEADS = 2
GQA = N_Q_HEADS // N_KV_HEADS
HEAD_DIM = 64
D_ATTN = N_Q_HEADS * HEAD_DIM
D_KV = N_KV_HEADS * HEAD_DIM
D_MIX = D_FOURIER + D_HGRN + D_ATTN
WINDOW = 128
ATTN_BLOCK = 128
ATTN_SCALE = HEAD_DIM ** -0.5
ROPE_BASE = 10000.0
NEG_BIG = -1e30
D_FF = 2816
N_EXPERTS = 8
TOP_K = 2
D_FF_EXPERT = 3584
DEEPNORM_ALPHA = (2 * DEPTH) ** 0.25
LN_EPS = 1e-5
RMS_EPS = 1e-6
N_MOD = 6
D_IN = 6 * 256 + D_ATTN + 2 * D_KV

TP = BATCH * SEQ
TS = DEC_BATCH * DEC_SEQ
T = TP + TS
N_COND = 1 + DEC_BATCH
COND_PAD = 8

TM = 512
TM_FFN = 512
TM_LN = 256
LANES = 128
VMEM_LIMIT = 56 * 1024 * 1024

F32 = jnp.float32
BF16 = jnp.bfloat16


def _cparams(sem, vmem=VMEM_LIMIT):
    return pltpu.CompilerParams(dimension_semantics=sem, vmem_limit_bytes=vmem)


def _ln(x):
    mu = jnp.mean(x, axis=-1, keepdims=True)
    xc = x - mu
    var = jnp.mean(xc * xc, axis=-1, keepdims=True)
    return xc * lax.rsqrt(var + LN_EPS)


def _silu(x):
    return x * jax.nn.sigmoid(x)


def _split3(a):
    p0 = a.astype(BF16)
    r1 = a - p0.astype(F32)
    p1 = r1.astype(BF16)
    r2 = r1 - p1.astype(F32)
    return p0, p1, r2.astype(BF16)


def _split2(a):
    hi = a.astype(BF16)
    return hi, (a - hi.astype(F32)).astype(BF16)


def _dot(a, b):
    return jnp.dot(a, b, preferred_element_type=F32)


def _dot_nt(a, b):
    return lax.dot_general(a, b, (((1,), (1,)), ((), ())), preferred_element_type=F32)


def _dot_tn(a, b):
    return lax.dot_general(a, b, (((0,), (0,)), ((), ())), preferred_element_type=F32)


def _cond_of_tile(i, tm):
    n_ctx = TP // tm
    return jnp.where(i < n_ctx, 0, 1 + (i - n_ctx) // (DEC_SEQ // tm))


def _mod_spec(l, tm, tile_of=lambda i, *_: i):
    return pl.BlockSpec((None, None, N_MOD, D_MODEL), lambda *a: (l, _cond_of_tile(tile_of(*a), tm), 0, 0))


def _layer_row_spec(l, width):
    return pl.BlockSpec((None, 1, width), lambda *_: (l, 0, 0))


def _mod_kernel(c_ref, w_ref, b_ref, o_ref):
    a = _silu(c_ref[...]).astype(BF16)
    o_ref[...] = _dot(a, w_ref[...].astype(BF16)) + b_ref[...]


def _modulation(cond, w_mod, b_mod):
    tn = 2048
    n_out = N_MOD * D_MODEL
    return pl.pallas_call(
        _mod_kernel,
        grid=(DEPTH, n_out // tn),
        in_specs=[
            pl.BlockSpec((COND_PAD, D_MODEL), lambda l, j: (0, 0)),
            pl.BlockSpec((None, D_MODEL, tn), lambda l, j: (l, 0, j)),
            pl.BlockSpec((None, 1, tn), lambda l, j: (l, 0, j)),
        ],
        out_specs=pl.BlockSpec((None, COND_PAD, tn), lambda l, j: (l, 0, j)),
        out_shape=jax.ShapeDtypeStruct((DEPTH, COND_PAD, n_out), F32),
        compiler_params=_cparams(("parallel", "parallel")),
        name="modulation",
    )(cond, w_mod, b_mod.reshape(DEPTH, 1, n_out))


def _inproj_kernel(x_ref, mod_ref, w_ref, o_ref, wb_ref):
    @pl.when(pl.program_id(0) == 0)
    def _():
        wb_ref[...] = w_ref[...].astype(BF16)

    h = _ln(x_ref[...]) * (1.0 + mod_ref[1:2, :]) + mod_ref[0:1, :]
    o_ref[...] = _dot(h.astype(BF16), wb_ref[...])


def _in_projection(x, mod, w_in, l):
    return pl.pallas_call(
        _inproj_kernel,
        grid=(T // TM,),
        in_specs=[
            pl.BlockSpec((TM, D_MODEL), lambda i: (i, 0)),
            _mod_spec(l, TM),
            pl.BlockSpec((None, D_MODEL, D_IN), lambda i: (l, 0, 0), pipeline_mode=pl.Buffered(1)),
        ],
        out_specs=pl.BlockSpec((TM, D_IN), lambda i: (i, 0)),
        out_shape=jax.ShapeDtypeStruct((T, D_IN), F32),
        scratch_shapes=[pltpu.VMEM((D_MODEL, D_IN), BF16)],
        compiler_params=_cparams(("arbitrary",)),
        name="in_projection",
    )(x, mod, w_in)


def _fourier_kernel(u_ref, ch_hi_ref, ch_lo_ref, tab_hi_ref, tab_lo_ref, wf_ref, o_ref,
                    ab_hi_ref, ab_lo_ref, *, scale):
    @pl.when(pl.program_id(1) == 0)
    def _():
        uh, ul = _split2(u_ref[...])
        ch_hi = ch_hi_ref[...]
        ab = _dot(uh, ch_hi) + _dot(uh, ch_lo_ref[...]) + _dot(ul, ch_hi)
        stacked = jnp.concatenate([ab[:, :D_FOURIER], ab[:, D_FOURIER:]], axis=0)
        hi, lo = _split2(stacked)
        ab_hi_ref[...] = hi
        ab_lo_ref[...] = lo

    th = tab_hi_ref[...]
    ab_hi = ab_hi_ref[...]
    z = _dot(th, ab_hi) + _dot(th, ab_lo_ref[...]) + _dot(tab_lo_ref[...], ab_hi)
    z = (z * scale).astype(BF16)
    o_ref[...] = _dot(z, wf_ref[...].astype(BF16)).astype(BF16)


def _dft_tables(n):
    blk = 32
    t = jnp.arange(n, dtype=jnp.int32)[None, :]
    ang1 = (((jnp.arange(n // blk, dtype=jnp.int32) * blk)[:, None] * t) % n).astype(F32) * (2.0 * math.pi / n)
    ang0 = ((jnp.arange(blk, dtype=jnp.int32)[:, None] * t) % n).astype(F32) * (2.0 * math.pi / n)
    c1, s1 = jnp.cos(ang1)[:, None, :], jnp.sin(ang1)[:, None, :]
    c0, s0 = jnp.cos(ang0)[None, :, :], jnp.sin(ang0)[None, :, :]
    cos = (c1 * c0 - s1 * s0).reshape(n, n)
    sin = (s1 * c0 + c1 * s0).reshape(n, n)
    return _split2(jnp.concatenate([cos, -sin], axis=1))


def _channel_tables():
    a = jnp.arange(D_FOURIER, dtype=jnp.int32)
    same = (a[:, None] // FOURIER_GROUP) == (a[None, :] // FOURIER_GROUP)
    prod = ((a[:, None] % FOURIER_GROUP) * (a[None, :] % FOURIER_GROUP)) % FOURIER_GROUP
    ang = prod.astype(F32) * (2.0 * math.pi / FOURIER_GROUP)
    c = jnp.where(same, jnp.cos(ang), 0.0)
    s = jnp.where(same, jnp.sin(ang), 0.0)
    return _split2(jnp.concatenate([c, s], axis=1))


def _fourier_mix(y_in, row0, nbatch, length, ch_tabs, pos_tabs, w_fourier, l):
    tr = min(length, 256)
    blk0 = row0 // length
    kern = functools.partial(_fourier_kernel, scale=1.0 / math.sqrt(length * FOURIER_GROUP))
    return pl.pallas_call(
        kern,
        grid=(nbatch, length // tr),
        in_specs=[
            pl.BlockSpec((length, D_FOURIER), lambda b, r: (blk0 + b, 0)),
            pl.BlockSpec((D_FOURIER, 2 * D_FOURIER), lambda b, r: (0, 0)),
            pl.BlockSpec((D_FOURIER, 2 * D_FOURIER), lambda b, r: (0, 0)),
            pl.BlockSpec((tr, 2 * length), lambda b, r: (r, 0)),
            pl.BlockSpec((tr, 2 * length), lambda b, r: (r, 0)),
            pl.BlockSpec((None, D_FOURIER, D_FOURIER), lambda b, r: (l, 0, 0)),
        ],
        out_specs=pl.BlockSpec((tr, D_FOURIER), lambda b, r: (b * (length // tr) + r, 0)),
        out_shape=jax.ShapeDtypeStruct((nbatch * length, D_FOURIER), BF16),
        scratch_shapes=[pltpu.VMEM((2 * length, D_FOURIER), BF16),
                        pltpu.VMEM((2 * length, D_FOURIER), BF16)],
        compiler_params=_cparams(("parallel", "arbitrary")),
        name=f"fourier_mix_{length}",
    )(y_in, ch_tabs[0], ch_tabs[1], pos_tabs[0], pos_tabs[1], w_fourier)


def _expand_state(cst):
    full = jnp.concatenate([cst] * N_HGRN_HEADS, axis=0)
    r = lax.broadcasted_iota(jnp.int32, full.shape, 0) // HGRN_HEAD
    c = lax.broadcasted_iota(jnp.int32, full.shape, 1) // HGRN_HEAD
    return jnp.where(r == c, full, jnp.zeros_like(full))


def _compress_state(full):
    lane_head = lax.broadcasted_iota(jnp.int32, (HGRN_HEAD, D_HGRN), 1) // HGRN_HEAD
    out = jnp.zeros((HGRN_HEAD, D_HGRN), full.dtype)
    for h in range(N_HGRN_HEADS):
        out = out + jnp.where(lane_head == h, full[h * HGRN_HEAD:(h + 1) * HGRN_HEAD, :], 0.0)
    return out


def _hgrn_pairwise_scan(d, nc, g_scr, k_scr, q_scr, v_ref, o_scr, st_scr, ones_bd):
    c = HGRN_CHUNK
    last = c - 1 if d == 0 else 0

    def chunk(ci, carry):
        row = lax.broadcasted_iota(jnp.int32, (c, c), 0)
        col = lax.broadcasted_iota(jnp.int32, (c, c), 1)
        tri = jnp.where((col <= row) if d == 0 else (col >= row), 1.0, 0.0).astype(BF16)
        cc = ci if d == 0 else nc - 1 - ci
        rows = pl.ds(pl.multiple_of(cc * c, c), c)
        g = g_scr[rows, :]
        kk = k_scr[rows, :]
        q = q_scr[rows, :]
        v = v_ref[rows, :]
        g0, g1, g2 = _split3(g)
        b = _dot(tri, g0) + _dot(tri, g1) + _dot(tri, g2)
        btot = b[last:last + 1, :]
        st = st_scr[...]
        o_inter = _dot_nt((q * jnp.exp(b)).astype(BF16), _expand_state(st.astype(BF16)))
        s_idx = lax.broadcasted_iota(jnp.int32, (c, c, D_HGRN), 0)
        t_idx = lax.broadcasted_iota(jnp.int32, (c, c, D_HGRN), 1)
        causal = (s_idx <= t_idx) if d == 0 else (s_idx >= t_idx)
        diff = b[None, :, :] - b[:, None, :]
        pair = q[None, :, :] * jnp.exp(jnp.minimum(diff, 0.0)) * kk[:, None, :]
        pair = jnp.where(causal, pair, 0.0).astype(BF16)
        attn = _dot(pair.reshape(c * c, D_HGRN), ones_bd).reshape(c, c, D_HGRN)
        o = o_inter + jnp.sum(attn * v[:, None, :], axis=0)
        if d == 0:
            o_scr[rows, :] = o
        else:
            o_scr[rows, :] = o_scr[rows, :] + o
        kd = kk * jnp.exp(btot - b)
        upd = _dot_tn(v.astype(BF16), kd.astype(BF16))
        st_scr[...] = jnp.exp(btot) * st + _compress_state(upd)
        return carry

    lax.fori_loop(0, nc, chunk, 0)


def _hgrn_factored_scan(d, nc, g_scr, k_scr, q_scr, v_ref, o_scr, st_scr, qd_scr, kd_scr, dec_scr, stb_scr):
    c = HGRN_CHUNK
    grp = HGRN_GROUP
    length = nc * c

    def group(gi, carry):
        rows = pl.ds(pl.multiple_of(gi * grp, grp), grp)
        r = lax.broadcasted_iota(jnp.int32, (grp, grp), 0)
        s = lax.broadcasted_iota(jnp.int32, (grp, grp), 1)
        same_chunk = (r // c) == (s // c)
        ordered = (s <= r) if d == 0 else (s >= r)
        tri = jnp.where(same_chunk & ordered, 1.0, 0.0).astype(BF16)
        ones_chunk = jnp.where(same_chunk, 1.0, 0.0).astype(BF16)
        g0, g1, g2 = _split3(g_scr[rows, :])
        b = _dot(tri, g0) + _dot(tri, g1) + _dot(tri, g2)
        tot = _dot(ones_chunk, g0) + _dot(ones_chunk, g1) + _dot(ones_chunk, g2)
        kk = k_scr[rows, :]
        q = q_scr[rows, :]
        half = 0.5 * tot
        qc = (q * jnp.exp(b - half)).astype(BF16)
        kh = (kk * jnp.exp(half - b)).astype(BF16)
        qd_scr[rows, :] = (q * jnp.exp(b)).astype(BF16)
        kd_scr[rows, :] = (kk * jnp.exp(tot - b)).astype(BF16)
        dec_scr[rows, :] = jnp.exp(tot)
        hs_row = lax.broadcasted_iota(jnp.int32, (N_HGRN_HEADS * grp, D_HGRN), 0) // grp
        hs_col = lax.broadcasted_iota(jnp.int32, (N_HGRN_HEADS * grp, D_HGRN), 1) // HGRN_HEAD
        same_head = hs_row == hs_col
        k_bd = jnp.where(same_head, jnp.concatenate([kh] * N_HGRN_HEADS, axis=0), 0.0)
        v_bd = jnp.where(same_head, jnp.concatenate([v_ref[rows, :].astype(BF16)] * N_HGRN_HEADS, axis=0), 0.0)
        t_idx = lax.broadcasted_iota(jnp.int32, (grp, N_HGRN_HEADS * grp), 0)
        s_idx = lax.broadcasted_iota(jnp.int32, (grp, N_HGRN_HEADS * grp), 1) % grp
        keep = ((t_idx // c) == (s_idx // c)) & ((s_idx <= t_idx) if d == 0 else (s_idx >= t_idx))
        attn = jnp.where(keep, _dot_nt(qc, k_bd), 0.0)
        o_intra = _dot(attn.astype(BF16), v_bd)
        if d == 0:
            o_scr[rows, :] = o_intra
        else:
            o_scr[rows, :] = o_scr[rows, :] + o_intra
        return carry

    lax.fori_loop(0, length // grp, group, 0, unroll=2)

    def state_step(ci, carry):
        cc = ci if d == 0 else nc - 1 - ci
        r0 = pl.multiple_of(cc * c, c)
        rows = pl.ds(r0, c)
        st = st_scr[...]
        stb_scr[cc] = st.astype(BF16)
        upd = _dot_tn(v_ref[rows, :].astype(BF16), kd_scr[rows, :])
        st_scr[...] = dec_scr[pl.ds(r0, 1), :] * st + _compress_state(upd)
        return carry

    lax.fori_loop(0, nc, state_step, 0, unroll=4)

    nb = HGRN_STATE_BLOCK
    blk_rows = nb * c

    def inter(bi, carry):
        rows = pl.ds(pl.multiple_of(bi * blk_rows, blk_rows), blk_rows)
        row_chunk = lax.broadcasted_iota(jnp.int32, (blk_rows, nb * D_HGRN), 0) // c
        col_chunk = lax.broadcasted_iota(jnp.int32, (blk_rows, nb * D_HGRN), 1) // D_HGRN
        q_bd = jnp.where(row_chunk == col_chunk, jnp.concatenate([qd_scr[rows, :]] * nb, axis=1), 0.0)
        states = jnp.concatenate([_expand_state(stb_scr[bi * nb + j]) for j in range(nb)], axis=1)
        o_scr[rows, :] = o_scr[rows, :] + _dot_nt(q_bd, states)
        return carry

    lax.fori_loop(0, nc // nb, inter, 0)


def _hgrn_kernel(hq_ref, hff_ref, hfb_ref, hi_ref, hg_ref, lb_ref, gw_ref, s0_ref,
                 o_ref, sfin_ref, q_scr, g_scr, k_scr, o_scr, dec_scr, qd_scr, kd_scr, st_scr, stb_scr,
                 *, length):
    c = HGRN_CHUNK
    nc = length // c
    q_scr[...] = _silu(hq_ref[...])
    r256 = lax.broadcasted_iota(jnp.int32, (D_HGRN, D_HGRN), 0) // HGRN_HEAD
    c256 = lax.broadcasted_iota(jnp.int32, (D_HGRN, D_HGRN), 1) // HGRN_HEAD
    ones_bd = jnp.where(r256 == c256, 1.0, 0.0).astype(BF16)

    for d in range(2):
        z_ref = hff_ref if d == 0 else hfb_ref
        one_minus_f = (1.0 - lb_ref[d:d + 1, :]) * jax.nn.sigmoid(-z_ref[...])
        g_scr[...] = jnp.log1p(-one_minus_f)
        k_scr[...] = one_minus_f
        st_scr[...] = s0_ref[d]
        chunk_decay = jnp.sum(g_scr[...].reshape(nc, c, D_HGRN), axis=1)
        safe = jnp.min(chunk_decay) >= -HGRN_SAFE_DECAY
        lax.cond(
            safe,
            functools.partial(_hgrn_factored_scan, d, nc, g_scr, k_scr, q_scr, hi_ref, o_scr, st_scr,
                              qd_scr, kd_scr, dec_scr, stb_scr),
            functools.partial(_hgrn_pairwise_scan, d, nc, g_scr, k_scr, q_scr, hi_ref, o_scr, st_scr, ones_bd))
        sfin_ref[d] = st_scr[...]

    o = o_scr[...]
    s0p, s1p, s2p = _split3(o * o)
    ms = (_dot(s0p, ones_bd) + _dot(s1p, ones_bd) + _dot(s2p, ones_bd)) * (1.0 / HGRN_HEAD)
    y = o * lax.rsqrt(ms + RMS_EPS) * gw_ref[...] * _silu(hg_ref[...])
    o_ref[...] = y.astype(BF16)


def _hgrn_mix(y_in, row0, nbatch, length, lb, gw, s0, l, s0_layer):
    blk0 = row0 // length
    col = lambda j: (lambda b: (blk0 + b, j))
    kern = functools.partial(_hgrn_kernel, length=length)
    return pl.pallas_call(
        kern,
        grid=(nbatch,),
        in_specs=[
            pl.BlockSpec((length, D_HGRN), col(1)),
            pl.BlockSpec((length, D_HGRN), col(2)),
            pl.BlockSpec((length, D_HGRN), col(3)),
            pl.BlockSpec((length, D_HGRN), col(4)),
            pl.BlockSpec((length, D_HGRN), col(5)),
            pl.BlockSpec((None, 2, D_HGRN), lambda b: (l, 0, 0)),
            _layer_row_spec(l, D_HGRN),
            pl.BlockSpec((None, None, 2, HGRN_HEAD, D_HGRN), lambda b: (b, s0_layer, 0, 0, 0)),
        ],
        out_specs=[
            pl.BlockSpec((length, D_HGRN), lambda b: (b, 0)),
            pl.BlockSpec((None, 2, HGRN_HEAD, D_HGRN), lambda b: (b, 0, 0, 0)),
        ],
        out_shape=[
            jax.ShapeDtypeStruct((nbatch * length, D_HGRN), BF16),
            jax.ShapeDtypeStruct((nbatch, 2, HGRN_HEAD, D_HGRN), F32),
        ],
        scratch_shapes=[pltpu.VMEM((length, D_HGRN), F32) for _ in range(5)]
        + [pltpu.VMEM((length, D_HGRN), BF16) for _ in range(2)]
        + [pltpu.VMEM((HGRN_HEAD, D_HGRN), F32), pltpu.VMEM((length // HGRN_CHUNK, HGRN_HEAD, D_HGRN), BF16)],
        compiler_params=_cparams(("parallel",)),
        name=f"hgrn_mix_{length}",
    )(y_in, y_in, y_in, y_in, y_in, lb, gw, s0)


def _states_to_kernel_layout(s):
    return jnp.moveaxis(s, -1, -3).reshape(s.shape[:-3] + (HGRN_HEAD, D_HGRN))


def _states_from_kernel_layout(st):
    return jnp.moveaxis(st.reshape(st.shape[:-1] + (N_HGRN_HEADS, HGRN_HEAD)), -3, -1)


def _group_attention(q_heads, sinks, kv_parts):
    rows = q_heads[0].shape[0]
    qg = jnp.concatenate([(q * ATTN_SCALE).astype(BF16) for q in q_heads], axis=0)
    head = lax.broadcasted_iota(jnp.int32, (len(q_heads) * rows, 1), 0) // rows
    sink = jnp.full(head.shape, sinks[0], F32)
    for i in range(1, len(q_heads)):
        sink = jnp.where(head == i, sinks[i], sink)
    m = sink
    scores = []
    for k, _, mask in kv_parts:
        s = _dot_nt(qg, k)
        if mask is not None:
            s = jnp.where(mask, s, NEG_BIG)
        m = jnp.maximum(m, jnp.max(s, axis=1, keepdims=True))
        scores.append(s)
    den = jnp.exp(sink - m)
    out = None
    for s, (_, v, _) in zip(scores, kv_parts):
        e = jnp.exp(s - m)
        den = den + jnp.sum(e, axis=1, keepdims=True)
        o = _dot(e.astype(BF16), v)
        out = o if out is None else out + o
    out = out * (1.0 / den)
    return [out[i * rows:(i + 1) * rows, :] for i in range(len(q_heads))]


def _ctx_attn_kernel(sink_ref, q_ref, k_ref, v_ref, o_ref, *, layer):
    outs = []
    for g in range(N_KV_HEADS):
        sl = slice(g * HEAD_DIM, (g + 1) * HEAD_DIM)
        heads = range(g * GQA, (g + 1) * GQA)
        outs += _group_attention(
            [q_ref[:, h * HEAD_DIM:(h + 1) * HEAD_DIM] for h in heads],
            [sink_ref[layer, h] for h in heads],
            [(k_ref[:, sl].astype(BF16), v_ref[:, sl].astype(BF16), None)])
    o_ref[...] = jnp.concatenate(outs, axis=1).astype(BF16)


def _context_attention(y_in, sink, l):
    qcol = (6 * 256) // D_ATTN
    kcol = (6 * 256 + D_ATTN) // D_KV
    return pl.pallas_call(
        functools.partial(_ctx_attn_kernel, layer=l),
        grid=(BATCH,),
        in_specs=[
            pl.BlockSpec(memory_space=pltpu.SMEM),
            pl.BlockSpec((SEQ, D_ATTN), lambda b: (b, qcol)),
            pl.BlockSpec((SEQ, D_KV), lambda b: (b, kcol)),
            pl.BlockSpec((SEQ, D_KV), lambda b: (b, kcol + 1)),
        ],
        out_specs=pl.BlockSpec((SEQ, D_ATTN), lambda b: (b, 0)),
        out_shape=jax.ShapeDtypeStruct((TP, D_ATTN), BF16),
        compiler_params=_cparams(("parallel",)),
        name="context_attention",
    )(sink, y_in, y_in, y_in)


def _rope(x, cos, sin):
    lane = lax.broadcasted_iota(jnp.int32, x.shape, 1)
    n_freq = HEAD_DIM // 4
    first = (lane % (2 * n_freq)) < n_freq
    swapped = jnp.where(first, pltpu.roll(x, LANES - n_freq, axis=1), pltpu.roll(x, n_freq, axis=1))
    return x * cos + swapped * sin


def _lat_attn_kernel(sink_ref, q_ref, k_ref, v_ref, kc_ref, vc_ref, cos_ref, sin_ref, o_ref, *, layer):
    j = pl.program_id(1)
    nb = DEC_SEQ // ATTN_BLOCK
    blk = ATTN_BLOCK
    q0 = pl.multiple_of(j * blk, blk)
    cos_q = cos_ref[pl.ds(q0, blk), :]
    sin_q = sin_ref[pl.ds(q0, blk), :]

    starts = [jnp.maximum(j - 1, 0), j, jnp.minimum(j + 1, nb - 1)]
    k_band, v_band = [], []
    for st in starts:
        r0 = pl.multiple_of(st * blk, blk)
        kb = _rope(k_ref[pl.ds(r0, blk), :], cos_ref[pl.ds(r0, blk), :], sin_ref[pl.ds(r0, blk), :])
        k_band.append(kb)
        v_band.append(v_ref[pl.ds(r0, blk), :])
    k_loc = jnp.concatenate(k_band, axis=0)
    v_loc = jnp.concatenate(v_band, axis=0)
    r = lax.broadcasted_iota(jnp.int32, (GQA * blk, 3 * blk), 0) % blk
    cidx = lax.broadcasted_iota(jnp.int32, (GQA * blk, 3 * blk), 1)
    s_pos = (j - 1) * blk + cidx
    valid = (cidx >= r) & (cidx <= r + 2 * WINDOW) & (s_pos >= 0) & (s_pos < DEC_SEQ)

    heads_per_chunk = LANES // HEAD_DIM
    q_chunks = [_rope(q_ref[:, cg * LANES:(cg + 1) * LANES], cos_q, sin_q)
                for cg in range(N_Q_HEADS // heads_per_chunk)]

    def q_head(h):
        off = (h % heads_per_chunk) * HEAD_DIM
        return q_chunks[h // heads_per_chunk][:, off:off + HEAD_DIM]

    outs = []
    for g in range(N_KV_HEADS):
        sl = slice(g * HEAD_DIM, (g + 1) * HEAD_DIM)
        heads = range(g * GQA, (g + 1) * GQA)
        outs += _group_attention(
            [q_head(h) for h in heads],
            [sink_ref[layer, h] for h in heads],
            [(k_loc[:, sl].astype(BF16), v_loc[:, sl].astype(BF16), valid),
             (kc_ref[:, sl].astype(BF16), vc_ref[:, sl].astype(BF16), None)])
    o_ref[...] = jnp.concatenate(outs, axis=1).astype(BF16)


def _latent_attention(y_in, sink, kc, vc, l, cos_t, sin_t):
    nb = DEC_SEQ // ATTN_BLOCK
    qrow0 = TP // ATTN_BLOCK
    krow0 = TP // DEC_SEQ
    qcol = (6 * 256) // D_ATTN
    kcol = (6 * 256 + D_ATTN) // D_KV
    return pl.pallas_call(
        functools.partial(_lat_attn_kernel, layer=l),
        grid=(DEC_BATCH, nb),
        in_specs=[
            pl.BlockSpec(memory_space=pltpu.SMEM),
            pl.BlockSpec((ATTN_BLOCK, D_ATTN), lambda b, j: (qrow0 + b * nb + j, qcol)),
            pl.BlockSpec((DEC_SEQ, D_KV), lambda b, j: (krow0 + b, kcol)),
            pl.BlockSpec((DEC_SEQ, D_KV), lambda b, j: (krow0 + b, kcol + 1)),
            pl.BlockSpec((None, None, PAST_LEN, D_KV), lambda b, j: (b, l, 0, 0)),
            pl.BlockSpec((None, None, PAST_LEN, D_KV), lambda b, j: (b, l, 0, 0)),
            pl.BlockSpec((DEC_SEQ, LANES), lambda b, j: (0, 0)),
            pl.BlockSpec((DEC_SEQ, LANES), lambda b, j: (0, 0)),
        ],
        out_specs=pl.BlockSpec((ATTN_BLOCK, D_ATTN), lambda b, j: (b * nb + j, 0)),
        out_shape=jax.ShapeDtypeStruct((TS, D_ATTN), BF16),
        compiler_params=_cparams(("parallel", "parallel")),
        name="latent_attention",
    )(sink, y_in, y_in, y_in, kc, vc, cos_t, sin_t)


def _rope_tables():
    t = jnp.arange(DEC_SEQ)
    rows = (t // GRID_W).astype(F32)
    cols = (t % GRID_W).astype(F32)
    n_freq = HEAD_DIM // 4
    inv = ROPE_BASE ** (-jnp.arange(n_freq, dtype=F32) / n_freq)
    ar = rows[:, None] * inv
    ac = cols[:, None] * inv
    cos = jnp.concatenate([jnp.cos(ar), jnp.cos(ar), jnp.cos(ac), jnp.cos(ac)], axis=1)
    sin = jnp.concatenate([-jnp.sin(ar), jnp.sin(ar), -jnp.sin(ac), jnp.sin(ac)], axis=1)
    return jnp.tile(cos, (1, N_KV_HEADS)), jnp.tile(sin, (1, N_KV_HEADS))


def _top2_route(h2, w_ref, b_ref, idx_ref, p_ref):
    logits = _dot(h2.astype(BF16), w_ref[...].astype(BF16)) + b_ref[...]
    lane = lax.broadcasted_iota(jnp.int32, logits.shape, 1)
    m1 = jnp.max(logits, axis=1, keepdims=True)
    i1 = jnp.min(jnp.where(logits == m1, lane, LANES), axis=1, keepdims=True)
    rest = jnp.where(lane == i1, -jnp.inf, logits)
    m2 = jnp.max(rest, axis=1, keepdims=True)
    i2 = jnp.min(jnp.where(rest == m2, lane, LANES), axis=1, keepdims=True)
    e = jnp.exp(m2 - m1)
    inv = 1.0 / (1.0 + e)
    idx_ref[...] = jnp.where(lane == 0, i1, jnp.where(lane == 1, i2, 0))
    p_ref[...] = jnp.where(lane == 0, inv, jnp.where(lane == 1, e * inv, 0.0))


def _outproj_kernel(a0_ref, a1_ref, b0_ref, b1_ref, c0_ref, c1_ref, x_ref, mod_ref, g_ref, beta_ref, w_ref,
                    *rest, route):
    if route:
        wr_ref, br_ref, x1_ref, h2_ref, idx_ref, p_ref, wb_ref = rest
    else:
        x1_ref, h2_ref, wb_ref = rest

    @pl.when(pl.program_id(0) == 0)
    def _():
        wb_ref[...] = w_ref[...].astype(BF16)

    is_ctx = pl.program_id(0) < TP // TM
    pick = lambda r0, r1: jnp.where(is_ctx, r0[...], r1[...])
    y = (_dot(pick(a0_ref, a1_ref), wb_ref[0:D_FOURIER, :])
         + _dot(pick(b0_ref, b1_ref), wb_ref[D_FOURIER:D_FOURIER + D_HGRN, :])
         + _dot(pick(c0_ref, c1_ref), wb_ref[D_FOURIER + D_HGRN:D_MIX, :]))
    x1 = _ln(DEEPNORM_ALPHA * x_ref[...] + mod_ref[2:3, :] * y) * g_ref[...] + beta_ref[...]
    x1_ref[...] = x1
    h2 = _ln(x1) * (1.0 + mod_ref[4:5, :]) + mod_ref[3:4, :]
    h2_ref[...] = h2
    if route:
        _top2_route(h2, wr_ref, br_ref, idx_ref, p_ref)


def _out_projection(mixed, x, mod, ln_g, ln_b, w_out, l, router=None):
    n_ctx = TP // TM
    row = lambda w: pl.BlockSpec((TM, w), lambda i: (i, 0))
    ctx = lambda w: pl.BlockSpec((TM, w), lambda i: (jnp.minimum(i, n_ctx - 1), 0))
    lat = lambda w: pl.BlockSpec((TM, w), lambda i: (jnp.maximum(i - n_ctx, 0), 0))
    vec = _layer_row_spec(l, D_MODEL)
    in_specs = [
        ctx(D_FOURIER), lat(D_FOURIER), ctx(D_HGRN), lat(D_HGRN), ctx(D_ATTN), lat(D_ATTN), row(D_MODEL),
        _mod_spec(l, TM),
        vec, vec,
        pl.BlockSpec((None, D_MIX, D_MODEL), lambda i: (l, 0, 0), pipeline_mode=pl.Buffered(1)),
    ]
    args = [mixed[0][0], mixed[0][1], mixed[1][0], mixed[1][1], mixed[2][0], mixed[2][1], x, mod, ln_g, ln_b,
            w_out]
    out_specs = [row(D_MODEL), row(D_MODEL)]
    out_shape = [jax.ShapeDtypeStruct((T, D_MODEL), F32), jax.ShapeDtypeStruct((T, D_MODEL), F32)]
    if router is not None:
        w, b, ri = router
        in_specs += [pl.BlockSpec((None, D_MODEL, LANES), lambda i: (ri, 0, 0)), _layer_row_spec(ri, LANES)]
        args += [w, b]
        out_specs += [row(LANES), row(LANES)]
        out_shape += [jax.ShapeDtypeStruct((T, LANES), jnp.int32), jax.ShapeDtypeStruct((T, LANES), F32)]
    return pl.pallas_call(
        functools.partial(_outproj_kernel, route=router is not None),
        grid=(T // TM,),
        in_specs=in_specs,
        out_specs=out_specs,
        out_shape=out_shape,
        scratch_shapes=[pltpu.VMEM((D_MIX, D_MODEL), BF16)],
        compiler_params=_cparams(("arbitrary",)),
        name="out_projection_route" if router is not None else "out_projection",
    )(*args)


def _expert_changed(te_ref, i):
    return (i == 0) | (te_ref[i] != te_ref[jnp.maximum(i - 1, 0)])


def _post_ffn(x1, y, g2, ln_g, ln_b):
    return _ln(DEEPNORM_ALPHA * x1 + g2 * y) * ln_g + ln_b


def _for_used_rows(rows, out_ref, compute):
    half = TM_FFN // 2

    @pl.when(rows > half)
    def _():
        compute(0, TM_FFN)

    @pl.when((rows > 0) & (rows <= half))
    def _():
        compute(0, half)
        out_ref[half:, :] = jnp.zeros((TM_FFN - half, out_ref.shape[1]), out_ref.dtype)

    @pl.when(rows == 0)
    def _():
        out_ref[...] = jnp.zeros_like(out_ref)


def _ffn_up_kernel(te_ref, nv_ref, tr_ref, x_ref, wg_ref, wu_ref, h_ref, wb_ref):
    i = pl.program_id(1)
    fh = wg_ref.shape[-1]

    @pl.when(_expert_changed(te_ref, i))
    def _():
        wb_ref[:, :fh] = wg_ref[...].astype(BF16)
        wb_ref[:, fh:] = wu_ref[...].astype(BF16)

    def compute(r0, n):
        ab = _dot(x_ref[r0:r0 + n, :].astype(BF16), wb_ref[...])
        h_ref[r0:r0 + n, :] = (_silu(ab[:, :fh]) * ab[:, fh:]).astype(BF16)

    _for_used_rows(tr_ref[i], h_ref, compute)


def _ffn_down_kernel(te_ref, nv_ref, tr_ref, h_ref, wd_ref, *rest, norm):
    if norm:
        x1_ref, mod_ref, g_ref, beta_ref, y_ref, wdb_ref = rest
    else:
        y_ref, wdb_ref = rest
    i = pl.program_id(1)

    @pl.when(_expert_changed(te_ref, i))
    def _():
        wdb_ref[...] = wd_ref[...].astype(BF16)

    if norm:
        y = _dot(h_ref[...], wdb_ref[...])
        y_ref[...] = _post_ffn(x1_ref[...], y, mod_ref[5:6, :], g_ref[...], beta_ref[...])
    else:
        def compute(r0, n):
            y_ref[r0:r0 + n, :] = _dot(h_ref[r0:r0 + n, :], wdb_ref[...])

        _for_used_rows(tr_ref[i], y_ref, compute)


def _grouped_ffn(x_rows, tile_expert, n_valid, tile_rows, w_gate, w_up, w_down, f_splits, norm_args=None):
    r = x_rows.shape[0]
    nt = r // TM_FFN
    f = w_gate.shape[-1]
    fh = f // f_splits
    assert fh * f_splits == f and fh % LANES == 0
    used = lambda i, nv: jnp.minimum(i, nv[0] - 1)
    h = pl.pallas_call(
        _ffn_up_kernel,
        grid_spec=pltpu.PrefetchScalarGridSpec(
            num_scalar_prefetch=3,
            grid=(f_splits, nt),
            in_specs=[
                pl.BlockSpec((TM_FFN, D_MODEL), lambda j, i, te, nv, tr: (used(i, nv), 0)),
                pl.BlockSpec((None, D_MODEL, fh), lambda j, i, te, nv, tr: (te[i], 0, j)),
                pl.BlockSpec((None, D_MODEL, fh), lambda j, i, te, nv, tr: (te[i], 0, j)),
            ],
            out_specs=pl.BlockSpec((TM_FFN, fh), lambda j, i, te, nv, tr: (i, j)),
            scratch_shapes=[pltpu.VMEM((D_MODEL, 2 * fh), BF16)],
        ),
        out_shape=jax.ShapeDtypeStruct((r, f), BF16),
        compiler_params=_cparams(("arbitrary", "arbitrary")),
        name="ffn_up",
    )(tile_expert, n_valid, tile_rows, x_rows, w_gate, w_up)

    d_splits = 1 if norm_args is not None else 2
    dh = D_MODEL // d_splits
    in_specs = [
        pl.BlockSpec((TM_FFN, f), lambda j, i, te, nv, tr: (used(i, nv), 0)),
        pl.BlockSpec((None, f, dh), lambda j, i, te, nv, tr: (te[i], 0, j)),
    ]
    args = [tile_expert, n_valid, tile_rows, h, w_down]
    if norm_args is not None:
        assert r == T
        x1, mod, ln_g, ln_b, l = norm_args
        vec = _layer_row_spec(l, D_MODEL)
        in_specs += [
            pl.BlockSpec((TM_FFN, D_MODEL), lambda j, i, te, nv, tr: (i, 0)),
            _mod_spec(l, TM_FFN, lambda j, i, *_: i),
            vec, vec]
        args += [x1, mod, ln_g, ln_b]
    return pl.pallas_call(
        functools.partial(_ffn_down_kernel, norm=norm_args is not None),
        grid_spec=pltpu.PrefetchScalarGridSpec(
            num_scalar_prefetch=3,
            grid=(d_splits, nt),
            in_specs=in_specs,
            out_specs=pl.BlockSpec((TM_FFN, dh), lambda j, i, te, nv, tr: (i, j)),
            scratch_shapes=[pltpu.VMEM((f, dh), BF16)],
        ),
        out_shape=jax.ShapeDtypeStruct((r, D_MODEL), F32),
        compiler_params=_cparams(("arbitrary", "arbitrary")),
        name="ffn_down_norm" if norm_args is not None else "ffn_down",
    )(*args)


def _row_copy(src, dst, s, d, sem):
    return pltpu.make_async_copy(src.at[pl.ds(s, 1), :], dst.at[pl.ds(d, 1), :], sem)


def _dispatch_kernel(dest_ref, last_ref, x_ref, o_hbm, zero_scr, sem, zsem):
    i = pl.program_id(0)

    @pl.when(i == 0)
    def _():
        zero_scr[...] = jnp.zeros_like(zero_scr)

        def fill(tile):
            r0 = pl.multiple_of(tile * TM_FFN, TM_FFN)
            return pltpu.make_async_copy(zero_scr, o_hbm.at[pl.ds(r0, TM_FFN), :], zsem)

        n_tiles = o_hbm.shape[0] // TM_FFN
        min_tiles = (T * TOP_K) // TM_FFN
        jobs = [(last_ref[e] >= 0, last_ref[e]) for e in range(N_EXPERTS)]
        jobs += [(t >= last_ref[N_EXPERTS], t) for t in range(min_tiles, n_tiles)]
        for go, tile in jobs:
            @pl.when(go)
            def _(tile=tile):
                fill(tile).start()
        for go, tile in jobs:
            @pl.when(go)
            def _(tile=tile):
                fill(tile).wait()

    base = i * TM_LN

    def issue(r, carry):
        for k in range(TOP_K):
            _row_copy(x_ref, o_hbm, r, dest_ref[(base + r) * TOP_K + k], sem).start(priority=k % 2)
        return carry

    lax.fori_loop(0, TM_LN, issue, 0, unroll=8)
    for k in range(TOP_K):
        pltpu.make_async_copy(x_ref, o_hbm.at[pl.ds(0, TM_LN), :], sem).wait()


def _dispatch(h2, dest, last_tile, n_rows):
    return pl.pallas_call(
        _dispatch_kernel,
        grid_spec=pltpu.PrefetchScalarGridSpec(
            num_scalar_prefetch=2,
            grid=(T // TM_LN,),
            in_specs=[pl.BlockSpec((TM_LN, D_MODEL), lambda i, d, l: (i, 0))],
            out_specs=pl.BlockSpec(memory_space=pl.ANY),
            scratch_shapes=[pltpu.VMEM((TM_FFN, D_MODEL), F32), pltpu.SemaphoreType.DMA(()),
                            pltpu.SemaphoreType.DMA(())],
        ),
        out_shape=jax.ShapeDtypeStruct((n_rows, D_MODEL), F32),
        compiler_params=_cparams(("arbitrary",)),
        name="dispatch",
    )(dest, last_tile, h2)


def _combine_kernel(pos_ref, x1_ref, p_ref, y_hbm, mod_ref, g_ref, beta_ref, o_ref, buf, sem):
    i = pl.program_id(0)
    n = pl.num_programs(0)

    def fetch(step, slot):
        def issue(r, carry):
            for k in range(TOP_K):
                src = pos_ref[(step * TM_LN + r) * TOP_K + k]
                _row_copy(y_hbm, buf.at[slot, k], src, r, sem.at[slot, k]).start(priority=k % 2)
            return carry

        lax.fori_loop(0, TM_LN, issue, 0, unroll=8)

    @pl.when(i == 0)
    def _():
        fetch(0, 0)

    slot = i % 2

    @pl.when(i + 1 < n)
    def _():
        fetch(i + 1, 1 - slot)

    for k in range(TOP_K):
        pltpu.make_async_copy(y_hbm.at[pl.ds(0, TM_LN), :], buf.at[slot, k], sem.at[slot, k]).wait()
    y = p_ref[:, 0:1] * buf[slot, 0]
    for k in range(1, TOP_K):
        y = y + p_ref[:, k:k + 1] * buf[slot, k]
    o_ref[...] = _post_ffn(x1_ref[...], y, mod_ref[5:6, :], g_ref[...], beta_ref[...])


def _combine_norm(x1, p, y_rows, pos, mod, ln_g, ln_b, l):
    row = pl.BlockSpec((TM_LN, D_MODEL), lambda i, s: (i, 0))
    vec = _layer_row_spec(l, D_MODEL)
    return pl.pallas_call(
        _combine_kernel,
        grid_spec=pltpu.PrefetchScalarGridSpec(
            num_scalar_prefetch=1,
            grid=(T // TM_LN,),
            in_specs=[row, pl.BlockSpec((TM_LN, LANES), lambda i, s: (i, 0)),
                      pl.BlockSpec(memory_space=pl.ANY),
                      _mod_spec(l, TM_LN),
                      vec, vec],
            out_specs=row,
            scratch_shapes=[pltpu.VMEM((2, TOP_K, TM_LN, D_MODEL), F32),
                            pltpu.SemaphoreType.DMA((2, TOP_K))],
        ),
        out_shape=jax.ShapeDtypeStruct((T, D_MODEL), F32),
        compiler_params=_cparams(("arbitrary",)),
        name="combine_norm",
    )(pos, x1, p, y_rows, mod, ln_g, ln_b)


def _moe_plan(idx):
    n_assign = T * TOP_K
    n_tiles = n_assign // TM_FFN + N_EXPERTS
    e = idx[:, :TOP_K].reshape(n_assign)
    onehot = (e[:, None] == jnp.arange(N_EXPERTS, dtype=jnp.int32)[None, :]).astype(jnp.int32)
    csum = jnp.cumsum(onehot, axis=0)
    counts = csum[-1]
    tiles_e = (counts + TM_FFN - 1) // TM_FFN
    tile_end = jnp.cumsum(tiles_e)
    row0 = (tile_end - tiles_e) * TM_FFN
    dest = jnp.sum((csum - 1 + row0[None, :]) * onehot, axis=1)
    n_valid = tile_end[-1]
    tiles = jnp.arange(n_tiles, dtype=jnp.int32)
    tile_id = jnp.minimum(tiles, n_valid - 1)
    tile_expert = jnp.sum((tile_id[:, None] >= tile_end[None, :]).astype(jnp.int32), axis=1)
    tile_start = tile_end - tiles_e
    in_group = (tiles[:, None] >= tile_start[None, :]) & (tiles[:, None] < tile_end[None, :])
    left = counts[None, :] - (tiles[:, None] - tile_start[None, :]) * TM_FFN
    tile_rows = jnp.sum(jnp.where(in_group, jnp.clip(left, 0, TM_FFN), 0), axis=1)
    last_tile = jnp.concatenate([jnp.where(tiles_e > 0, tile_end - 1, -1), n_valid.reshape(1)])
    return (dest.astype(jnp.int32), tile_expert.astype(jnp.int32), n_valid.reshape(1).astype(jnp.int32),
            tile_rows.astype(jnp.int32), last_tile.astype(jnp.int32), n_tiles * TM_FFN)


def _moe_ffn(h2, x1, idx, p, mod, ln_g, ln_b, l, w_gate, w_up, w_down, expert0):
    dest, tile_expert, n_valid, tile_rows, last_tile, n_rows = _moe_plan(idx)
    x_rows = _dispatch(h2, dest, last_tile, n_rows)
    y_rows = _grouped_ffn(x_rows, tile_expert + expert0, n_valid, tile_rows, w_gate, w_up, w_down, 4)
    return _combine_norm(x1, p, y_rows, dest, mod, ln_g, ln_b, l)


def _dense_ffn(h2, x1, mod, ln_g, ln_b, l, w_gate, w_up, w_down, index):
    nt = T // TM_FFN
    return _grouped_ffn(h2, jnp.full((nt,), index, jnp.int32), jnp.full((1,), nt, jnp.int32),
                        jnp.full((nt,), TM_FFN, jnp.int32), w_gate, w_up, w_down, 2,
                        norm_args=(x1, mod, ln_g, ln_b, l))


def kernel(x_prompt, x_sample, c, cache_k, cache_v, state_hgrn, c_ctx, w_mod, b_mod, w_in, w_fourier, lb_logits, hgrn_norm, attn_sink, w_out, ln1_g, ln1_b, ln2_g, ln2_b, ffn_w_gate, ffn_w_up, ffn_w_down, router_w, router_b, moe_w_gate, moe_w_up, moe_w_down):
    lb_sm = jax.nn.softmax(lb_logits.astype(F32), axis=0)
    lower_bounds = jnp.clip(jnp.cumsum(lb_sm, axis=0) - lb_sm[0], 0.0, 1.0).reshape(DEPTH, 2, D_HGRN)
    gw = jnp.tile(hgrn_norm, (1, N_HGRN_HEADS)).reshape(DEPTH, 1, D_HGRN)
    per_layer = lambda v: v.reshape(DEPTH, 1, D_MODEL)
    ln1_g, ln1_b, ln2_g, ln2_b = per_layer(ln1_g), per_layer(ln1_b), per_layer(ln2_g), per_layer(ln2_b)
    n_moe = moe_w_gate.shape[0]
    router_wp = jnp.zeros((n_moe, D_MODEL, LANES), F32).at[:, :, :N_EXPERTS].set(router_w)
    router_bp = jnp.full((n_moe, 1, LANES), NEG_BIG, F32).at[:, 0, :N_EXPERTS].set(router_b)
    s0_lat = _states_to_kernel_layout(state_hgrn)
    s0_ctx = jnp.zeros((BATCH, 1, 2, HGRN_HEAD, D_HGRN), F32)

    cond = jnp.zeros((COND_PAD, D_MODEL), F32).at[0].set(c_ctx).at[1:N_COND].set(c)
    mod = _modulation(cond, w_mod, b_mod).reshape(DEPTH, COND_PAD, N_MOD, D_MODEL)

    ch_tabs = _channel_tables()
    pos_tabs_ctx = _dft_tables(SEQ)
    pos_tabs_lat = _dft_tables(DEC_SEQ)
    cos_t, sin_t = _rope_tables()
    kc = cache_k.reshape(DEC_BATCH, DEPTH, PAST_LEN, D_KV)
    vc = cache_v.reshape(DEC_BATCH, DEPTH, PAST_LEN, D_KV)
    moe_wg = moe_w_gate.reshape(n_moe * N_EXPERTS, D_MODEL, D_FF_EXPERT)
    moe_wu = moe_w_up.reshape(n_moe * N_EXPERTS, D_MODEL, D_FF_EXPERT)
    moe_wd = moe_w_down.reshape(n_moe * N_EXPERTS, D_FF_EXPERT, D_MODEL)

    x = jnp.concatenate([x_prompt.reshape(TP, D_MODEL), x_sample.reshape(TS, D_MODEL)], axis=0)
    new_kv, new_s = [], []
    for l in range(DEPTH):
        y_in = _in_projection(x, mod, w_in, l)

        a_ctx = _fourier_mix(y_in, 0, BATCH, SEQ, ch_tabs, pos_tabs_ctx, w_fourier, l)
        a_lat = _fourier_mix(y_in, TP, DEC_BATCH, DEC_SEQ, ch_tabs, pos_tabs_lat, w_fourier, l)
        b_ctx, s_ctx = _hgrn_mix(y_in, 0, BATCH, SEQ, lower_bounds, gw, s0_ctx, l, 0)
        b_lat, _ = _hgrn_mix(y_in, TP, DEC_BATCH, DEC_SEQ, lower_bounds, gw, s0_lat, l, l)
        c_ctx_out = _context_attention(y_in, attn_sink, l)
        c_lat = _latent_attention(y_in, attn_sink, kc, vc, l, cos_t, sin_t)

        k0 = 6 * 256 + D_ATTN
        new_kv.append(y_in[:TP, k0:k0 + 2 * D_KV])
        new_s.append(s_ctx)

        mixed = ((a_ctx, a_lat), (b_ctx, b_lat), (c_ctx_out, c_lat))
        i = l // 2
        if l % 2 == 0:
            x1, h2 = _out_projection(mixed, x, mod, ln1_g, ln1_b, w_out, l)
            x = _dense_ffn(h2, x1, mod, ln2_g, ln2_b, l, ffn_w_gate, ffn_w_up, ffn_w_down, i)
        else:
            x1, h2, idx, p = _out_projection(mixed, x, mod, ln1_g, ln1_b, w_out, l,
                                             router=(router_wp, router_bp, i))
            x = _moe_ffn(h2, x1, idx, p, mod, ln2_g, ln2_b, l, moe_wg, moe_wu, moe_wd, i * N_EXPERTS)

    xp = x[:TP].reshape(BATCH, SEQ, D_MODEL)
    xs = x[TP:].reshape(DEC_BATCH, DEC_SEQ, D_MODEL)
    kv = jnp.stack(new_kv, axis=0).reshape(DEPTH, BATCH, SEQ, 2, N_KV_HEADS, HEAD_DIM)
    kv = jnp.transpose(kv, (3, 1, 0, 2, 4, 5))
    states = _states_from_kernel_layout(jnp.stack(new_s, axis=1))
    return (xp, xs, kv[0], kv[1], states)
```

```python
import functools
import math

import jax
import jax.numpy as jnp
import numpy as np
from jax import lax
from jax.experimental import pallas as pl
from jax.experimental.pallas import tpu as pltpu

D_MODEL = 1024
BATCH = 16
SEQ = 256
DEPTH = 4
DEC_BATCH = 2
DEC_SEQ = 2048
PAST_LEN = 512
GRID_W = 64
D_FOURIER = 256
N_FOURIER_GROUPS = 4
FOURIER_GROUP = D_FOURIER // N_FOURIER_GROUPS
D_HGRN = 256
N_HGRN_HEADS = 4
HGRN_HEAD = D_HGRN // N_HGRN_HEADS
HGRN_CHUNK = 32
HGRN_GROUP = 128
HGRN_STATE_BLOCK = 8
HGRN_SAFE_DECAY = 120.0
N_Q_HEADS = 8
N_KV_HEADS = 2
GQA = N_Q_HEADS // N_KV_HEADS
HEAD_DIM = 64
D_ATTN = N_Q_HEADS * HEAD_DIM
D_KV = N_KV_HEADS * HEAD_DIM
D_MIX = D_FOURIER + D_HGRN + D_ATTN
WINDOW = 128
ATTN_BLOCK = 128
ATTN_SCALE = HEAD_DIM ** -0.5
ROPE_BASE = 10000.0
NEG_BIG = -1e30
D_FF = 2816
N_EXPERTS = 8
TOP_K = 2
D_FF_EXPERT = 3584
DEEPNORM_ALPHA = (2 * DEPTH) ** 0.25
LN_EPS = 1e-5
RMS_EPS = 1e-6
N_MOD = 6
D_IN = 6 * 256 + D_ATTN + 2 * D_KV

TP = BATCH * SEQ
TS = DEC_BATCH * DEC_SEQ
T = TP + TS
N_COND = 1 + DEC_BATCH
COND_PAD = 8

TM = 512
TM_FFN = 512
TM_MOE = 1024
FFN_ROW_STEP = 256
TM_LN = 256
LANES = 128
VMEM_LIMIT = 56 * 1024 * 1024

F32 = jnp.float32
BF16 = jnp.bfloat16


def _cparams(sem, vmem=VMEM_LIMIT):
    return pltpu.CompilerParams(dimension_semantics=sem, vmem_limit_bytes=vmem)


def _ln(x):
    mu = jnp.mean(x, axis=-1, keepdims=True)
    xc = x - mu
    var = jnp.mean(xc * xc, axis=-1, keepdims=True)
    return xc * lax.rsqrt(var + LN_EPS)


def _silu(x):
    return x * jax.nn.sigmoid(x)


def _split3(a):
    p0 = a.astype(BF16)
    r1 = a - p0.astype(F32)
    p1 = r1.astype(BF16)
    r2 = r1 - p1.astype(F32)
    return p0, p1, r2.astype(BF16)


def _split2(a):
    hi = a.astype(BF16)
    return hi, (a - hi.astype(F32)).astype(BF16)


def _dot(a, b):
    return jnp.dot(a, b, preferred_element_type=F32)


def _dot_nt(a, b):
    return lax.dot_general(a, b, (((1,), (1,)), ((), ())), preferred_element_type=F32)


def _dot_tn(a, b):
    return lax.dot_general(a, b, (((0,), (0,)), ((), ())), preferred_element_type=F32)


def _cond_of_tile(i, tm):
    n_ctx = TP // tm
    return jnp.where(i < n_ctx, 0, 1 + (i - n_ctx) // (DEC_SEQ // tm))


def _mod_spec(l, tm, tile_of=lambda i, *_: i):
    return pl.BlockSpec((None, None, N_MOD, D_MODEL), lambda *a: (l, _cond_of_tile(tile_of(*a), tm), 0, 0))


def _layer_row_spec(l, width):
    return pl.BlockSpec((None, 1, width), lambda *_: (l, 0, 0))


def _mod_kernel(c_ref, w_ref, b_ref, o_ref):
    a = _silu(c_ref[...]).astype(BF16)
    o_ref[...] = _dot(a, w_ref[...].astype(BF16)) + b_ref[...]


def _modulation(cond, w_mod, b_mod):
    tn = 2048
    n_out = N_MOD * D_MODEL
    return pl.pallas_call(
        _mod_kernel,
        grid=(DEPTH, n_out // tn),
        in_specs=[
            pl.BlockSpec((COND_PAD, D_MODEL), lambda l, j: (0, 0)),
            pl.BlockSpec((None, D_MODEL, tn), lambda l, j: (l, 0, j)),
            pl.BlockSpec((None, 1, tn), lambda l, j: (l, 0, j)),
        ],
        out_specs=pl.BlockSpec((None, COND_PAD, tn), lambda l, j: (l, 0, j)),
        out_shape=jax.ShapeDtypeStruct((DEPTH, COND_PAD, n_out), F32),
        compiler_params=_cparams(("parallel", "parallel")),
        name="modulation",
    )(cond, w_mod, b_mod.reshape(DEPTH, 1, n_out))


def _inproj_kernel(*refs, split):
    if split:
        x0_ref, x1_ref, mod_ref, w_ref, o_ref, wb_ref = refs
        x = jnp.where(pl.program_id(0) < TP // TM, x0_ref[...], x1_ref[...])
    else:
        x_ref, mod_ref, w_ref, o_ref, wb_ref = refs
        x = x_ref[...]

    @pl.when(pl.program_id(0) == 0)
    def _():
        wb_ref[...] = w_ref[...].astype(BF16)

    h = _ln(x) * (1.0 + mod_ref[1:2, :]) + mod_ref[0:1, :]
    o_ref[...] = _dot(h.astype(BF16), wb_ref[...])


def _in_projection(x, mod, w_in, l):
    split = isinstance(x, tuple)
    n_ctx = TP // TM
    if split:
        x_specs = [pl.BlockSpec((TM, D_MODEL), lambda i: (jnp.minimum(i, n_ctx - 1), 0)),
                   pl.BlockSpec((TM, D_MODEL), lambda i: (jnp.maximum(i - n_ctx, 0), 0))]
    else:
        x_specs = [pl.BlockSpec((TM, D_MODEL), lambda i: (i, 0))]
        x = (x,)
    return pl.pallas_call(
        functools.partial(_inproj_kernel, split=split),
        grid=(T // TM,),
        in_specs=x_specs + [
            _mod_spec(l, TM),
            pl.BlockSpec((None, D_MODEL, D_IN), lambda i: (l, 0, 0), pipeline_mode=pl.Buffered(1)),
        ],
        out_specs=pl.BlockSpec((TM, D_IN), lambda i: (i, 0)),
        out_shape=jax.ShapeDtypeStruct((T, D_IN), F32),
        scratch_shapes=[pltpu.VMEM((D_MODEL, D_IN), BF16)],
        compiler_params=_cparams(("arbitrary",)),
        name="in_projection",
    )(*x, mod, w_in)


def _fourier_kernel(u_ref, ch_hi_ref, ch_lo_ref, tab_hi_ref, tab_lo_ref, wf_ref, o_ref,
                    ab_hi_ref, ab_lo_ref, *, scale):
    @pl.when(pl.program_id(1) == 0)
    def _():
        uh, ul = _split2(u_ref[...])
        ch_hi = ch_hi_ref[...]
        ab = _dot(uh, ch_hi) + _dot(uh, ch_lo_ref[...]) + _dot(ul, ch_hi)
        stacked = jnp.concatenate([ab[:, :D_FOURIER], ab[:, D_FOURIER:]], axis=0)
        hi, lo = _split2(stacked)
        ab_hi_ref[...] = hi
        ab_lo_ref[...] = lo

    th = tab_hi_ref[...]
    ab_hi = ab_hi_ref[...]
    z = _dot(th, ab_hi) + _dot(th, ab_lo_ref[...]) + _dot(tab_lo_ref[...], ab_hi)
    z = (z * scale).astype(BF16)
    o_ref[...] = _dot(z, wf_ref[...].astype(BF16)).astype(BF16)


def _dft_tables(n):
    blk = 32
    t = jnp.arange(n, dtype=jnp.int32)[None, :]
    ang1 = (((jnp.arange(n // blk, dtype=jnp.int32) * blk)[:, None] * t) % n).astype(F32) * (2.0 * math.pi / n)
    ang0 = ((jnp.arange(blk, dtype=jnp.int32)[:, None] * t) % n).astype(F32) * (2.0 * math.pi / n)
    c1, s1 = jnp.cos(ang1)[:, None, :], jnp.sin(ang1)[:, None, :]
    c0, s0 = jnp.cos(ang0)[None, :, :], jnp.sin(ang0)[None, :, :]
    cos = (c1 * c0 - s1 * s0).reshape(n, n)
    sin = (s1 * c0 + c1 * s0).reshape(n, n)
    return _split2(jnp.concatenate([cos, -sin], axis=1))


def _channel_tables():
    a = jnp.arange(D_FOURIER, dtype=jnp.int32)
    same = (a[:, None] // FOURIER_GROUP) == (a[None, :] // FOURIER_GROUP)
    prod = ((a[:, None] % FOURIER_GROUP) * (a[None, :] % FOURIER_GROUP)) % FOURIER_GROUP
    ang = prod.astype(F32) * (2.0 * math.pi / FOURIER_GROUP)
    c = jnp.where(same, jnp.cos(ang), 0.0)
    s = jnp.where(same, jnp.sin(ang), 0.0)
    return _split2(jnp.concatenate([c, s], axis=1))


def _fourier_mix(y_in, row0, nbatch, length, ch_tabs, pos_tabs, w_fourier, l):
    tr = min(length, 256)
    blk0 = row0 // length
    kern = functools.partial(_fourier_kernel, scale=1.0 / math.sqrt(length * FOURIER_GROUP))
    return pl.pallas_call(
        kern,
        grid=(nbatch, length // tr),
        in_specs=[
            pl.BlockSpec((length, D_FOURIER), lambda b, r: (blk0 + b, 0)),
            pl.BlockSpec((D_FOURIER, 2 * D_FOURIER), lambda b, r: (0, 0)),
            pl.BlockSpec((D_FOURIER, 2 * D_FOURIER), lambda b, r: (0, 0)),
            pl.BlockSpec((tr, 2 * length), lambda b, r: (r, 0)),
            pl.BlockSpec((tr, 2 * length), lambda b, r: (r, 0)),
            pl.BlockSpec((None, D_FOURIER, D_FOURIER), lambda b, r: (l, 0, 0)),
        ],
        out_specs=pl.BlockSpec((tr, D_FOURIER), lambda b, r: (b * (length // tr) + r, 0)),
        out_shape=jax.ShapeDtypeStruct((nbatch * length, D_FOURIER), BF16),
        scratch_shapes=[pltpu.VMEM((2 * length, D_FOURIER), BF16),
                        pltpu.VMEM((2 * length, D_FOURIER), BF16)],
        compiler_params=_cparams(("parallel", "arbitrary")),
        name=f"fourier_mix_{length}",
    )(y_in, ch_tabs[0], ch_tabs[1], pos_tabs[0], pos_tabs[1], w_fourier)


def _expand_state(cst):
    full = jnp.concatenate([cst] * N_HGRN_HEADS, axis=0)
    r = lax.broadcasted_iota(jnp.int32, full.shape, 0) // HGRN_HEAD
    c = lax.broadcasted_iota(jnp.int32, full.shape, 1) // HGRN_HEAD
    return jnp.where(r == c, full, jnp.zeros_like(full))


def _compress_state(full):
    lane_head = lax.broadcasted_iota(jnp.int32, (HGRN_HEAD, D_HGRN), 1) // HGRN_HEAD
    out = jnp.zeros((HGRN_HEAD, D_HGRN), full.dtype)
    for h in range(N_HGRN_HEADS):
        out = out + jnp.where(lane_head == h, full[h * HGRN_HEAD:(h + 1) * HGRN_HEAD, :], 0.0)
    return out


def _hgrn_pairwise_scan(d, nc, g_scr, k_scr, q_scr, v_ref, o_scr, st_scr, ones_bd):
    c = HGRN_CHUNK
    last = c - 1 if d == 0 else 0

    def chunk(ci, carry):
        row = lax.broadcasted_iota(jnp.int32, (c, c), 0)
        col = lax.broadcasted_iota(jnp.int32, (c, c), 1)
        tri = jnp.where((col <= row) if d == 0 else (col >= row), 1.0, 0.0).astype(BF16)
        cc = ci if d == 0 else nc - 1 - ci
        rows = pl.ds(pl.multiple_of(cc * c, c), c)
        g = g_scr[rows, :]
        kk = k_scr[rows, :]
        q = q_scr[rows, :]
        v = v_ref[rows, :]
        g0, g1, g2 = _split3(g)
        b = _dot(tri, g0) + _dot(tri, g1) + _dot(tri, g2)
        btot = b[last:last + 1, :]
        st = st_scr[...]
        o_inter = _dot_nt((q * jnp.exp(b)).astype(BF16), _expand_state(st.astype(BF16)))
        s_idx = lax.broadcasted_iota(jnp.int32, (c, c, D_HGRN), 0)
        t_idx = lax.broadcasted_iota(jnp.int32, (c, c, D_HGRN), 1)
        causal = (s_idx <= t_idx) if d == 0 else (s_idx >= t_idx)
        diff = b[None, :, :] - b[:, None, :]
        pair = q[None, :, :] * jnp.exp(jnp.minimum(diff, 0.0)) * kk[:, None, :]
        pair = jnp.where(causal, pair, 0.0).astype(BF16)
        attn = _dot(pair.reshape(c * c, D_HGRN), ones_bd).reshape(c, c, D_HGRN)
        o = o_inter + jnp.sum(attn * v[:, None, :], axis=0)
        if d == 0:
            o_scr[rows, :] = o
        else:
            o_scr[rows, :] = o_scr[rows, :] + o
        kd = kk * jnp.exp(btot - b)
        upd = _dot_tn(v.astype(BF16), kd.astype(BF16))
        st_scr[...] = jnp.exp(btot) * st + _compress_state(upd)
        return carry

    lax.fori_loop(0, nc, chunk, 0)


def _hgrn_factored_scan(d, nc, g_scr, k_scr, q_scr, v_ref, o_scr, st_scr, qd_scr, kd_scr, dec_scr, stb_scr):
    c = HGRN_CHUNK
    grp = HGRN_GROUP
    length = nc * c

    def group(gi, carry):
        rows = pl.ds(pl.multiple_of(gi * grp, grp), grp)
        r = lax.broadcasted_iota(jnp.int32, (grp, grp), 0)
        s = lax.broadcasted_iota(jnp.int32, (grp, grp), 1)
        same_chunk = (r // c) == (s // c)
        ordered = (s <= r) if d == 0 else (s >= r)
        tri = jnp.where(same_chunk & ordered, 1.0, 0.0).astype(BF16)
        ones_chunk = jnp.where(same_chunk, 1.0, 0.0).astype(BF16)
        g0, g1, g2 = _split3(g_scr[rows, :])
        b = _dot(tri, g0) + _dot(tri, g1) + _dot(tri, g2)
        tot = _dot(ones_chunk, g0) + _dot(ones_chunk, g1) + _dot(ones_chunk, g2)
        kk = k_scr[rows, :]
        q = q_scr[rows, :]
        half = 0.5 * tot
        qc = (q * jnp.exp(b - half)).astype(BF16)
        kh = (kk * jnp.exp(half - b)).astype(BF16)
        qd_scr[rows, :] = (q * jnp.exp(b)).astype(BF16)
        kd_scr[rows, :] = (kk * jnp.exp(tot - b)).astype(BF16)
        dec_scr[rows, :] = jnp.exp(tot)
        hs_row = lax.broadcasted_iota(jnp.int32, (N_HGRN_HEADS * grp, D_HGRN), 0) // grp
        hs_col = lax.broadcasted_iota(jnp.int32, (N_HGRN_HEADS * grp, D_HGRN), 1) // HGRN_HEAD
        same_head = hs_row == hs_col
        k_bd = jnp.where(same_head, jnp.concatenate([kh] * N_HGRN_HEADS, axis=0), 0.0)
        v_bd = jnp.where(same_head, jnp.concatenate([v_ref[rows, :].astype(BF16)] * N_HGRN_HEADS, axis=0), 0.0)
        t_idx = lax.broadcasted_iota(jnp.int32, (grp, N_HGRN_HEADS * grp), 0)
        s_idx = lax.broadcasted_iota(jnp.int32, (grp, N_HGRN_HEADS * grp), 1) % grp
        keep = ((t_idx // c) == (s_idx // c)) & ((s_idx <= t_idx) if d == 0 else (s_idx >= t_idx))
        attn = jnp.where(keep, _dot_nt(qc, k_bd), 0.0)
        o_intra = _dot(attn.astype(BF16), v_bd)
        if d == 0:
            o_scr[rows, :] = o_intra
        else:
            o_scr[rows, :] = o_scr[rows, :] + o_intra
        return carry

    lax.fori_loop(0, length // grp, group, 0, unroll=2)

    def state_step(ci, carry):
        cc = ci if d == 0 else nc - 1 - ci
        r0 = pl.multiple_of(cc * c, c)
        rows = pl.ds(r0, c)
        st = st_scr[...]
        stb_scr[cc] = st.astype(BF16)
        upd = _dot_tn(v_ref[rows, :].astype(BF16), kd_scr[rows, :])
        st_scr[...] = dec_scr[pl.ds(r0, 1), :] * st + _compress_state(upd)
        return carry

    lax.fori_loop(0, nc, state_step, 0, unroll=4)

    nb = HGRN_STATE_BLOCK
    blk_rows = nb * c

    def inter(bi, carry):
        rows = pl.ds(pl.multiple_of(bi * blk_rows, blk_rows), blk_rows)
        row_chunk = lax.broadcasted_iota(jnp.int32, (blk_rows, nb * D_HGRN), 0) // c
        col_chunk = lax.broadcasted_iota(jnp.int32, (blk_rows, nb * D_HGRN), 1) // D_HGRN
        q_bd = jnp.where(row_chunk == col_chunk, jnp.concatenate([qd_scr[rows, :]] * nb, axis=1), 0.0)
        states = jnp.concatenate([_expand_state(stb_scr[bi * nb + j]) for j in range(nb)], axis=1)
        o_scr[rows, :] = o_scr[rows, :] + _dot_nt(q_bd, states)
        return carry

    lax.fori_loop(0, nc // nb, inter, 0)


def _hgrn_kernel(hq_ref, hff_ref, hfb_ref, hi_ref, hg_ref, lb_ref, gw_ref, s0_ref,
                 o_ref, sfin_ref, q_scr, g_scr, k_scr, o_scr, dec_scr, qd_scr, kd_scr, st_scr, stb_scr,
                 *, length):
    c = HGRN_CHUNK
    nc = length // c
    q_scr[...] = _silu(hq_ref[...])
    r256 = lax.broadcasted_iota(jnp.int32, (D_HGRN, D_HGRN), 0) // HGRN_HEAD
    c256 = lax.broadcasted_iota(jnp.int32, (D_HGRN, D_HGRN), 1) // HGRN_HEAD
    ones_bd = jnp.where(r256 == c256, 1.0, 0.0).astype(BF16)

    for d in range(2):
        z_ref = hff_ref if d == 0 else hfb_ref
        one_minus_f = (1.0 - lb_ref[d:d + 1, :]) * jax.nn.sigmoid(-z_ref[...])
        g_scr[...] = jnp.log1p(-one_minus_f)
        k_scr[...] = one_minus_f
        st_scr[...] = s0_ref[d]
        chunk_decay = jnp.sum(g_scr[...].reshape(nc, c, D_HGRN), axis=1)
        safe = jnp.min(chunk_decay) >= -HGRN_SAFE_DECAY
        lax.cond(
            safe,
            functools.partial(_hgrn_factored_scan, d, nc, g_scr, k_scr, q_scr, hi_ref, o_scr, st_scr,
                              qd_scr, kd_scr, dec_scr, stb_scr),
            functools.partial(_hgrn_pairwise_scan, d, nc, g_scr, k_scr, q_scr, hi_ref, o_scr, st_scr, ones_bd))
        sfin_ref[d] = st_scr[...]

    o = o_scr[...]
    s0p, s1p, s2p = _split3(o * o)
    ms = (_dot(s0p, ones_bd) + _dot(s1p, ones_bd) + _dot(s2p, ones_bd)) * (1.0 / HGRN_HEAD)
    y = o * lax.rsqrt(ms + RMS_EPS) * gw_ref[...] * _silu(hg_ref[...])
    o_ref[...] = y.astype(BF16)


def _hgrn_mix(y_in, row0, nbatch, length, lb, gw, s0, l, s0_layer):
    blk0 = row0 // length
    col = lambda j: (lambda b: (blk0 + b, j))
    kern = functools.partial(_hgrn_kernel, length=length)
    return pl.pallas_call(
        kern,
        grid=(nbatch,),
        in_specs=[
            pl.BlockSpec((length, D_HGRN), col(1)),
            pl.BlockSpec((length, D_HGRN), col(2)),
            pl.BlockSpec((length, D_HGRN), col(3)),
            pl.BlockSpec((length, D_HGRN), col(4)),
            pl.BlockSpec((length, D_HGRN), col(5)),
            pl.BlockSpec((None, 2, D_HGRN), lambda b: (l, 0, 0)),
            _layer_row_spec(l, D_HGRN),
            pl.BlockSpec((None, None, 2, HGRN_HEAD, D_HGRN), lambda b: (b, s0_layer, 0, 0, 0)),
        ],
        out_specs=[
            pl.BlockSpec((length, D_HGRN), lambda b: (b, 0)),
            pl.BlockSpec((None, 2, HGRN_HEAD, D_HGRN), lambda b: (b, 0, 0, 0)),
        ],
        out_shape=[
            jax.ShapeDtypeStruct((nbatch * length, D_HGRN), BF16),
            jax.ShapeDtypeStruct((nbatch, 2, HGRN_HEAD, D_HGRN), F32),
        ],
        scratch_shapes=[pltpu.VMEM((length, D_HGRN), F32) for _ in range(5)]
        + [pltpu.VMEM((length, D_HGRN), BF16) for _ in range(2)]
        + [pltpu.VMEM((HGRN_HEAD, D_HGRN), F32), pltpu.VMEM((length // HGRN_CHUNK, HGRN_HEAD, D_HGRN), BF16)],
        compiler_params=_cparams(("parallel",)),
        name=f"hgrn_mix_{length}",
    )(y_in, y_in, y_in, y_in, y_in, lb, gw, s0)


def _states_to_kernel_layout(s):
    return jnp.moveaxis(s, -1, -3).reshape(s.shape[:-3] + (HGRN_HEAD, D_HGRN))


def _states_from_kernel_layout(st):
    return jnp.moveaxis(st.reshape(st.shape[:-1] + (N_HGRN_HEADS, HGRN_HEAD)), -3, -1)


def _group_attention(q_heads, sinks, kv_parts):
    rows = q_heads[0].shape[0]
    qg = jnp.concatenate([(q * ATTN_SCALE).astype(BF16) for q in q_heads], axis=0)
    head = lax.broadcasted_iota(jnp.int32, (len(q_heads) * rows, 1), 0) // rows
    sink = jnp.full(head.shape, sinks[0], F32)
    for i in range(1, len(q_heads)):
        sink = jnp.where(head == i, sinks[i], sink)
    m = sink
    scores = []
    for k, _, bias in kv_parts:
        s = _dot_nt(qg, k)
        if bias is not None:
            s = s + bias
        m = jnp.maximum(m, jnp.max(s, axis=1, keepdims=True))
        scores.append(s)
    den = jnp.exp(sink - m)
    out = None
    for s, (_, v, _) in zip(scores, kv_parts):
        e = jnp.exp(s - m)
        den = den + jnp.sum(e, axis=1, keepdims=True)
        o = _dot(e.astype(BF16), v)
        out = o if out is None else out + o
    out = out * (1.0 / den)
    return [out[i * rows:(i + 1) * rows, :] for i in range(len(q_heads))]


def _ctx_attn_kernel(sink_ref, q_ref, k_ref, v_ref, o_ref, *, layer):
    outs = []
    for g in range(N_KV_HEADS):
        sl = slice(g * HEAD_DIM, (g + 1) * HEAD_DIM)
        heads = range(g * GQA, (g + 1) * GQA)
        outs += _group_attention(
            [q_ref[:, h * HEAD_DIM:(h + 1) * HEAD_DIM] for h in heads],
            [sink_ref[layer, h] for h in heads],
            [(k_ref[:, sl].astype(BF16), v_ref[:, sl].astype(BF16), None)])
    o_ref[...] = jnp.concatenate(outs, axis=1).astype(BF16)


def _context_attention(y_in, sink, l):
    qcol = (6 * 256) // D_ATTN
    kcol = (6 * 256 + D_ATTN) // D_KV
    return pl.pallas_call(
        functools.partial(_ctx_attn_kernel, layer=l),
        grid=(BATCH,),
        in_specs=[
            pl.BlockSpec(memory_space=pltpu.SMEM),
            pl.BlockSpec((SEQ, D_ATTN), lambda b: (b, qcol)),
            pl.BlockSpec((SEQ, D_KV), lambda b: (b, kcol)),
            pl.BlockSpec((SEQ, D_KV), lambda b: (b, kcol + 1)),
        ],
        out_specs=pl.BlockSpec((SEQ, D_ATTN), lambda b: (b, 0)),
        out_shape=jax.ShapeDtypeStruct((TP, D_ATTN), BF16),
        compiler_params=_cparams(("parallel",)),
        name="context_attention",
    )(sink, y_in, y_in, y_in)


def _rope(x, cos, sin):
    lane = lax.broadcasted_iota(jnp.int32, x.shape, 1)
    n_freq = HEAD_DIM // 4
    first = (lane % (2 * n_freq)) < n_freq
    swapped = jnp.where(first, pltpu.roll(x, LANES - n_freq, axis=1), pltpu.roll(x, n_freq, axis=1))
    return x * cos + swapped * sin


def _lat_attn_kernel(sink_ref, q_ref, k_ref, v_ref, kc_ref, vc_ref, cos_ref, sin_ref, bias_ref, o_ref, *, layer):
    j = pl.program_id(1)
    nb = DEC_SEQ // ATTN_BLOCK
    blk = ATTN_BLOCK
    q0 = pl.multiple_of(j * blk, blk)
    cos_q = cos_ref[pl.ds(q0, blk), :]
    sin_q = sin_ref[pl.ds(q0, blk), :]

    starts = [jnp.maximum(j - 1, 0), j, jnp.minimum(j + 1, nb - 1)]
    k_band, v_band = [], []
    for st in starts:
        r0 = pl.multiple_of(st * blk, blk)
        kb = _rope(k_ref[pl.ds(r0, blk), :], cos_ref[pl.ds(r0, blk), :], sin_ref[pl.ds(r0, blk), :])
        k_band.append(kb)
        v_band.append(v_ref[pl.ds(r0, blk), :])
    k_loc = jnp.concatenate(k_band, axis=0)
    v_loc = jnp.concatenate(v_band, axis=0)
    band_bias = bias_ref[...]

    heads_per_chunk = LANES // HEAD_DIM
    q_chunks = [_rope(q_ref[:, cg * LANES:(cg + 1) * LANES], cos_q, sin_q)
                for cg in range(N_Q_HEADS // heads_per_chunk)]

    def q_head(h):
        off = (h % heads_per_chunk) * HEAD_DIM
        return q_chunks[h // heads_per_chunk][:, off:off + HEAD_DIM]

    outs = []
    for g in range(N_KV_HEADS):
        sl = slice(g * HEAD_DIM, (g + 1) * HEAD_DIM)
        heads = range(g * GQA, (g + 1) * GQA)
        outs += _group_attention(
            [q_head(h) for h in heads],
            [sink_ref[layer, h] for h in heads],
            [(k_loc[:, sl].astype(BF16), v_loc[:, sl].astype(BF16), band_bias),
             (kc_ref[:, sl].astype(BF16), vc_ref[:, sl].astype(BF16), None)])
    o_ref[...] = jnp.concatenate(outs, axis=1).astype(BF16)


def _band_bias():
    blk = ATTN_BLOCK
    r = np.arange(GQA * blk)[:, None] % blk
    c = np.arange(3 * blk)[None, :]
    in_window = np.abs(r + blk - c) <= WINDOW
    exists = [c >= blk, c >= 0, c < 2 * blk]
    return jnp.asarray(np.stack([np.where(in_window & e, 0.0, NEG_BIG) for e in exists]).astype(np.float32))


def _latent_attention(y_in, sink, kc, vc, l, cos_t, sin_t, band_bias):
    nb = DEC_SEQ // ATTN_BLOCK
    qrow0 = TP // ATTN_BLOCK
    krow0 = TP // DEC_SEQ
    qcol = (6 * 256) // D_ATTN
    kcol = (6 * 256 + D_ATTN) // D_KV
    return pl.pallas_call(
        functools.partial(_lat_attn_kernel, layer=l),
        grid=(DEC_BATCH, nb),
        in_specs=[
            pl.BlockSpec(memory_space=pltpu.SMEM),
            pl.BlockSpec((ATTN_BLOCK, D_ATTN), lambda b, j: (qrow0 + b * nb + j, qcol)),
            pl.BlockSpec((DEC_SEQ, D_KV), lambda b, j: (krow0 + b, kcol)),
            pl.BlockSpec((DEC_SEQ, D_KV), lambda b, j: (krow0 + b, kcol + 1)),
            pl.BlockSpec((None, None, PAST_LEN, D_KV), lambda b, j: (b, l, 0, 0)),
            pl.BlockSpec((None, None, PAST_LEN, D_KV), lambda b, j: (b, l, 0, 0)),
            pl.BlockSpec((DEC_SEQ, LANES), lambda b, j: (0, 0)),
            pl.BlockSpec((DEC_SEQ, LANES), lambda b, j: (0, 0)),
            pl.BlockSpec((None, GQA * ATTN_BLOCK, 3 * ATTN_BLOCK),
                         lambda b, j: (jnp.where(j == 0, 0, jnp.where(j == nb - 1, 2, 1)), 0, 0)),
        ],
        out_specs=pl.BlockSpec((ATTN_BLOCK, D_ATTN), lambda b, j: (b * nb + j, 0)),
        out_shape=jax.ShapeDtypeStruct((TS, D_ATTN), BF16),
        compiler_params=_cparams(("parallel", "parallel")),
        name="latent_attention",
    )(sink, y_in, y_in, y_in, kc, vc, cos_t, sin_t, band_bias)


def _rope_tables():
    t = jnp.arange(DEC_SEQ)
    rows = (t // GRID_W).astype(F32)
    cols = (t % GRID_W).astype(F32)
    n_freq = HEAD_DIM // 4
    inv = ROPE_BASE ** (-jnp.arange(n_freq, dtype=F32) / n_freq)
    ar = rows[:, None] * inv
    ac = cols[:, None] * inv
    cos = jnp.concatenate([jnp.cos(ar), jnp.cos(ar), jnp.cos(ac), jnp.cos(ac)], axis=1)
    sin = jnp.concatenate([-jnp.sin(ar), jnp.sin(ar), -jnp.sin(ac), jnp.sin(ac)], axis=1)
    return jnp.tile(cos, (1, N_KV_HEADS)), jnp.tile(sin, (1, N_KV_HEADS))


def _top2_route(h2, w_ref, b_ref, idx_ref, p_ref):
    logits = _dot(h2.astype(BF16), w_ref[...].astype(BF16)) + b_ref[...]
    lane = lax.broadcasted_iota(jnp.int32, logits.shape, 1)
    m1 = jnp.max(logits, axis=1, keepdims=True)
    i1 = jnp.min(jnp.where(logits == m1, lane, LANES), axis=1, keepdims=True)
    rest = jnp.where(lane == i1, -jnp.inf, logits)
    m2 = jnp.max(rest, axis=1, keepdims=True)
    i2 = jnp.min(jnp.where(rest == m2, lane, LANES), axis=1, keepdims=True)
    e = jnp.exp(m2 - m1)
    inv = 1.0 / (1.0 + e)
    idx_ref[...] = jnp.where(lane == 0, i1, jnp.where(lane == 1, i2, 0))
    p_ref[...] = jnp.where(lane == 0, inv, jnp.where(lane == 1, e * inv, 0.0))


def _outproj_kernel(a0_ref, a1_ref, b0_ref, b1_ref, c0_ref, c1_ref, *rest, route, split):
    is_ctx = pl.program_id(0) < TP // TM
    pick = lambda r0, r1: jnp.where(is_ctx, r0[...], r1[...])
    if split:
        x = pick(rest[0], rest[1])
        rest = rest[2:]
    else:
        x = rest[0][...]
        rest = rest[1:]
    mod_ref, g_ref, beta_ref, w_ref = rest[:4]
    rest = rest[4:]
    if route:
        wr_ref, br_ref, x1_ref, h2_ref, idx_ref, p_ref, wb_ref = rest
    else:
        x1_ref, h2_ref, wb_ref = rest

    @pl.when(pl.program_id(0) == 0)
    def _():
        wb_ref[...] = w_ref[...].astype(BF16)

    y = (_dot(pick(a0_ref, a1_ref), wb_ref[0:D_FOURIER, :])
         + _dot(pick(b0_ref, b1_ref), wb_ref[D_FOURIER:D_FOURIER + D_HGRN, :])
         + _dot(pick(c0_ref, c1_ref), wb_ref[D_FOURIER + D_HGRN:D_MIX, :]))
    x1 = _ln(DEEPNORM_ALPHA * x + mod_ref[2:3, :] * y) * g_ref[...] + beta_ref[...]
    x1_ref[...] = x1
    h2 = _ln(x1) * (1.0 + mod_ref[4:5, :]) + mod_ref[3:4, :]
    h2_ref[...] = h2
    if route:
        _top2_route(h2, wr_ref, br_ref, idx_ref, p_ref)


def _out_projection(mixed, x, mod, ln_g, ln_b, w_out, l, router=None):
    n_ctx = TP // TM
    row = lambda w: pl.BlockSpec((TM, w), lambda i: (i, 0))
    ctx = lambda w: pl.BlockSpec((TM, w), lambda i: (jnp.minimum(i, n_ctx - 1), 0))
    lat = lambda w: pl.BlockSpec((TM, w), lambda i: (jnp.maximum(i - n_ctx, 0), 0))
    vec = _layer_row_spec(l, D_MODEL)
    split = isinstance(x, tuple)
    x_specs = [ctx(D_MODEL), lat(D_MODEL)] if split else [row(D_MODEL)]
    x = x if split else (x,)
    in_specs = [
        ctx(D_FOURIER), lat(D_FOURIER), ctx(D_HGRN), lat(D_HGRN), ctx(D_ATTN), lat(D_ATTN), *x_specs,
        _mod_spec(l, TM),
        vec, vec,
        pl.BlockSpec((None, D_MIX, D_MODEL), lambda i: (l, 0, 0), pipeline_mode=pl.Buffered(1)),
    ]
    args = [mixed[0][0], mixed[0][1], mixed[1][0], mixed[1][1], mixed[2][0], mixed[2][1], *x, mod, ln_g, ln_b,
            w_out]
    out_specs = [row(D_MODEL), row(D_MODEL)]
    out_shape = [jax.ShapeDtypeStruct((T, D_MODEL), F32), jax.ShapeDtypeStruct((T, D_MODEL), F32)]
    if router is not None:
        w, b, ri = router
        in_specs += [pl.BlockSpec((None, D_MODEL, LANES), lambda i: (ri, 0, 0)), _layer_row_spec(ri, LANES)]
        args += [w, b]
        out_specs += [row(LANES), row(LANES)]
        out_shape += [jax.ShapeDtypeStruct((T, LANES), jnp.int32), jax.ShapeDtypeStruct((T, LANES), F32)]
    return pl.pallas_call(
        functools.partial(_outproj_kernel, route=router is not None, split=split),
        grid=(T // TM,),
        in_specs=in_specs,
        out_specs=out_specs,
        out_shape=out_shape,
        scratch_shapes=[pltpu.VMEM((D_MIX, D_MODEL), BF16)],
        compiler_params=_cparams(("arbitrary",)),
        name="out_projection_route" if router is not None else "out_projection",
    )(*args)


def _expert_changed(te_ref, i):
    return (i == 0) | (te_ref[i] != te_ref[jnp.maximum(i - 1, 0)])


def _post_ffn(x1, y, g2, ln_g, ln_b):
    return _ln(DEEPNORM_ALPHA * x1 + g2 * y) * ln_g + ln_b


def _for_used_rows(rows, out_ref, compute):
    tm = out_ref.shape[0]
    for n in range(FFN_ROW_STEP, tm + 1, FFN_ROW_STEP):
        @pl.when((rows > n - FFN_ROW_STEP) & (rows <= n))
        def _(n=n):
            compute(n)
            if n < tm:
                out_ref[n:, :] = jnp.zeros((tm - n, out_ref.shape[1]), out_ref.dtype)

    @pl.when(rows == 0)
    def _():
        out_ref[...] = jnp.zeros_like(out_ref)


def _ffn_up_kernel(te_ref, nv_ref, tr_ref, x_ref, wg_ref, wu_ref, h_ref, wb_ref):
    i = pl.program_id(1)
    fh = wg_ref.shape[-1]

    @pl.when(_expert_changed(te_ref, i))
    def _():
        wb_ref[:, :fh] = wg_ref[...].astype(BF16)
        wb_ref[:, fh:] = wu_ref[...].astype(BF16)

    def compute(n):
        ab = _dot(x_ref[0:n, :].astype(BF16), wb_ref[...])
        h_ref[0:n, :] = (_silu(ab[:, :fh]) * ab[:, fh:]).astype(BF16)

    _for_used_rows(tr_ref[i], h_ref, compute)


def _ffn_down_kernel(te_ref, nv_ref, tr_ref, h_ref, wd_ref, *rest, norm):
    if norm:
        x1_ref, mod_ref, g_ref, beta_ref, y_ref, wdb_ref = rest
    else:
        y_ref, wdb_ref = rest
    i = pl.program_id(1)

    @pl.when(_expert_changed(te_ref, i))
    def _():
        wdb_ref[...] = wd_ref[...].astype(BF16)

    if norm:
        y = _dot(h_ref[...], wdb_ref[...])
        y_ref[...] = _post_ffn(x1_ref[...], y, mod_ref[5:6, :], g_ref[...], beta_ref[...])
    else:
        def compute(n):
            y_ref[0:n, :] = _dot(h_ref[0:n, :], wdb_ref[...])

        _for_used_rows(tr_ref[i], y_ref, compute)


def _grouped_ffn(x_rows, tile_expert, n_valid, tile_rows, w_gate, w_up, w_down, f_splits, tm, norm_args=None):
    r = x_rows.shape[0]
    nt = r // tm
    f = w_gate.shape[-1]
    fh = f // f_splits
    assert fh * f_splits == f and fh % LANES == 0
    used = lambda i, nv: jnp.minimum(i, nv[0] - 1)
    h = pl.pallas_call(
        _ffn_up_kernel,
        grid_spec=pltpu.PrefetchScalarGridSpec(
            num_scalar_prefetch=3,
            grid=(f_splits, nt),
            in_specs=[
                pl.BlockSpec((tm, D_MODEL), lambda j, i, te, nv, tr: (used(i, nv), 0)),
                pl.BlockSpec((None, D_MODEL, fh), lambda j, i, te, nv, tr: (te[i], 0, j)),
                pl.BlockSpec((None, D_MODEL, fh), lambda j, i, te, nv, tr: (te[i], 0, j)),
            ],
            out_specs=pl.BlockSpec((tm, fh), lambda j, i, te, nv, tr: (i, j)),
            scratch_shapes=[pltpu.VMEM((D_MODEL, 2 * fh), BF16)],
        ),
        out_shape=jax.ShapeDtypeStruct((r, f), BF16),
        compiler_params=_cparams(("arbitrary", "arbitrary")),
        name="ffn_up",
    )(tile_expert, n_valid, tile_rows, x_rows, w_gate, w_up)

    d_splits = 1 if norm_args is not None else 2
    dh = D_MODEL // d_splits
    in_specs = [
        pl.BlockSpec((tm, f), lambda j, i, te, nv, tr: (used(i, nv), 0)),
        pl.BlockSpec((None, f, dh), lambda j, i, te, nv, tr: (te[i], 0, j)),
    ]
    args = [tile_expert, n_valid, tile_rows, h, w_down]
    if norm_args is not None:
        assert r == T
        x1, mod, ln_g, ln_b, l = norm_args
        vec = _layer_row_spec(l, D_MODEL)
        in_specs += [
            pl.BlockSpec((tm, D_MODEL), lambda j, i, te, nv, tr: (i, 0)),
            _mod_spec(l, tm, lambda j, i, *_: i),
            vec, vec]
        args += [x1, mod, ln_g, ln_b]
    return pl.pallas_call(
        functools.partial(_ffn_down_kernel, norm=norm_args is not None),
        grid_spec=pltpu.PrefetchScalarGridSpec(
            num_scalar_prefetch=3,
            grid=(d_splits, nt),
            in_specs=in_specs,
            out_specs=pl.BlockSpec((tm, dh), lambda j, i, te, nv, tr: (i, j)),
            scratch_shapes=[pltpu.VMEM((f, dh), BF16)],
        ),
        out_shape=jax.ShapeDtypeStruct((r, D_MODEL), F32),
        compiler_params=_cparams(("arbitrary", "arbitrary")),
        name="ffn_down_norm" if norm_args is not None else "ffn_down",
    )(*args)


def _row_copy(src, dst, s, d, sem):
    return pltpu.make_async_copy(src.at[pl.ds(s, 1), :], dst.at[pl.ds(d, 1), :], sem)


def _dispatch_kernel(dest_ref, last_ref, x_ref, o_hbm, zero_scr, sem, zsem):
    i = pl.program_id(0)

    @pl.when(i == 0)
    def _():
        zero_scr[...] = jnp.zeros_like(zero_scr)

        def fill(tile):
            r0 = pl.multiple_of(tile * TM_MOE, TM_MOE)
            return pltpu.make_async_copy(zero_scr, o_hbm.at[pl.ds(r0, TM_MOE), :], zsem)

        n_tiles = o_hbm.shape[0] // TM_MOE
        min_tiles = (T * TOP_K) // TM_MOE
        jobs = [(last_ref[e] >= 0, last_ref[e]) for e in range(N_EXPERTS)]
        jobs += [(t >= last_ref[N_EXPERTS], t) for t in range(min_tiles, n_tiles)]
        for go, tile in jobs:
            @pl.when(go)
            def _(tile=tile):
                fill(tile).start()
        for go, tile in jobs:
            @pl.when(go)
            def _(tile=tile):
                fill(tile).wait()

    base = i * TM_LN

    def issue(r, carry):
        for k in range(TOP_K):
            _row_copy(x_ref, o_hbm, r, dest_ref[(base + r) * TOP_K + k], sem).start(priority=k % 2)
        return carry

    lax.fori_loop(0, TM_LN, issue, 0, unroll=8)
    for k in range(TOP_K):
        pltpu.make_async_copy(x_ref, o_hbm.at[pl.ds(0, TM_LN), :], sem).wait()


def _dispatch(h2, dest, last_tile, n_rows):
    return pl.pallas_call(
        _dispatch_kernel,
        grid_spec=pltpu.PrefetchScalarGridSpec(
            num_scalar_prefetch=2,
            grid=(T // TM_LN,),
            in_specs=[pl.BlockSpec((TM_LN, D_MODEL), lambda i, d, l: (i, 0))],
            out_specs=pl.BlockSpec(memory_space=pl.ANY),
            scratch_shapes=[pltpu.VMEM((TM_MOE, D_MODEL), F32), pltpu.SemaphoreType.DMA(()),
                            pltpu.SemaphoreType.DMA(())],
        ),
        out_shape=jax.ShapeDtypeStruct((n_rows, D_MODEL), F32),
        compiler_params=_cparams(("arbitrary",)),
        name="dispatch",
    )(dest, last_tile, h2)


def _combine_kernel(pos_ref, x1_ref, p_ref, y_hbm, mod_ref, g_ref, beta_ref, o_ref, buf, sem):
    i = pl.program_id(0)
    n = pl.num_programs(0)

    def fetch(step, slot):
        def issue(r, carry):
            for k in range(TOP_K):
                src = pos_ref[(step * TM_LN + r) * TOP_K + k]
                _row_copy(y_hbm, buf.at[slot, k], src, r, sem.at[slot, k]).start(priority=k % 2)
            return carry

        lax.fori_loop(0, TM_LN, issue, 0, unroll=8)

    @pl.when(i == 0)
    def _():
        fetch(0, 0)

    slot = i % 2

    @pl.when(i + 1 < n)
    def _():
        fetch(i + 1, 1 - slot)

    for k in range(TOP_K):
        pltpu.make_async_copy(y_hbm.at[pl.ds(0, TM_LN), :], buf.at[slot, k], sem.at[slot, k]).wait()
    y = p_ref[:, 0:1] * buf[slot, 0]
    for k in range(1, TOP_K):
        y = y + p_ref[:, k:k + 1] * buf[slot, k]
    o_ref[...] = _post_ffn(x1_ref[...], y, mod_ref[5:6, :], g_ref[...], beta_ref[...])


def _combine_norm(x1, p, y_rows, pos, mod, ln_g, ln_b, l):
    row = pl.BlockSpec((TM_LN, D_MODEL), lambda i, s: (i, 0))
    vec = _layer_row_spec(l, D_MODEL)
    return pl.pallas_call(
        _combine_kernel,
        grid_spec=pltpu.PrefetchScalarGridSpec(
            num_scalar_prefetch=1,
            grid=(T // TM_LN,),
            in_specs=[row, pl.BlockSpec((TM_LN, LANES), lambda i, s: (i, 0)),
                      pl.BlockSpec(memory_space=pl.ANY),
                      _mod_spec(l, TM_LN),
                      vec, vec],
            out_specs=row,
            scratch_shapes=[pltpu.VMEM((2, TOP_K, TM_LN, D_MODEL), F32),
                            pltpu.SemaphoreType.DMA((2, TOP_K))],
        ),
        out_shape=jax.ShapeDtypeStruct((T, D_MODEL), F32),
        compiler_params=_cparams(("arbitrary",)),
        name="combine_norm",
    )(pos, x1, p, y_rows, mod, ln_g, ln_b)


def _moe_plan(idx):
    n_assign = T * TOP_K
    n_tiles = n_assign // TM_MOE + N_EXPERTS
    e = idx[:, :TOP_K].reshape(n_assign)
    onehot = (e[:, None] == jnp.arange(N_EXPERTS, dtype=jnp.int32)[None, :]).astype(jnp.int32)
    csum = jnp.cumsum(onehot, axis=0)
    counts = csum[-1]
    tiles_e = (counts + TM_MOE - 1) // TM_MOE
    tile_end = jnp.cumsum(tiles_e)
    row0 = (tile_end - tiles_e) * TM_MOE
    dest = jnp.sum((csum - 1 + row0[None, :]) * onehot, axis=1)
    n_valid = tile_end[-1]
    tiles = jnp.arange(n_tiles, dtype=jnp.int32)
    tile_id = jnp.minimum(tiles, n_valid - 1)
    tile_expert = jnp.sum((tile_id[:, None] >= tile_end[None, :]).astype(jnp.int32), axis=1)
    tile_start = tile_end - tiles_e
    in_group = (tiles[:, None] >= tile_start[None, :]) & (tiles[:, None] < tile_end[None, :])
    left = counts[None, :] - (tiles[:, None] - tile_start[None, :]) * TM_MOE
    tile_rows = jnp.sum(jnp.where(in_group, jnp.clip(left, 0, TM_MOE), 0), axis=1)
    last_tile = jnp.concatenate([jnp.where(tiles_e > 0, tile_end - 1, -1), n_valid.reshape(1)])
    return (dest.astype(jnp.int32), tile_expert.astype(jnp.int32), n_valid.reshape(1).astype(jnp.int32),
            tile_rows.astype(jnp.int32), last_tile.astype(jnp.int32), n_tiles * TM_MOE)


def _moe_ffn(h2, x1, idx, p, mod, ln_g, ln_b, l, w_gate, w_up, w_down, expert0):
    dest, tile_expert, n_valid, tile_rows, last_tile, n_rows = _moe_plan(idx)
    x_rows = _dispatch(h2, dest, last_tile, n_rows)
    y_rows = _grouped_ffn(x_rows, tile_expert + expert0, n_valid, tile_rows, w_gate, w_up, w_down, 4, TM_MOE)
    return _combine_norm(x1, p, y_rows, dest, mod, ln_g, ln_b, l)


def _dense_ffn(h2, x1, mod, ln_g, ln_b, l, w_gate, w_up, w_down, index):
    nt = T // TM_FFN
    return _grouped_ffn(h2, jnp.full((nt,), index, jnp.int32), jnp.full((1,), nt, jnp.int32),
                        jnp.full((nt,), TM_FFN, jnp.int32), w_gate, w_up, w_down, 2, TM_FFN,
                        norm_args=(x1, mod, ln_g, ln_b, l))


def kernel(x_prompt, x_sample, c, cache_k, cache_v, state_hgrn, c_ctx, w_mod, b_mod, w_in, w_fourier, lb_logits, hgrn_norm, attn_sink, w_out, ln1_g, ln1_b, ln2_g, ln2_b, ffn_w_gate, ffn_w_up, ffn_w_down, router_w, router_b, moe_w_gate, moe_w_up, moe_w_down):
    lb_sm = jax.nn.softmax(lb_logits.astype(F32), axis=0)
    lower_bounds = jnp.clip(jnp.cumsum(lb_sm, axis=0) - lb_sm[0], 0.0, 1.0).reshape(DEPTH, 2, D_HGRN)
    gw = jnp.tile(hgrn_norm, (1, N_HGRN_HEADS)).reshape(DEPTH, 1, D_HGRN)
    per_layer = lambda v: v.reshape(DEPTH, 1, D_MODEL)
    ln1_g, ln1_b, ln2_g, ln2_b = per_layer(ln1_g), per_layer(ln1_b), per_layer(ln2_g), per_layer(ln2_b)
    n_moe = moe_w_gate.shape[0]
    router_wp = jnp.zeros((n_moe, D_MODEL, LANES), F32).at[:, :, :N_EXPERTS].set(router_w)
    router_bp = jnp.full((n_moe, 1, LANES), NEG_BIG, F32).at[:, 0, :N_EXPERTS].set(router_b)
    s0_lat = _states_to_kernel_layout(state_hgrn)
    s0_ctx = jnp.zeros((BATCH, 1, 2, HGRN_HEAD, D_HGRN), F32)

    cond = jnp.zeros((COND_PAD, D_MODEL), F32).at[0].set(c_ctx).at[1:N_COND].set(c)
    mod = _modulation(cond, w_mod, b_mod).reshape(DEPTH, COND_PAD, N_MOD, D_MODEL)

    ch_tabs = _channel_tables()
    pos_tabs_ctx = _dft_tables(SEQ)
    pos_tabs_lat = _dft_tables(DEC_SEQ)
    cos_t, sin_t = _rope_tables()
    band_bias = _band_bias()
    kc = cache_k.reshape(DEC_BATCH, DEPTH, PAST_LEN, D_KV)
    vc = cache_v.reshape(DEC_BATCH, DEPTH, PAST_LEN, D_KV)
    moe_wg = moe_w_gate.reshape(n_moe * N_EXPERTS, D_MODEL, D_FF_EXPERT)
    moe_wu = moe_w_up.reshape(n_moe * N_EXPERTS, D_MODEL, D_FF_EXPERT)
    moe_wd = moe_w_down.reshape(n_moe * N_EXPERTS, D_FF_EXPERT, D_MODEL)

    x = (x_prompt.reshape(TP, D_MODEL), x_sample.reshape(TS, D_MODEL))
    new_kv, new_s = [], []
    for l in range(DEPTH):
        y_in = _in_projection(x, mod, w_in, l)

        a_ctx = _fourier_mix(y_in, 0, BATCH, SEQ, ch_tabs, pos_tabs_ctx, w_fourier, l)
        a_lat = _fourier_mix(y_in, TP, DEC_BATCH, DEC_SEQ, ch_tabs, pos_tabs_lat, w_fourier, l)
        b_ctx, s_ctx = _hgrn_mix(y_in, 0, BATCH, SEQ, lower_bounds, gw, s0_ctx, l, 0)
        b_lat, _ = _hgrn_mix(y_in, TP, DEC_BATCH, DEC_SEQ, lower_bounds, gw, s0_lat, l, l)
        c_ctx_out = _context_attention(y_in, attn_sink, l)
        c_lat = _latent_attention(y_in, attn_sink, kc, vc, l, cos_t, sin_t, band_bias)

        k0 = 6 * 256 + D_ATTN
        new_kv.append(y_in[:TP, k0:k0 + 2 * D_KV])
        new_s.append(s_ctx)

        mixed = ((a_ctx, a_lat), (b_ctx, b_lat), (c_ctx_out, c_lat))
        i = l // 2
        if l % 2 == 0:
            x1, h2 = _out_projection(mixed, x, mod, ln1_g, ln1_b, w_out, l)
            x = _dense_ffn(h2, x1, mod, ln2_g, ln2_b, l, ffn_w_gate, ffn_w_up, ffn_w_down, i)
        else:
            x1, h2, idx, p = _out_projection(mixed, x, mod, ln1_g, ln1_b, w_out, l,
                                             router=(router_wp, router_bp, i))
            x = _moe_ffn(h2, x1, idx, p, mod, ln2_g, ln2_b, l, moe_wg, moe_wu, moe_wd, i * N_EXPERTS)

    xp = x[:TP].reshape(BATCH, SEQ, D_MODEL)
    xs = x[TP:].reshape(DEC_BATCH, DEC_SEQ, D_MODEL)
    kv = jnp.stack(new_kv, axis=0).reshape(DEPTH, BATCH, SEQ, 2, N_KV_HEADS, HEAD_DIM)
    kv = jnp.transpose(kv, (3, 1, 0, 2, 4, 5))
    states = _states_from_kernel_layout(jnp.stack(new_s, axis=1))
    return (xp, xs, kv[0], kv[1], states)
```

```python
import functools
import math

import jax
import jax.numpy as jnp
import numpy as np
from jax import lax
from jax.experimental import pallas as pl
from jax.experimental.pallas import tpu as pltpu

D_MODEL = 1024
BATCH = 16
SEQ = 256
DEPTH = 4
DEC_BATCH = 2
DEC_SEQ = 2048
PAST_LEN = 512
GRID_W = 64
D_FOURIER = 256
N_FOURIER_GROUPS = 4
FOURIER_GROUP = D_FOURIER // N_FOURIER_GROUPS
D_HGRN = 256
N_HGRN_HEADS = 4
HGRN_HEAD = D_HGRN // N_HGRN_HEADS
HGRN_CHUNK = 32
HGRN_GROUP = 128
HGRN_STATE_BLOCK = 8
HGRN_SAFE_DECAY = 120.0
N_Q_HEADS = 8
N_KV_HEADS = 2
GQA = N_Q_HEADS // N_KV_HEADS
HEAD_DIM = 64
D_ATTN = N_Q_HEADS * HEAD_DIM
D_KV = N_KV_HEADS * HEAD_DIM
D_MIX = D_FOURIER + D_HGRN + D_ATTN
WINDOW = 128
ATTN_BLOCK = 128
ATTN_SCALE = HEAD_DIM ** -0.5
ROPE_BASE = 10000.0
NEG_BIG = -1e30
D_FF = 2816
N_EXPERTS = 8
TOP_K = 2
D_FF_EXPERT = 3584
DEEPNORM_ALPHA = (2 * DEPTH) ** 0.25
LN_EPS = 1e-5
RMS_EPS = 1e-6
N_MOD = 6
D_IN = 6 * 256 + D_ATTN + 2 * D_KV

TP = BATCH * SEQ
TS = DEC_BATCH * DEC_SEQ
T = TP + TS
N_COND = 1 + DEC_BATCH
COND_PAD = 8

TM = 512
TM_FFN = 512
TM_MOE = 1024
FFN_ROW_STEP = 256
TM_LN = 256
ROW_CHUNK = 256
LANES = 128
VMEM_LIMIT = 56 * 1024 * 1024

F32 = jnp.float32
BF16 = jnp.bfloat16


def _cparams(sem, vmem=VMEM_LIMIT):
    return pltpu.CompilerParams(dimension_semantics=sem, vmem_limit_bytes=vmem)


def _ln(x):
    mu = jnp.mean(x, axis=-1, keepdims=True)
    xc = x - mu
    var = jnp.mean(xc * xc, axis=-1, keepdims=True)
    return xc * lax.rsqrt(var + LN_EPS)


def _silu(x):
    return x * jax.nn.sigmoid(x)


def _split3(a):
    p0 = a.astype(BF16)
    r1 = a - p0.astype(F32)
    p1 = r1.astype(BF16)
    r2 = r1 - p1.astype(F32)
    return p0, p1, r2.astype(BF16)


def _split2(a):
    hi = a.astype(BF16)
    return hi, (a - hi.astype(F32)).astype(BF16)


def _dot(a, b):
    return jnp.dot(a, b, preferred_element_type=F32)


def _dot_nt(a, b):
    return lax.dot_general(a, b, (((1,), (1,)), ((), ())), preferred_element_type=F32)


def _dot_tn(a, b):
    return lax.dot_general(a, b, (((0,), (0,)), ((), ())), preferred_element_type=F32)


def _cond_of_tile(i, tm):
    n_ctx = TP // tm
    return jnp.where(i < n_ctx, 0, 1 + (i - n_ctx) // (DEC_SEQ // tm))


def _mod_spec(l, tm, tile_of=lambda i, *_: i):
    return pl.BlockSpec((None, None, N_MOD, D_MODEL), lambda *a: (l, _cond_of_tile(tile_of(*a), tm), 0, 0))


def _layer_row_spec(l, width):
    return pl.BlockSpec((None, 1, width), lambda *_: (l, 0, 0))


def _mod_kernel(c_ref, w_ref, b_ref, o_ref):
    a = _silu(c_ref[...]).astype(BF16)
    o_ref[...] = _dot(a, w_ref[...].astype(BF16)) + b_ref[...]


def _modulation(cond, w_mod, b_mod):
    tn = 2048
    n_out = N_MOD * D_MODEL
    return pl.pallas_call(
        _mod_kernel,
        grid=(DEPTH, n_out // tn),
        in_specs=[
            pl.BlockSpec((COND_PAD, D_MODEL), lambda l, j: (0, 0)),
            pl.BlockSpec((None, D_MODEL, tn), lambda l, j: (l, 0, j)),
            pl.BlockSpec((None, 1, tn), lambda l, j: (l, 0, j)),
        ],
        out_specs=pl.BlockSpec((None, COND_PAD, tn), lambda l, j: (l, 0, j)),
        out_shape=jax.ShapeDtypeStruct((DEPTH, COND_PAD, n_out), F32),
        compiler_params=_cparams(("parallel", "parallel")),
        name="modulation",
    )(cond, w_mod, b_mod.reshape(DEPTH, 1, n_out))


def _inproj_kernel(*refs, split):
    if split:
        x0_ref, x1_ref, mod_ref, w_ref, o_ref, wb_ref = refs
        is_ctx = pl.program_id(0) < TP // TM
        load = lambda rows: jnp.where(is_ctx, x0_ref[rows, :], x1_ref[rows, :])
    else:
        x_ref, mod_ref, w_ref, o_ref, wb_ref = refs
        load = lambda rows: x_ref[rows, :]

    @pl.when(pl.program_id(0) == 0)
    def _():
        wb_ref[...] = w_ref[...].astype(BF16)

    for r0 in range(0, TM, ROW_CHUNK):
        rows = slice(r0, r0 + ROW_CHUNK)
        h = _ln(load(rows)) * (1.0 + mod_ref[1:2, :]) + mod_ref[0:1, :]
        o_ref[rows, :] = _dot(h.astype(BF16), wb_ref[...])


def _in_projection(x, mod, w_in, l):
    split = isinstance(x, tuple)
    n_ctx = TP // TM
    if split:
        x_specs = [pl.BlockSpec((TM, D_MODEL), lambda i: (jnp.minimum(i, n_ctx - 1), 0)),
                   pl.BlockSpec((TM, D_MODEL), lambda i: (jnp.maximum(i - n_ctx, 0), 0))]
    else:
        x_specs = [pl.BlockSpec((TM, D_MODEL), lambda i: (i, 0))]
        x = (x,)
    return pl.pallas_call(
        functools.partial(_inproj_kernel, split=split),
        grid=(T // TM,),
        in_specs=x_specs + [
            _mod_spec(l, TM),
            pl.BlockSpec((None, D_MODEL, D_IN), lambda i: (l, 0, 0), pipeline_mode=pl.Buffered(1)),
        ],
        out_specs=pl.BlockSpec((TM, D_IN), lambda i: (i, 0)),
        out_shape=jax.ShapeDtypeStruct((T, D_IN), F32),
        scratch_shapes=[pltpu.VMEM((D_MODEL, D_IN), BF16)],
        compiler_params=_cparams(("arbitrary",)),
        name="in_projection",
    )(*x, mod, w_in)


def _fourier_kernel(u_ref, ch_hi_ref, ch_lo_ref, tab_hi_ref, tab_lo_ref, wf_ref, o_ref,
                    ab_hi_ref, ab_lo_ref, *, scale):
    @pl.when(pl.program_id(1) == 0)
    def _():
        uh, ul = _split2(u_ref[...])
        ch_hi = ch_hi_ref[...]
        ab = _dot(uh, ch_hi) + _dot(uh, ch_lo_ref[...]) + _dot(ul, ch_hi)
        stacked = jnp.concatenate([ab[:, :D_FOURIER], ab[:, D_FOURIER:]], axis=0)
        hi, lo = _split2(stacked)
        ab_hi_ref[...] = hi
        ab_lo_ref[...] = lo

    th = tab_hi_ref[...]
    ab_hi = ab_hi_ref[...]
    z = _dot(th, ab_hi) + _dot(th, ab_lo_ref[...]) + _dot(tab_lo_ref[...], ab_hi)
    z = (z * scale).astype(BF16)
    o_ref[...] = _dot(z, wf_ref[...].astype(BF16)).astype(BF16)


def _dft_tables(n):
    blk = 32
    t = jnp.arange(n, dtype=jnp.int32)[None, :]
    ang1 = (((jnp.arange(n // blk, dtype=jnp.int32) * blk)[:, None] * t) % n).astype(F32) * (2.0 * math.pi / n)
    ang0 = ((jnp.arange(blk, dtype=jnp.int32)[:, None] * t) % n).astype(F32) * (2.0 * math.pi / n)
    c1, s1 = jnp.cos(ang1)[:, None, :], jnp.sin(ang1)[:, None, :]
    c0, s0 = jnp.cos(ang0)[None, :, :], jnp.sin(ang0)[None, :, :]
    cos = (c1 * c0 - s1 * s0).reshape(n, n)
    sin = (s1 * c0 + c1 * s0).reshape(n, n)
    return _split2(jnp.concatenate([cos, -sin], axis=1))


def _channel_tables():
    a = jnp.arange(D_FOURIER, dtype=jnp.int32)
    same = (a[:, None] // FOURIER_GROUP) == (a[None, :] // FOURIER_GROUP)
    prod = ((a[:, None] % FOURIER_GROUP) * (a[None, :] % FOURIER_GROUP)) % FOURIER_GROUP
    ang = prod.astype(F32) * (2.0 * math.pi / FOURIER_GROUP)
    c = jnp.where(same, jnp.cos(ang), 0.0)
    s = jnp.where(same, jnp.sin(ang), 0.0)
    return _split2(jnp.concatenate([c, s], axis=1))


def _fourier_mix(y_in, row0, nbatch, length, ch_tabs, pos_tabs, w_fourier, l):
    tr = min(length, 256)
    blk0 = row0 // length
    kern = functools.partial(_fourier_kernel, scale=1.0 / math.sqrt(length * FOURIER_GROUP))
    return pl.pallas_call(
        kern,
        grid=(nbatch, length // tr),
        in_specs=[
            pl.BlockSpec((length, D_FOURIER), lambda b, r: (blk0 + b, 0)),
            pl.BlockSpec((D_FOURIER, 2 * D_FOURIER), lambda b, r: (0, 0)),
            pl.BlockSpec((D_FOURIER, 2 * D_FOURIER), lambda b, r: (0, 0)),
            pl.BlockSpec((tr, 2 * length), lambda b, r: (r, 0)),
            pl.BlockSpec((tr, 2 * length), lambda b, r: (r, 0)),
            pl.BlockSpec((None, D_FOURIER, D_FOURIER), lambda b, r: (l, 0, 0)),
        ],
        out_specs=pl.BlockSpec((tr, D_FOURIER), lambda b, r: (b * (length // tr) + r, 0)),
        out_shape=jax.ShapeDtypeStruct((nbatch * length, D_FOURIER), BF16),
        scratch_shapes=[pltpu.VMEM((2 * length, D_FOURIER), BF16),
                        pltpu.VMEM((2 * length, D_FOURIER), BF16)],
        compiler_params=_cparams(("parallel", "arbitrary")),
        name=f"fourier_mix_{length}",
    )(y_in, ch_tabs[0], ch_tabs[1], pos_tabs[0], pos_tabs[1], w_fourier)


def _expand_state(cst):
    full = jnp.concatenate([cst] * N_HGRN_HEADS, axis=0)
    r = lax.broadcasted_iota(jnp.int32, full.shape, 0) // HGRN_HEAD
    c = lax.broadcasted_iota(jnp.int32, full.shape, 1) // HGRN_HEAD
    return jnp.where(r == c, full, jnp.zeros_like(full))


def _compress_state(full):
    lane_head = lax.broadcasted_iota(jnp.int32, (HGRN_HEAD, D_HGRN), 1) // HGRN_HEAD
    out = jnp.zeros((HGRN_HEAD, D_HGRN), full.dtype)
    for h in range(N_HGRN_HEADS):
        out = out + jnp.where(lane_head == h, full[h * HGRN_HEAD:(h + 1) * HGRN_HEAD, :], 0.0)
    return out


def _hgrn_pairwise_scan(d, nc, g_scr, k_scr, q_scr, v_ref, o_scr, st_scr, ones_bd):
    c = HGRN_CHUNK
    last = c - 1 if d == 0 else 0

    def chunk(ci, carry):
        row = lax.broadcasted_iota(jnp.int32, (c, c), 0)
        col = lax.broadcasted_iota(jnp.int32, (c, c), 1)
        tri = jnp.where((col <= row) if d == 0 else (col >= row), 1.0, 0.0).astype(BF16)
        cc = ci if d == 0 else nc - 1 - ci
        rows = pl.ds(pl.multiple_of(cc * c, c), c)
        g = g_scr[rows, :]
        kk = k_scr[rows, :]
        q = q_scr[rows, :]
        v = v_ref[rows, :]
        g0, g1, g2 = _split3(g)
        b = _dot(tri, g0) + _dot(tri, g1) + _dot(tri, g2)
        btot = b[last:last + 1, :]
        st = st_scr[...]
        o_inter = _dot_nt((q * jnp.exp(b)).astype(BF16), _expand_state(st.astype(BF16)))
        s_idx = lax.broadcasted_iota(jnp.int32, (c, c, D_HGRN), 0)
        t_idx = lax.broadcasted_iota(jnp.int32, (c, c, D_HGRN), 1)
        causal = (s_idx <= t_idx) if d == 0 else (s_idx >= t_idx)
        diff = b[None, :, :] - b[:, None, :]
        pair = q[None, :, :] * jnp.exp(jnp.minimum(diff, 0.0)) * kk[:, None, :]
        pair = jnp.where(causal, pair, 0.0).astype(BF16)
        attn = _dot(pair.reshape(c * c, D_HGRN), ones_bd).reshape(c, c, D_HGRN)
        o = o_inter + jnp.sum(attn * v[:, None, :], axis=0)
        if d == 0:
            o_scr[rows, :] = o
        else:
            o_scr[rows, :] = o_scr[rows, :] + o
        kd = kk * jnp.exp(btot - b)
        upd = _dot_tn(v.astype(BF16), kd.astype(BF16))
        st_scr[...] = jnp.exp(btot) * st + _compress_state(upd)
        return carry

    lax.fori_loop(0, nc, chunk, 0)


def _hgrn_factored_scan(d, nc, g_scr, k_scr, q_scr, v_ref, o_scr, st_scr, qd_scr, kd_scr, dec_scr, stb_scr):
    c = HGRN_CHUNK
    grp = HGRN_GROUP
    length = nc * c

    def group(gi, carry):
        rows = pl.ds(pl.multiple_of(gi * grp, grp), grp)
        r = lax.broadcasted_iota(jnp.int32, (grp, grp), 0)
        s = lax.broadcasted_iota(jnp.int32, (grp, grp), 1)
        same_chunk = (r // c) == (s // c)
        ordered = (s <= r) if d == 0 else (s >= r)
        tri = jnp.where(same_chunk & ordered, 1.0, 0.0).astype(BF16)
        ones_chunk = jnp.where(same_chunk, 1.0, 0.0).astype(BF16)
        g0, g1, g2 = _split3(g_scr[rows, :])
        b = _dot(tri, g0) + _dot(tri, g1) + _dot(tri, g2)
        tot = _dot(ones_chunk, g0) + _dot(ones_chunk, g1) + _dot(ones_chunk, g2)
        kk = k_scr[rows, :]
        q = q_scr[rows, :]
        half = 0.5 * tot
        qc = (q * jnp.exp(b - half)).astype(BF16)
        kh = (kk * jnp.exp(half - b)).astype(BF16)
        qd_scr[rows, :] = (q * jnp.exp(b)).astype(BF16)
        kd_scr[rows, :] = (kk * jnp.exp(tot - b)).astype(BF16)
        dec_scr[rows, :] = jnp.exp(tot)
        hs_row = lax.broadcasted_iota(jnp.int32, (N_HGRN_HEADS * grp, D_HGRN), 0) // grp
        hs_col = lax.broadcasted_iota(jnp.int32, (N_HGRN_HEADS * grp, D_HGRN), 1) // HGRN_HEAD
        same_head = hs_row == hs_col
        k_bd = jnp.where(same_head, jnp.concatenate([kh] * N_HGRN_HEADS, axis=0), 0.0)
        v_bd = jnp.where(same_head, jnp.concatenate([v_ref[rows, :].astype(BF16)] * N_HGRN_HEADS, axis=0), 0.0)
        t_idx = lax.broadcasted_iota(jnp.int32, (grp, N_HGRN_HEADS * grp), 0)
        s_idx = lax.broadcasted_iota(jnp.int32, (grp, N_HGRN_HEADS * grp), 1) % grp
        keep = ((t_idx // c) == (s_idx // c)) & ((s_idx <= t_idx) if d == 0 else (s_idx >= t_idx))
        attn = jnp.where(keep, _dot_nt(qc, k_bd), 0.0)
        o_intra = _dot(attn.astype(BF16), v_bd)
        if d == 0:
            o_scr[rows, :] = o_intra
        else:
            o_scr[rows, :] = o_scr[rows, :] + o_intra
        return carry

    lax.fori_loop(0, length // grp, group, 0, unroll=min(4, length // grp))

    def state_step(ci, carry):
        cc = ci if d == 0 else nc - 1 - ci
        r0 = pl.multiple_of(cc * c, c)
        rows = pl.ds(r0, c)
        st = st_scr[...]
        stb_scr[cc] = st.astype(BF16)
        upd = _dot_tn(v_ref[rows, :].astype(BF16), kd_scr[rows, :])
        st_scr[...] = dec_scr[pl.ds(r0, 1), :] * st + _compress_state(upd)
        return carry

    lax.fori_loop(0, nc, state_step, 0, unroll=4)

    nb = HGRN_STATE_BLOCK
    blk_rows = nb * c

    def inter(bi, carry):
        rows = pl.ds(pl.multiple_of(bi * blk_rows, blk_rows), blk_rows)
        row_chunk = lax.broadcasted_iota(jnp.int32, (blk_rows, nb * D_HGRN), 0) // c
        col_chunk = lax.broadcasted_iota(jnp.int32, (blk_rows, nb * D_HGRN), 1) // D_HGRN
        q_bd = jnp.where(row_chunk == col_chunk, jnp.concatenate([qd_scr[rows, :]] * nb, axis=1), 0.0)
        states = jnp.concatenate([_expand_state(stb_scr[bi * nb + j]) for j in range(nb)], axis=1)
        o_scr[rows, :] = o_scr[rows, :] + _dot_nt(q_bd, states)
        return carry

    lax.fori_loop(0, nc // nb, inter, 0)


def _hgrn_kernel(hq_ref, hff_ref, hfb_ref, hi_ref, hg_ref, lb_ref, gw_ref, s0_ref,
                 o_ref, sfin_ref, q_scr, g_scr, k_scr, o_scr, dec_scr, qd_scr, kd_scr, st_scr, stb_scr,
                 *, length):
    c = HGRN_CHUNK
    nc = length // c
    q_scr[...] = _silu(hq_ref[...])
    r256 = lax.broadcasted_iota(jnp.int32, (D_HGRN, D_HGRN), 0) // HGRN_HEAD
    c256 = lax.broadcasted_iota(jnp.int32, (D_HGRN, D_HGRN), 1) // HGRN_HEAD
    ones_bd = jnp.where(r256 == c256, 1.0, 0.0).astype(BF16)

    for d in range(2):
        z_ref = hff_ref if d == 0 else hfb_ref
        one_minus_f = (1.0 - lb_ref[d:d + 1, :]) * jax.nn.sigmoid(-z_ref[...])
        g_scr[...] = jnp.log1p(-one_minus_f)
        k_scr[...] = one_minus_f
        st_scr[...] = s0_ref[d]
        chunk_decay = jnp.sum(g_scr[...].reshape(nc, c, D_HGRN), axis=1)
        safe = jnp.min(chunk_decay) >= -HGRN_SAFE_DECAY
        lax.cond(
            safe,
            functools.partial(_hgrn_factored_scan, d, nc, g_scr, k_scr, q_scr, hi_ref, o_scr, st_scr,
                              qd_scr, kd_scr, dec_scr, stb_scr),
            functools.partial(_hgrn_pairwise_scan, d, nc, g_scr, k_scr, q_scr, hi_ref, o_scr, st_scr, ones_bd))
        sfin_ref[d] = st_scr[...]

    o = o_scr[...]
    s0p, s1p, s2p = _split3(o * o)
    ms = (_dot(s0p, ones_bd) + _dot(s1p, ones_bd) + _dot(s2p, ones_bd)) * (1.0 / HGRN_HEAD)
    y = o * lax.rsqrt(ms + RMS_EPS) * gw_ref[...] * _silu(hg_ref[...])
    o_ref[...] = y.astype(BF16)


def _hgrn_mix(y_in, row0, nbatch, length, lb, gw, s0, l, s0_layer):
    blk0 = row0 // length
    col = lambda j: (lambda b: (blk0 + b, j))
    kern = functools.partial(_hgrn_kernel, length=length)
    return pl.pallas_call(
        kern,
        grid=(nbatch,),
        in_specs=[
            pl.BlockSpec((length, D_HGRN), col(1)),
            pl.BlockSpec((length, D_HGRN), col(2)),
            pl.BlockSpec((length, D_HGRN), col(3)),
            pl.BlockSpec((length, D_HGRN), col(4)),
            pl.BlockSpec((length, D_HGRN), col(5)),
            pl.BlockSpec((None, 2, D_HGRN), lambda b: (l, 0, 0)),
            _layer_row_spec(l, D_HGRN),
            pl.BlockSpec((None, None, 2, HGRN_HEAD, D_HGRN), lambda b: (b, s0_layer, 0, 0, 0)),
        ],
        out_specs=[
            pl.BlockSpec((length, D_HGRN), lambda b: (b, 0)),
            pl.BlockSpec((None, 2, HGRN_HEAD, D_HGRN), lambda b: (b, 0, 0, 0)),
        ],
        out_shape=[
            jax.ShapeDtypeStruct((nbatch * length, D_HGRN), BF16),
            jax.ShapeDtypeStruct((nbatch, 2, HGRN_HEAD, D_HGRN), F32),
        ],
        scratch_shapes=[pltpu.VMEM((length, D_HGRN), F32) for _ in range(5)]
        + [pltpu.VMEM((length, D_HGRN), BF16) for _ in range(2)]
        + [pltpu.VMEM((HGRN_HEAD, D_HGRN), F32), pltpu.VMEM((length // HGRN_CHUNK, HGRN_HEAD, D_HGRN), BF16)],
        compiler_params=_cparams(("parallel",)),
        name=f"hgrn_mix_{length}",
    )(y_in, y_in, y_in, y_in, y_in, lb, gw, s0)


def _states_to_kernel_layout(s):
    return jnp.moveaxis(s, -1, -3).reshape(s.shape[:-3] + (HGRN_HEAD, D_HGRN))


def _states_from_kernel_layout(st):
    return jnp.moveaxis(st.reshape(st.shape[:-1] + (N_HGRN_HEADS, HGRN_HEAD)), -3, -1)


def _group_attention(q_heads, sinks, kv_parts):
    rows = q_heads[0].shape[0]
    qg = jnp.concatenate([(q * ATTN_SCALE).astype(BF16) for q in q_heads], axis=0)
    head = lax.broadcasted_iota(jnp.int32, (len(q_heads) * rows, 1), 0) // rows
    sink = jnp.full(head.shape, sinks[0], F32)
    for i in range(1, len(q_heads)):
        sink = jnp.where(head == i, sinks[i], sink)
    m = sink
    scores = []
    for k, _, bias in kv_parts:
        s = _dot_nt(qg, k)
        if bias is not None:
            s = s + bias
        m = jnp.maximum(m, jnp.max(s, axis=1, keepdims=True))
        scores.append(s)
    den = jnp.exp(sink - m)
    out = None
    for s, (_, v, _) in zip(scores, kv_parts):
        e = jnp.exp(s - m)
        den = den + jnp.sum(e, axis=1, keepdims=True)
        o = _dot(e.astype(BF16), v)
        out = o if out is None else out + o
    out = out * (1.0 / den)
    return [out[i * rows:(i + 1) * rows, :] for i in range(len(q_heads))]


def _ctx_attn_kernel(sink_ref, q_ref, k_ref, v_ref, o_ref, *, layer):
    outs = []
    for g in range(N_KV_HEADS):
        sl = slice(g * HEAD_DIM, (g + 1) * HEAD_DIM)
        heads = range(g * GQA, (g + 1) * GQA)
        outs += _group_attention(
            [q_ref[:, h * HEAD_DIM:(h + 1) * HEAD_DIM] for h in heads],
            [sink_ref[layer, h] for h in heads],
            [(k_ref[:, sl].astype(BF16), v_ref[:, sl].astype(BF16), None)])
    o_ref[...] = jnp.concatenate(outs, axis=1).astype(BF16)


def _context_attention(y_in, sink, l):
    qcol = (6 * 256) // D_ATTN
    kcol = (6 * 256 + D_ATTN) // D_KV
    return pl.pallas_call(
        functools.partial(_ctx_attn_kernel, layer=l),
        grid=(BATCH,),
        in_specs=[
            pl.BlockSpec(memory_space=pltpu.SMEM),
            pl.BlockSpec((SEQ, D_ATTN), lambda b: (b, qcol)),
            pl.BlockSpec((SEQ, D_KV), lambda b: (b, kcol)),
            pl.BlockSpec((SEQ, D_KV), lambda b: (b, kcol + 1)),
        ],
        out_specs=pl.BlockSpec((SEQ, D_ATTN), lambda b: (b, 0)),
        out_shape=jax.ShapeDtypeStruct((TP, D_ATTN), BF16),
        compiler_params=_cparams(("parallel",)),
        name="context_attention",
    )(sink, y_in, y_in, y_in)


def _rope(x, cos, sin):
    lane = lax.broadcasted_iota(jnp.int32, x.shape, 1)
    n_freq = HEAD_DIM // 4
    first = (lane % (2 * n_freq)) < n_freq
    swapped = jnp.where(first, pltpu.roll(x, LANES - n_freq, axis=1), pltpu.roll(x, n_freq, axis=1))
    return x * cos + swapped * sin


def _lat_attn_kernel(sink_ref, q_ref, k_ref, v_ref, kc_ref, vc_ref, cos_ref, sin_ref, bias_ref, o_ref, *, layer):
    j = pl.program_id(1)
    nb = DEC_SEQ // ATTN_BLOCK
    blk = ATTN_BLOCK
    q0 = pl.multiple_of(j * blk, blk)
    cos_q = cos_ref[pl.ds(q0, blk), :]
    sin_q = sin_ref[pl.ds(q0, blk), :]

    starts = [jnp.maximum(j - 1, 0), j, jnp.minimum(j + 1, nb - 1)]
    k_band, v_band = [], []
    for st in starts:
        r0 = pl.multiple_of(st * blk, blk)
        kb = _rope(k_ref[pl.ds(r0, blk), :], cos_ref[pl.ds(r0, blk), :], sin_ref[pl.ds(r0, blk), :])
        k_band.append(kb)
        v_band.append(v_ref[pl.ds(r0, blk), :])
    k_loc = jnp.concatenate(k_band, axis=0)
    v_loc = jnp.concatenate(v_band, axis=0)
    band_bias = bias_ref[...]

    heads_per_chunk = LANES // HEAD_DIM
    q_chunks = [_rope(q_ref[:, cg * LANES:(cg + 1) * LANES], cos_q, sin_q)
                for cg in range(N_Q_HEADS // heads_per_chunk)]

    def q_head(h):
        off = (h % heads_per_chunk) * HEAD_DIM
        return q_chunks[h // heads_per_chunk][:, off:off + HEAD_DIM]

    outs = []
    for g in range(N_KV_HEADS):
        sl = slice(g * HEAD_DIM, (g + 1) * HEAD_DIM)
        heads = range(g * GQA, (g + 1) * GQA)
        outs += _group_attention(
            [q_head(h) for h in heads],
            [sink_ref[layer, h] for h in heads],
            [(k_loc[:, sl].astype(BF16), v_loc[:, sl].astype(BF16), band_bias),
             (kc_ref[:, sl].astype(BF16), vc_ref[:, sl].astype(BF16), None)])
    o_ref[...] = jnp.concatenate(outs, axis=1).astype(BF16)


def _band_bias():
    blk = ATTN_BLOCK
    r = np.arange(GQA * blk)[:, None] % blk
    c = np.arange(3 * blk)[None, :]
    in_window = np.abs(r + blk - c) <= WINDOW
    exists = [c >= blk, c >= 0, c < 2 * blk]
    return jnp.asarray(np.stack([np.where(in_window & e, 0.0, NEG_BIG) for e in exists]).astype(np.float32))


def _latent_attention(y_in, sink, kc, vc, l, cos_t, sin_t, band_bias):
    nb = DEC_SEQ // ATTN_BLOCK
    qrow0 = TP // ATTN_BLOCK
    krow0 = TP // DEC_SEQ
    qcol = (6 * 256) // D_ATTN
    kcol = (6 * 256 + D_ATTN) // D_KV
    return pl.pallas_call(
        functools.partial(_lat_attn_kernel, layer=l),
        grid=(DEC_BATCH, nb),
        in_specs=[
            pl.BlockSpec(memory_space=pltpu.SMEM),
            pl.BlockSpec((ATTN_BLOCK, D_ATTN), lambda b, j: (qrow0 + b * nb + j, qcol)),
            pl.BlockSpec((DEC_SEQ, D_KV), lambda b, j: (krow0 + b, kcol)),
            pl.BlockSpec((DEC_SEQ, D_KV), lambda b, j: (krow0 + b, kcol + 1)),
            pl.BlockSpec((None, None, PAST_LEN, D_KV), lambda b, j: (b, l, 0, 0)),
            pl.BlockSpec((None, None, PAST_LEN, D_KV), lambda b, j: (b, l, 0, 0)),
            pl.BlockSpec((DEC_SEQ, LANES), lambda b, j: (0, 0)),
            pl.BlockSpec((DEC_SEQ, LANES), lambda b, j: (0, 0)),
            pl.BlockSpec((None, GQA * ATTN_BLOCK, 3 * ATTN_BLOCK),
                         lambda b, j: (jnp.where(j == 0, 0, jnp.where(j == nb - 1, 2, 1)), 0, 0)),
        ],
        out_specs=pl.BlockSpec((ATTN_BLOCK, D_ATTN), lambda b, j: (b * nb + j, 0)),
        out_shape=jax.ShapeDtypeStruct((TS, D_ATTN), BF16),
        compiler_params=_cparams(("parallel", "parallel")),
        name="latent_attention",
    )(sink, y_in, y_in, y_in, kc, vc, cos_t, sin_t, band_bias)


def _rope_tables():
    t = jnp.arange(DEC_SEQ)
    rows = (t // GRID_W).astype(F32)
    cols = (t % GRID_W).astype(F32)
    n_freq = HEAD_DIM // 4
    inv = ROPE_BASE ** (-jnp.arange(n_freq, dtype=F32) / n_freq)
    ar = rows[:, None] * inv
    ac = cols[:, None] * inv
    cos = jnp.concatenate([jnp.cos(ar), jnp.cos(ar), jnp.cos(ac), jnp.cos(ac)], axis=1)
    sin = jnp.concatenate([-jnp.sin(ar), jnp.sin(ar), -jnp.sin(ac), jnp.sin(ac)], axis=1)
    return jnp.tile(cos, (1, N_KV_HEADS)), jnp.tile(sin, (1, N_KV_HEADS))


def _top2_route(h2, w_ref, b_ref):
    logits = _dot(h2.astype(BF16), w_ref[...].astype(BF16)) + b_ref[...]
    lane = lax.broadcasted_iota(jnp.int32, logits.shape, 1)
    m1 = jnp.max(logits, axis=1, keepdims=True)
    i1 = jnp.min(jnp.where(logits == m1, lane, LANES), axis=1, keepdims=True)
    rest = jnp.where(lane == i1, -jnp.inf, logits)
    m2 = jnp.max(rest, axis=1, keepdims=True)
    i2 = jnp.min(jnp.where(rest == m2, lane, LANES), axis=1, keepdims=True)
    e = jnp.exp(m2 - m1)
    inv = 1.0 / (1.0 + e)
    idx = jnp.where(lane == 0, i1, jnp.where(lane == 1, i2, 0))
    return idx, jnp.where(lane == 0, inv, jnp.where(lane == 1, e * inv, 0.0))


def _outproj_kernel(a0_ref, a1_ref, b0_ref, b1_ref, c0_ref, c1_ref, *rest, route, split):
    is_ctx = pl.program_id(0) < TP // TM
    pick = lambda r0, r1, rows: jnp.where(is_ctx, r0[rows, :], r1[rows, :])
    if split:
        load_x = functools.partial(pick, rest[0], rest[1])
        rest = rest[2:]
    else:
        load_x = lambda rows, x_ref=rest[0]: x_ref[rows, :]
        rest = rest[1:]
    mod_ref, g_ref, beta_ref, w_ref = rest[:4]
    rest = rest[4:]
    if route:
        wr_ref, br_ref, x1_ref, h2_ref, idx_ref, p_ref, wb_ref = rest
    else:
        x1_ref, h2_ref, wb_ref = rest

    @pl.when(pl.program_id(0) == 0)
    def _():
        wb_ref[...] = w_ref[...].astype(BF16)

    for r0 in range(0, TM, ROW_CHUNK):
        rows = slice(r0, r0 + ROW_CHUNK)
        y = (_dot(pick(a0_ref, a1_ref, rows), wb_ref[0:D_FOURIER, :])
             + _dot(pick(b0_ref, b1_ref, rows), wb_ref[D_FOURIER:D_FOURIER + D_HGRN, :])
             + _dot(pick(c0_ref, c1_ref, rows), wb_ref[D_FOURIER + D_HGRN:D_MIX, :]))
        x1 = _ln(DEEPNORM_ALPHA * load_x(rows) + mod_ref[2:3, :] * y) * g_ref[...] + beta_ref[...]
        x1_ref[rows, :] = x1
        h2 = _ln(x1) * (1.0 + mod_ref[4:5, :]) + mod_ref[3:4, :]
        h2_ref[rows, :] = h2
        if route:
            idx_ref[rows, :], p_ref[rows, :] = _top2_route(h2, wr_ref, br_ref)


def _out_projection(mixed, x, mod, ln_g, ln_b, w_out, l, router=None):
    n_ctx = TP // TM
    row = lambda w: pl.BlockSpec((TM, w), lambda i: (i, 0))
    ctx = lambda w: pl.BlockSpec((TM, w), lambda i: (jnp.minimum(i, n_ctx - 1), 0))
    lat = lambda w: pl.BlockSpec((TM, w), lambda i: (jnp.maximum(i - n_ctx, 0), 0))
    vec = _layer_row_spec(l, D_MODEL)
    split = isinstance(x, tuple)
    x_specs = [ctx(D_MODEL), lat(D_MODEL)] if split else [row(D_MODEL)]
    x = x if split else (x,)
    in_specs = [
        ctx(D_FOURIER), lat(D_FOURIER), ctx(D_HGRN), lat(D_HGRN), ctx(D_ATTN), lat(D_ATTN), *x_specs,
        _mod_spec(l, TM),
        vec, vec,
        pl.BlockSpec((None, D_MIX, D_MODEL), lambda i: (l, 0, 0), pipeline_mode=pl.Buffered(1)),
    ]
    args = [mixed[0][0], mixed[0][1], mixed[1][0], mixed[1][1], mixed[2][0], mixed[2][1], *x, mod, ln_g, ln_b,
            w_out]
    out_specs = [row(D_MODEL), row(D_MODEL)]
    out_shape = [jax.ShapeDtypeStruct((T, D_MODEL), F32), jax.ShapeDtypeStruct((T, D_MODEL), F32)]
    if router is not None:
        w, b, ri = router
        in_specs += [pl.BlockSpec((None, D_MODEL, LANES), lambda i: (ri, 0, 0)), _layer_row_spec(ri, LANES)]
        args += [w, b]
        out_specs += [row(LANES), row(LANES)]
        out_shape += [jax.ShapeDtypeStruct((T, LANES), jnp.int32), jax.ShapeDtypeStruct((T, LANES), F32)]
    return pl.pallas_call(
        functools.partial(_outproj_kernel, route=router is not None, split=split),
        grid=(T // TM,),
        in_specs=in_specs,
        out_specs=out_specs,
        out_shape=out_shape,
        scratch_shapes=[pltpu.VMEM((D_MIX, D_MODEL), BF16)],
        compiler_params=_cparams(("arbitrary",)),
        name="out_projection_route" if router is not None else "out_projection",
    )(*args)


def _expert_changed(te_ref, i):
    return (i == 0) | (te_ref[i] != te_ref[jnp.maximum(i - 1, 0)])


def _post_ffn(x1, y, g2, ln_g, ln_b):
    return _ln(DEEPNORM_ALPHA * x1 + g2 * y) * ln_g + ln_b


def _for_used_rows(rows, out_ref, compute):
    tm = out_ref.shape[0]
    for n in range(FFN_ROW_STEP, tm + 1, FFN_ROW_STEP):
        @pl.when((rows > n - FFN_ROW_STEP) & (rows <= n))
        def _(n=n):
            compute(n)
            if n < tm:
                out_ref[n:, :] = jnp.zeros((tm - n, out_ref.shape[1]), out_ref.dtype)

    @pl.when(rows == 0)
    def _():
        out_ref[...] = jnp.zeros_like(out_ref)


def _ffn_up_kernel(te_ref, nv_ref, tr_ref, x_ref, wg_ref, wu_ref, h_ref, wb_ref):
    i = pl.program_id(1)
    fh = wg_ref.shape[-1]

    @pl.when(_expert_changed(te_ref, i))
    def _():
        wb_ref[:, :fh] = wg_ref[...].astype(BF16)
        wb_ref[:, fh:] = wu_ref[...].astype(BF16)

    def compute(n):
        ab = _dot(x_ref[0:n, :].astype(BF16), wb_ref[...])
        h_ref[0:n, :] = (_silu(ab[:, :fh]) * ab[:, fh:]).astype(BF16)

    _for_used_rows(tr_ref[i], h_ref, compute)


def _ffn_down_kernel(te_ref, nv_ref, tr_ref, h_ref, wd_ref, *rest, norm):
    if norm:
        x1_ref, mod_ref, g_ref, beta_ref, y_ref, wdb_ref = rest
    else:
        y_ref, wdb_ref = rest
    i = pl.program_id(1)

    @pl.when(_expert_changed(te_ref, i))
    def _():
        wdb_ref[...] = wd_ref[...].astype(BF16)

    if norm:
        for r0 in range(0, y_ref.shape[0], ROW_CHUNK):
            rows = slice(r0, r0 + ROW_CHUNK)
            y = _dot(h_ref[rows, :], wdb_ref[...])
            y_ref[rows, :] = _post_ffn(x1_ref[rows, :], y, mod_ref[5:6, :], g_ref[...], beta_ref[...])
    else:
        def compute(n):
            y_ref[0:n, :] = _dot(h_ref[0:n, :], wdb_ref[...])

        _for_used_rows(tr_ref[i], y_ref, compute)


def _grouped_ffn(x_rows, tile_expert, n_valid, tile_rows, w_gate, w_up, w_down, f_splits, tm, norm_args=None):
    r = x_rows.shape[0]
    nt = r // tm
    f = w_gate.shape[-1]
    fh = f // f_splits
    assert fh * f_splits == f and fh % LANES == 0
    used = lambda i, nv: jnp.minimum(i, nv[0] - 1)
    h = pl.pallas_call(
        _ffn_up_kernel,
        grid_spec=pltpu.PrefetchScalarGridSpec(
            num_scalar_prefetch=3,
            grid=(f_splits, nt),
            in_specs=[
                pl.BlockSpec((tm, D_MODEL), lambda j, i, te, nv, tr: (used(i, nv), 0)),
                pl.BlockSpec((None, D_MODEL, fh), lambda j, i, te, nv, tr: (te[i], 0, j)),
                pl.BlockSpec((None, D_MODEL, fh), lambda j, i, te, nv, tr: (te[i], 0, j)),
            ],
            out_specs=pl.BlockSpec((tm, fh), lambda j, i, te, nv, tr: (i, j)),
            scratch_shapes=[pltpu.VMEM((D_MODEL, 2 * fh), BF16)],
        ),
        out_shape=jax.ShapeDtypeStruct((r, f), BF16),
        compiler_params=_cparams(("arbitrary", "arbitrary")),
        name="ffn_up",
    )(tile_expert, n_valid, tile_rows, x_rows, w_gate, w_up)

    d_splits = 1 if norm_args is not None else 2
    dh = D_MODEL // d_splits
    in_specs = [
        pl.BlockSpec((tm, f), lambda j, i, te, nv, tr: (used(i, nv), 0)),
        pl.BlockSpec((None, f, dh), lambda j, i, te, nv, tr: (te[i], 0, j)),
    ]
    args = [tile_expert, n_valid, tile_rows, h, w_down]
    if norm_args is not None:
        assert r == T
        x1, mod, ln_g, ln_b, l = norm_args
        vec = _layer_row_spec(l, D_MODEL)
        in_specs += [
            pl.BlockSpec((tm, D_MODEL), lambda j, i, te, nv, tr: (i, 0)),
            _mod_spec(l, tm, lambda j, i, *_: i),
            vec, vec]
        args += [x1, mod, ln_g, ln_b]
    return pl.pallas_call(
        functools.partial(_ffn_down_kernel, norm=norm_args is not None),
        grid_spec=pltpu.PrefetchScalarGridSpec(
            num_scalar_prefetch=3,
            grid=(d_splits, nt),
            in_specs=in_specs,
            out_specs=pl.BlockSpec((tm, dh), lambda j, i, te, nv, tr: (i, j)),
            scratch_shapes=[pltpu.VMEM((f, dh), BF16)],
        ),
        out_shape=jax.ShapeDtypeStruct((r, D_MODEL), F32),
        compiler_params=_cparams(("arbitrary", "arbitrary")),
        name="ffn_down_norm" if norm_args is not None else "ffn_down",
    )(*args)


def _row_copy(src, dst, s, d, sem):
    return pltpu.make_async_copy(src.at[pl.ds(s, 1), :], dst.at[pl.ds(d, 1), :], sem)


def _dispatch_kernel(dest_ref, last_ref, x_ref, o_hbm, zero_scr, sem, zsem):
    i = pl.program_id(0)

    @pl.when(i == 0)
    def _():
        zero_scr[...] = jnp.zeros_like(zero_scr)

        def fill(tile):
            r0 = pl.multiple_of(tile * TM_MOE, TM_MOE)
            return pltpu.make_async_copy(zero_scr, o_hbm.at[pl.ds(r0, TM_MOE), :], zsem)

        n_tiles = o_hbm.shape[0] // TM_MOE
        min_tiles = (T * TOP_K) // TM_MOE
        jobs = [(last_ref[e] >= 0, last_ref[e]) for e in range(N_EXPERTS)]
        jobs += [(t >= last_ref[N_EXPERTS], t) for t in range(min_tiles, n_tiles)]
        for go, tile in jobs:
            @pl.when(go)
            def _(tile=tile):
                fill(tile).start()
        for go, tile in jobs:
            @pl.when(go)
            def _(tile=tile):
                fill(tile).wait()

    base = i * TM_LN

    def issue(r, carry):
        for k in range(TOP_K):
            _row_copy(x_ref, o_hbm, r, dest_ref[(base + r) * TOP_K + k], sem).start(priority=k % 2)
        return carry

    lax.fori_loop(0, TM_LN, issue, 0, unroll=8)
    for k in range(TOP_K):
        pltpu.make_async_copy(x_ref, o_hbm.at[pl.ds(0, TM_LN), :], sem).wait()


def _dispatch(h2, dest, last_tile, n_rows):
    return pl.pallas_call(
        _dispatch_kernel,
        grid_spec=pltpu.PrefetchScalarGridSpec(
            num_scalar_prefetch=2,
            grid=(T // TM_LN,),
            in_specs=[pl.BlockSpec((TM_LN, D_MODEL), lambda i, d, l: (i, 0))],
            out_specs=pl.BlockSpec(memory_space=pl.ANY),
            scratch_shapes=[pltpu.VMEM((TM_MOE, D_MODEL), F32), pltpu.SemaphoreType.DMA(()),
                            pltpu.SemaphoreType.DMA(())],
        ),
        out_shape=jax.ShapeDtypeStruct((n_rows, D_MODEL), F32),
        compiler_params=_cparams(("arbitrary",)),
        name="dispatch",
    )(dest, last_tile, h2)


def _combine_kernel(pos_ref, x1_ref, p_ref, y_hbm, mod_ref, g_ref, beta_ref, o_ref, buf, sem):
    i = pl.program_id(0)
    n = pl.num_programs(0)

    def fetch(step, slot):
        def issue(r, carry):
            for k in range(TOP_K):
                src = pos_ref[(step * TM_LN + r) * TOP_K + k]
                _row_copy(y_hbm, buf.at[slot, k], src, r, sem.at[slot, k]).start(priority=k % 2)
            return carry

        lax.fori_loop(0, TM_LN, issue, 0, unroll=8)

    @pl.when(i == 0)
    def _():
        fetch(0, 0)

    slot = i % 2

    @pl.when(i + 1 < n)
    def _():
        fetch(i + 1, 1 - slot)

    for k in range(TOP_K):
        pltpu.make_async_copy(y_hbm.at[pl.ds(0, TM_LN), :], buf.at[slot, k], sem.at[slot, k]).wait()
    y = p_ref[:, 0:1] * buf[slot, 0]
    for k in range(1, TOP_K):
        y = y + p_ref[:, k:k + 1] * buf[slot, k]
    o_ref[...] = _post_ffn(x1_ref[...], y, mod_ref[5:6, :], g_ref[...], beta_ref[...])


def _combine_norm(x1, p, y_rows, pos, mod, ln_g, ln_b, l):
    row = pl.BlockSpec((TM_LN, D_MODEL), lambda i, s: (i, 0))
    vec = _layer_row_spec(l, D_MODEL)
    return pl.pallas_call(
        _combine_kernel,
        grid_spec=pltpu.PrefetchScalarGridSpec(
            num_scalar_prefetch=1,
            grid=(T // TM_LN,),
            in_specs=[row, pl.BlockSpec((TM_LN, LANES), lambda i, s: (i, 0)),
                      pl.BlockSpec(memory_space=pl.ANY),
                      _mod_spec(l, TM_LN),
                      vec, vec],
            out_specs=row,
            scratch_shapes=[pltpu.VMEM((2, TOP_K, TM_LN, D_MODEL), F32),
                            pltpu.SemaphoreType.DMA((2, TOP_K))],
        ),
        out_shape=jax.ShapeDtypeStruct((T, D_MODEL), F32),
        compiler_params=_cparams(("arbitrary",)),
        name="combine_norm",
    )(pos, x1, p, y_rows, mod, ln_g, ln_b)


def _moe_plan(idx):
    n_assign = T * TOP_K
    n_tiles = n_assign // TM_MOE + N_EXPERTS
    e = idx[:, :TOP_K].reshape(n_assign)
    onehot = (e[:, None] == jnp.arange(N_EXPERTS, dtype=jnp.int32)[None, :]).astype(jnp.int32)
    csum = jnp.cumsum(onehot, axis=0)
    counts = csum[-1]
    tiles_e = (counts + TM_MOE - 1) // TM_MOE
    tile_end = jnp.cumsum(tiles_e)
    row0 = (tile_end - tiles_e) * TM_MOE
    dest = jnp.sum((csum - 1 + row0[None, :]) * onehot, axis=1)
    n_valid = tile_end[-1]
    tiles = jnp.arange(n_tiles, dtype=jnp.int32)
    tile_id = jnp.minimum(tiles, n_valid - 1)
    tile_expert = jnp.sum((tile_id[:, None] >= tile_end[None, :]).astype(jnp.int32), axis=1)
    tile_start = tile_end - tiles_e
    in_group = (tiles[:, None] >= tile_start[None, :]) & (tiles[:, None] < tile_end[None, :])
    left = counts[None, :] - (tiles[:, None] - tile_start[None, :]) * TM_MOE
    tile_rows = jnp.sum(jnp.where(in_group, jnp.clip(left, 0, TM_MOE), 0), axis=1)
    last_tile = jnp.concatenate([jnp.where(tiles_e > 0, tile_end - 1, -1), n_valid.reshape(1)])
    return (dest.astype(jnp.int32), tile_expert.astype(jnp.int32), n_valid.reshape(1).astype(jnp.int32),
            tile_rows.astype(jnp.int32), last_tile.astype(jnp.int32), n_tiles * TM_MOE)


def _moe_ffn(h2, x1, idx, p, mod, ln_g, ln_b, l, w_gate, w_up, w_down, expert0):
    dest, tile_expert, n_valid, tile_rows, last_tile, n_rows = _moe_plan(idx)
    x_rows = _dispatch(h2, dest, last_tile, n_rows)
    y_rows = _grouped_ffn(x_rows, tile_expert + expert0, n_valid, tile_rows, w_gate, w_up, w_down, 4, TM_MOE)
    return _combine_norm(x1, p, y_rows, dest, mod, ln_g, ln_b, l)


def _dense_ffn(h2, x1, mod, ln_g, ln_b, l, w_gate, w_up, w_down, index):
    nt = T // TM_FFN
    return _grouped_ffn(h2, jnp.full((nt,), index, jnp.int32), jnp.full((1,), nt, jnp.int32),
                        jnp.full((nt,), TM_FFN, jnp.int32), w_gate, w_up, w_down, 2, TM_FFN,
                        norm_args=(x1, mod, ln_g, ln_b, l))


def kernel(x_prompt, x_sample, c, cache_k, cache_v, state_hgrn, c_ctx, w_mod, b_mod, w_in, w_fourier, lb_logits, hgrn_norm, attn_sink, w_out, ln1_g, ln1_b, ln2_g, ln2_b, ffn_w_gate, ffn_w_up, ffn_w_down, router_w, router_b, moe_w_gate, moe_w_up, moe_w_down):
    lb_sm = jax.nn.softmax(lb_logits.astype(F32), axis=0)
    lower_bounds = jnp.clip(jnp.cumsum(lb_sm, axis=0) - lb_sm[0], 0.0, 1.0).reshape(DEPTH, 2, D_HGRN)
    gw = jnp.tile(hgrn_norm, (1, N_HGRN_HEADS)).reshape(DEPTH, 1, D_HGRN)
    per_layer = lambda v: v.reshape(DEPTH, 1, D_MODEL)
    ln1_g, ln1_b, ln2_g, ln2_b = per_layer(ln1_g), per_layer(ln1_b), per_layer(ln2_g), per_layer(ln2_b)
    n_moe = moe_w_gate.shape[0]
    router_wp = jnp.zeros((n_moe, D_MODEL, LANES), F32).at[:, :, :N_EXPERTS].set(router_w)
    router_bp = jnp.full((n_moe, 1, LANES), NEG_BIG, F32).at[:, 0, :N_EXPERTS].set(router_b)
    s0_lat = _states_to_kernel_layout(state_hgrn)
    s0_ctx = jnp.zeros((BATCH, 1, 2, HGRN_HEAD, D_HGRN), F32)

    cond = jnp.zeros((COND_PAD, D_MODEL), F32).at[0].set(c_ctx).at[1:N_COND].set(c)
    mod = _modulation(cond, w_mod, b_mod).reshape(DEPTH, COND_PAD, N_MOD, D_MODEL)

    ch_tabs = _channel_tables()
    pos_tabs_ctx = _dft_tables(SEQ)
    pos_tabs_lat = _dft_tables(DEC_SEQ)
    cos_t, sin_t = _rope_tables()
    band_bias = _band_bias()
    kc = cache_k.reshape(DEC_BATCH, DEPTH, PAST_LEN, D_KV)
    vc = cache_v.reshape(DEC_BATCH, DEPTH, PAST_LEN, D_KV)
    moe_wg = moe_w_gate.reshape(n_moe * N_EXPERTS, D_MODEL, D_FF_EXPERT)
    moe_wu = moe_w_up.reshape(n_moe * N_EXPERTS, D_MODEL, D_FF_EXPERT)
    moe_wd = moe_w_down.reshape(n_moe * N_EXPERTS, D_FF_EXPERT, D_MODEL)

    x = (x_prompt.reshape(TP, D_MODEL), x_sample.reshape(TS, D_MODEL))
    new_kv, new_s = [], []
    for l in range(DEPTH):
        y_in = _in_projection(x, mod, w_in, l)

        a_ctx = _fourier_mix(y_in, 0, BATCH, SEQ, ch_tabs, pos_tabs_ctx, w_fourier, l)
        a_lat = _fourier_mix(y_in, TP, DEC_BATCH, DEC_SEQ, ch_tabs, pos_tabs_lat, w_fourier, l)
        b_ctx, s_ctx = _hgrn_mix(y_in, 0, BATCH, SEQ, lower_bounds, gw, s0_ctx, l, 0)
        b_lat, _ = _hgrn_mix(y_in, TP, DEC_BATCH, DEC_SEQ, lower_bounds, gw, s0_lat, l, l)
        c_ctx_out = _context_attention(y_in, attn_sink, l)
        c_lat = _latent_attention(y_in, attn_sink, kc, vc, l, cos_t, sin_t, band_bias)

        k0 = 6 * 256 + D_ATTN
        new_kv.append(y_in[:TP, k0:k0 + 2 * D_KV])
        new_s.append(s_ctx)

        mixed = ((a_ctx, a_lat), (b_ctx, b_lat), (c_ctx_out, c_lat))
        i = l // 2
        if l % 2 == 0:
            x1, h2 = _out_projection(mixed, x, mod, ln1_g, ln1_b, w_out, l)
            x = _dense_ffn(h2, x1, mod, ln2_g, ln2_b, l, ffn_w_gate, ffn_w_up, ffn_w_down, i)
        else:
            x1, h2, idx, p = _out_projection(mixed, x, mod, ln1_g, ln1_b, w_out, l,
                                             router=(router_wp, router_bp, i))
            x = _moe_ffn(h2, x1, idx, p, mod, ln2_g, ln2_b, l, moe_wg, moe_wu, moe_wd, i * N_EXPERTS)

    xp = x[:TP].reshape(BATCH, SEQ, D_MODEL)
    xs = x[TP:].reshape(DEC_BATCH, DEC_SEQ, D_MODEL)
    kv = jnp.stack(new_kv, axis=0).reshape(DEPTH, BATCH, SEQ, 2, N_KV_HEADS, HEAD_DIM)
    kv = jnp.transpose(kv, (3, 1, 0, 2, 4, 5))
    states = _states_from_kernel_layout(jnp.stack(new_s, axis=1))
    return (xp, xs, kv[0], kv[1], states)
```

```python
import functools
import math

import jax
import jax.numpy as jnp
import numpy as np
from jax import lax
from jax.experimental import pallas as pl
from jax.experimental.pallas import tpu as pltpu

D_MODEL = 1024
BATCH = 16
SEQ = 256
DEPTH = 4
DEC_BATCH = 2
DEC_SEQ = 2048
PAST_LEN = 512
GRID_W = 64
D_FOURIER = 256
N_FOURIER_GROUPS = 4
FOURIER_GROUP = D_FOURIER // N_FOURIER_GROUPS
D_HGRN = 256
N_HGRN_HEADS = 4
HGRN_HEAD = D_HGRN // N_HGRN_HEADS
HGRN_CHUNK = 32
HGRN_GROUP = 128
HGRN_STATE_BLOCK = 8
HGRN_SAFE_DECAY = 120.0
N_Q_HEADS = 8
N_KV_HEADS = 2
GQA = N_Q_HEADS // N_KV_HEADS
HEAD_DIM = 64
D_ATTN = N_Q_HEADS * HEAD_DIM
D_KV = N_KV_HEADS * HEAD_DIM
D_MIX = D_FOURIER + D_HGRN + D_ATTN
WINDOW = 128
ATTN_BLOCK = 128
ATTN_SCALE = HEAD_DIM ** -0.5
ROPE_BASE = 10000.0
NEG_BIG = -1e30
D_FF = 2816
N_EXPERTS = 8
TOP_K = 2
D_FF_EXPERT = 3584
DEEPNORM_ALPHA = (2 * DEPTH) ** 0.25
LN_EPS = 1e-5
RMS_EPS = 1e-6
N_MOD = 6
D_IN = 6 * 256 + D_ATTN + 2 * D_KV

TP = BATCH * SEQ
TS = DEC_BATCH * DEC_SEQ
T = TP + TS
N_COND = 1 + DEC_BATCH
COND_PAD = 8

TM = 512
TM_FFN = 512
TM_MOE = 1024
FFN_ROW_STEP = 256
TM_LN = 256
ROW_CHUNK = 256
LANES = 128
VMEM_LIMIT = 56 * 1024 * 1024

F32 = jnp.float32
BF16 = jnp.bfloat16


def _cparams(sem, vmem=VMEM_LIMIT):
    return pltpu.CompilerParams(dimension_semantics=sem, vmem_limit_bytes=vmem)


def _ln(x):
    mu = jnp.mean(x, axis=-1, keepdims=True)
    xc = x - mu
    var = jnp.mean(xc * xc, axis=-1, keepdims=True)
    return xc * lax.rsqrt(var + LN_EPS)


def _silu(x):
    return x * jax.nn.sigmoid(x)


def _split3(a):
    p0 = a.astype(BF16)
    r1 = a - p0.astype(F32)
    p1 = r1.astype(BF16)
    r2 = r1 - p1.astype(F32)
    return p0, p1, r2.astype(BF16)


def _split2(a):
    hi = a.astype(BF16)
    return hi, (a - hi.astype(F32)).astype(BF16)


def _dot(a, b):
    return jnp.dot(a, b, preferred_element_type=F32)


def _dot_nt(a, b):
    return lax.dot_general(a, b, (((1,), (1,)), ((), ())), preferred_element_type=F32)


def _dot_tn(a, b):
    return lax.dot_general(a, b, (((0,), (0,)), ((), ())), preferred_element_type=F32)


def _cond_of_tile(i, tm):
    n_ctx = TP // tm
    return jnp.where(i < n_ctx, 0, 1 + (i - n_ctx) // (DEC_SEQ // tm))


def _mod_spec(l, tm, tile_of=lambda i, *_: i):
    return pl.BlockSpec((None, None, N_MOD, D_MODEL), lambda *a: (l, _cond_of_tile(tile_of(*a), tm), 0, 0))


def _layer_row_spec(l, width):
    return pl.BlockSpec((None, 1, width), lambda *_: (l, 0, 0))


def _mod_kernel(c_ref, w_ref, b_ref, o_ref):
    a = _silu(c_ref[...]).astype(BF16)
    o_ref[...] = _dot(a, w_ref[...].astype(BF16)) + b_ref[...]


def _modulation(cond, w_mod, b_mod):
    tn = 2048
    n_out = N_MOD * D_MODEL
    return pl.pallas_call(
        _mod_kernel,
        grid=(DEPTH, n_out // tn),
        in_specs=[
            pl.BlockSpec((COND_PAD, D_MODEL), lambda l, j: (0, 0)),
            pl.BlockSpec((None, D_MODEL, tn), lambda l, j: (l, 0, j)),
            pl.BlockSpec((None, 1, tn), lambda l, j: (l, 0, j)),
        ],
        out_specs=pl.BlockSpec((None, COND_PAD, tn), lambda l, j: (l, 0, j)),
        out_shape=jax.ShapeDtypeStruct((DEPTH, COND_PAD, n_out), F32),
        compiler_params=_cparams(("parallel", "parallel")),
        name="modulation",
    )(cond, w_mod, b_mod.reshape(DEPTH, 1, n_out))


def _inproj_kernel(*refs, split):
    if split:
        x0_ref, x1_ref, mod_ref, w_ref, o_ref, wb_ref = refs
        is_ctx = pl.program_id(0) < TP // TM
        load = lambda rows: jnp.where(is_ctx, x0_ref[rows, :], x1_ref[rows, :])
    else:
        x_ref, mod_ref, w_ref, o_ref, wb_ref = refs
        load = lambda rows: x_ref[rows, :]

    @pl.when(pl.program_id(0) == 0)
    def _():
        wb_ref[...] = w_ref[...].astype(BF16)

    for r0 in range(0, TM, ROW_CHUNK):
        rows = slice(r0, r0 + ROW_CHUNK)
        h = _ln(load(rows)) * (1.0 + mod_ref[1:2, :]) + mod_ref[0:1, :]
        o_ref[rows, :] = _dot(h.astype(BF16), wb_ref[...])


def _in_projection(x, mod, w_in, l):
    split = isinstance(x, tuple)
    n_ctx = TP // TM
    if split:
        x_specs = [pl.BlockSpec((TM, D_MODEL), lambda i: (jnp.minimum(i, n_ctx - 1), 0)),
                   pl.BlockSpec((TM, D_MODEL), lambda i: (jnp.maximum(i - n_ctx, 0), 0))]
    else:
        x_specs = [pl.BlockSpec((TM, D_MODEL), lambda i: (i, 0))]
        x = (x,)
    return pl.pallas_call(
        functools.partial(_inproj_kernel, split=split),
        grid=(T // TM,),
        in_specs=x_specs + [
            _mod_spec(l, TM),
            pl.BlockSpec((None, D_MODEL, D_IN), lambda i: (l, 0, 0), pipeline_mode=pl.Buffered(1)),
        ],
        out_specs=pl.BlockSpec((TM, D_IN), lambda i: (i, 0)),
        out_shape=jax.ShapeDtypeStruct((T, D_IN), F32),
        scratch_shapes=[pltpu.VMEM((D_MODEL, D_IN), BF16)],
        compiler_params=_cparams(("arbitrary",)),
        name="in_projection",
    )(*x, mod, w_in)


def _fourier_kernel(u_ref, ch_hi_ref, ch_lo_ref, tab_ref, wf_ref, o_ref, ab_ref, *, scale):
    @pl.when(pl.program_id(1) == 0)
    def _():
        uh, ul = _split2(u_ref[...])
        ch_hi = ch_hi_ref[...]
        ab = _dot(uh, ch_hi) + _dot(uh, ch_lo_ref[...]) + _dot(ul, ch_hi)
        stacked = jnp.concatenate([ab[:, :D_FOURIER], ab[:, D_FOURIER:]], axis=0)
        ab_ref[...] = stacked.astype(BF16)

    z = (_dot(tab_ref[...], ab_ref[...]) * scale).astype(BF16)
    o_ref[...] = _dot(z, wf_ref[...].astype(BF16)).astype(BF16)


def _dft_tables(n):
    blk = 32
    t = jnp.arange(n, dtype=jnp.int32)[None, :]
    ang1 = (((jnp.arange(n // blk, dtype=jnp.int32) * blk)[:, None] * t) % n).astype(F32) * (2.0 * math.pi / n)
    ang0 = ((jnp.arange(blk, dtype=jnp.int32)[:, None] * t) % n).astype(F32) * (2.0 * math.pi / n)
    c1, s1 = jnp.cos(ang1)[:, None, :], jnp.sin(ang1)[:, None, :]
    c0, s0 = jnp.cos(ang0)[None, :, :], jnp.sin(ang0)[None, :, :]
    cos = (c1 * c0 - s1 * s0).reshape(n, n)
    sin = (s1 * c0 + c1 * s0).reshape(n, n)
    return jnp.concatenate([cos, -sin], axis=1).astype(BF16)


def _channel_tables():
    a = jnp.arange(D_FOURIER, dtype=jnp.int32)
    same = (a[:, None] // FOURIER_GROUP) == (a[None, :] // FOURIER_GROUP)
    prod = ((a[:, None] % FOURIER_GROUP) * (a[None, :] % FOURIER_GROUP)) % FOURIER_GROUP
    ang = prod.astype(F32) * (2.0 * math.pi / FOURIER_GROUP)
    c = jnp.where(same, jnp.cos(ang), 0.0)
    s = jnp.where(same, jnp.sin(ang), 0.0)
    return _split2(jnp.concatenate([c, s], axis=1))


def _fourier_mix(y_in, row0, nbatch, length, ch_tabs, pos_tabs, w_fourier, l):
    tr = min(length, 256)
    blk0 = row0 // length
    kern = functools.partial(_fourier_kernel, scale=1.0 / math.sqrt(length * FOURIER_GROUP))
    return pl.pallas_call(
        kern,
        grid=(nbatch, length // tr),
        in_specs=[
            pl.BlockSpec((length, D_FOURIER), lambda b, r: (blk0 + b, 0)),
            pl.BlockSpec((D_FOURIER, 2 * D_FOURIER), lambda b, r: (0, 0)),
            pl.BlockSpec((D_FOURIER, 2 * D_FOURIER), lambda b, r: (0, 0)),
            pl.BlockSpec((tr, 2 * length), lambda b, r: (r, 0)),
            pl.BlockSpec((None, D_FOURIER, D_FOURIER), lambda b, r: (l, 0, 0)),
        ],
        out_specs=pl.BlockSpec((tr, D_FOURIER), lambda b, r: (b * (length // tr) + r, 0)),
        out_shape=jax.ShapeDtypeStruct((nbatch * length, D_FOURIER), BF16),
        scratch_shapes=[pltpu.VMEM((2 * length, D_FOURIER), BF16)],
        compiler_params=_cparams(("parallel", "arbitrary")),
        name=f"fourier_mix_{length}",
    )(y_in, ch_tabs[0], ch_tabs[1], pos_tabs, w_fourier)


def _expand_state(cst):
    full = jnp.concatenate([cst] * N_HGRN_HEADS, axis=0)
    r = lax.broadcasted_iota(jnp.int32, full.shape, 0) // HGRN_HEAD
    c = lax.broadcasted_iota(jnp.int32, full.shape, 1) // HGRN_HEAD
    return jnp.where(r == c, full, jnp.zeros_like(full))


def _compress_state(full):
    lane_head = lax.broadcasted_iota(jnp.int32, (HGRN_HEAD, D_HGRN), 1) // HGRN_HEAD
    out = jnp.zeros((HGRN_HEAD, D_HGRN), full.dtype)
    for h in range(N_HGRN_HEADS):
        out = out + jnp.where(lane_head == h, full[h * HGRN_HEAD:(h + 1) * HGRN_HEAD, :], 0.0)
    return out


def _hgrn_pairwise_scan(d, nc, g_scr, k_scr, q_scr, v_ref, o_scr, st_scr, ones_bd):
    c = HGRN_CHUNK
    last = c - 1 if d == 0 else 0

    def chunk(ci, carry):
        row = lax.broadcasted_iota(jnp.int32, (c, c), 0)
        col = lax.broadcasted_iota(jnp.int32, (c, c), 1)
        tri = jnp.where((col <= row) if d == 0 else (col >= row), 1.0, 0.0).astype(BF16)
        cc = ci if d == 0 else nc - 1 - ci
        rows = pl.ds(pl.multiple_of(cc * c, c), c)
        g = g_scr[rows, :]
        kk = k_scr[rows, :]
        q = q_scr[rows, :]
        v = v_ref[rows, :]
        g0, g1, g2 = _split3(g)
        b = _dot(tri, g0) + _dot(tri, g1) + _dot(tri, g2)
        btot = b[last:last + 1, :]
        st = st_scr[...]
        o_inter = _dot_nt((q * jnp.exp(b)).astype(BF16), _expand_state(st.astype(BF16)))
        s_idx = lax.broadcasted_iota(jnp.int32, (c, c, D_HGRN), 0)
        t_idx = lax.broadcasted_iota(jnp.int32, (c, c, D_HGRN), 1)
        causal = (s_idx <= t_idx) if d == 0 else (s_idx >= t_idx)
        diff = b[None, :, :] - b[:, None, :]
        pair = q[None, :, :] * jnp.exp(jnp.minimum(diff, 0.0)) * kk[:, None, :]
        pair = jnp.where(causal, pair, 0.0).astype(BF16)
        attn = _dot(pair.reshape(c * c, D_HGRN), ones_bd).reshape(c, c, D_HGRN)
        o = o_inter + jnp.sum(attn * v[:, None, :], axis=0)
        if d == 0:
            o_scr[rows, :] = o
        else:
            o_scr[rows, :] = o_scr[rows, :] + o
        kd = kk * jnp.exp(btot - b)
        upd = _dot_tn(v.astype(BF16), kd.astype(BF16))
        st_scr[...] = jnp.exp(btot) * st + _compress_state(upd)
        return carry

    lax.fori_loop(0, nc, chunk, 0)


def _hgrn_factored_scan(d, nc, g_scr, k_scr, q_scr, v_ref, o_scr, st_scr, qd_scr, kd_scr, dec_scr, stb_scr):
    c = HGRN_CHUNK
    grp = HGRN_GROUP
    length = nc * c

    def group(gi, carry):
        rows = pl.ds(pl.multiple_of(gi * grp, grp), grp)
        r = lax.broadcasted_iota(jnp.int32, (grp, grp), 0)
        s = lax.broadcasted_iota(jnp.int32, (grp, grp), 1)
        same_chunk = (r // c) == (s // c)
        ordered = (s <= r) if d == 0 else (s >= r)
        tri = jnp.where(same_chunk & ordered, 1.0, 0.0).astype(BF16)
        ones_chunk = jnp.where(same_chunk, 1.0, 0.0).astype(BF16)
        g0, g1, g2 = _split3(g_scr[rows, :])
        b = _dot(tri, g0) + _dot(tri, g1) + _dot(tri, g2)
        tot = _dot(ones_chunk, g0) + _dot(ones_chunk, g1) + _dot(ones_chunk, g2)
        kk = k_scr[rows, :]
        q = q_scr[rows, :]
        half = 0.5 * tot
        qc = (q * jnp.exp(b - half)).astype(BF16)
        kh = (kk * jnp.exp(half - b)).astype(BF16)
        qd_scr[rows, :] = (q * jnp.exp(b)).astype(BF16)
        kd_scr[rows, :] = (kk * jnp.exp(tot - b)).astype(BF16)
        dec_scr[rows, :] = jnp.exp(tot)
        hs_row = lax.broadcasted_iota(jnp.int32, (N_HGRN_HEADS * grp, D_HGRN), 0) // grp
        hs_col = lax.broadcasted_iota(jnp.int32, (N_HGRN_HEADS * grp, D_HGRN), 1) // HGRN_HEAD
        same_head = hs_row == hs_col
        k_bd = jnp.where(same_head, jnp.concatenate([kh] * N_HGRN_HEADS, axis=0), 0.0)
        v_bd = jnp.where(same_head, jnp.concatenate([v_ref[rows, :].astype(BF16)] * N_HGRN_HEADS, axis=0), 0.0)
        t_idx = lax.broadcasted_iota(jnp.int32, (grp, N_HGRN_HEADS * grp), 0)
        s_idx = lax.broadcasted_iota(jnp.int32, (grp, N_HGRN_HEADS * grp), 1) % grp
        keep = ((t_idx // c) == (s_idx // c)) & ((s_idx <= t_idx) if d == 0 else (s_idx >= t_idx))
        attn = jnp.where(keep, _dot_nt(qc, k_bd), 0.0)
        o_intra = _dot(attn.astype(BF16), v_bd)
        if d == 0:
            o_scr[rows, :] = o_intra
        else:
            o_scr[rows, :] = o_scr[rows, :] + o_intra
        return carry

    lax.fori_loop(0, length // grp, group, 0, unroll=min(4, length // grp))

    def state_step(ci, carry):
        cc = ci if d == 0 else nc - 1 - ci
        r0 = pl.multiple_of(cc * c, c)
        rows = pl.ds(r0, c)
        st = st_scr[...]
        stb_scr[cc] = st.astype(BF16)
        upd = _dot_tn(v_ref[rows, :].astype(BF16), kd_scr[rows, :])
        st_scr[...] = dec_scr[pl.ds(r0, 1), :] * st + _compress_state(upd)
        return carry

    lax.fori_loop(0, nc, state_step, 0, unroll=4)

    nb = HGRN_STATE_BLOCK
    blk_rows = nb * c

    def inter(bi, carry):
        rows = pl.ds(pl.multiple_of(bi * blk_rows, blk_rows), blk_rows)
        row_chunk = lax.broadcasted_iota(jnp.int32, (blk_rows, nb * D_HGRN), 0) // c
        col_chunk = lax.broadcasted_iota(jnp.int32, (blk_rows, nb * D_HGRN), 1) // D_HGRN
        q_bd = jnp.where(row_chunk == col_chunk, jnp.concatenate([qd_scr[rows, :]] * nb, axis=1), 0.0)
        states = jnp.concatenate([_expand_state(stb_scr[bi * nb + j]) for j in range(nb)], axis=1)
        o_scr[rows, :] = o_scr[rows, :] + _dot_nt(q_bd, states)
        return carry

    lax.fori_loop(0, nc // nb, inter, 0)


def _hgrn_kernel(hq_ref, hff_ref, hfb_ref, hi_ref, hg_ref, lb_ref, gw_ref, s0_ref,
                 o_ref, sfin_ref, q_scr, g_scr, k_scr, o_scr, dec_scr, qd_scr, kd_scr, st_scr, stb_scr,
                 *, length):
    c = HGRN_CHUNK
    nc = length // c
    q_scr[...] = _silu(hq_ref[...])
    r256 = lax.broadcasted_iota(jnp.int32, (D_HGRN, D_HGRN), 0) // HGRN_HEAD
    c256 = lax.broadcasted_iota(jnp.int32, (D_HGRN, D_HGRN), 1) // HGRN_HEAD
    ones_bd = jnp.where(r256 == c256, 1.0, 0.0).astype(BF16)

    for d in range(2):
        z_ref = hff_ref if d == 0 else hfb_ref
        one_minus_f = (1.0 - lb_ref[d:d + 1, :]) * jax.nn.sigmoid(-z_ref[...])
        g_scr[...] = jnp.log1p(-one_minus_f)
        k_scr[...] = one_minus_f
        st_scr[...] = s0_ref[d]
        chunk_decay = jnp.sum(g_scr[...].reshape(nc, c, D_HGRN), axis=1)
        safe = jnp.min(chunk_decay) >= -HGRN_SAFE_DECAY
        lax.cond(
            safe,
            functools.partial(_hgrn_factored_scan, d, nc, g_scr, k_scr, q_scr, hi_ref, o_scr, st_scr,
                              qd_scr, kd_scr, dec_scr, stb_scr),
            functools.partial(_hgrn_pairwise_scan, d, nc, g_scr, k_scr, q_scr, hi_ref, o_scr, st_scr, ones_bd))
        sfin_ref[d] = st_scr[...]

    o = o_scr[...]
    s0p, s1p, s2p = _split3(o * o)
    ms = (_dot(s0p, ones_bd) + _dot(s1p, ones_bd) + _dot(s2p, ones_bd)) * (1.0 / HGRN_HEAD)
    y = o * lax.rsqrt(ms + RMS_EPS) * gw_ref[...] * _silu(hg_ref[...])
    o_ref[...] = y.astype(BF16)


def _hgrn_mix(y_in, row0, nbatch, length, lb, gw, s0, l, s0_layer):
    blk0 = row0 // length
    col = lambda j: (lambda b: (blk0 + b, j))
    kern = functools.partial(_hgrn_kernel, length=length)
    return pl.pallas_call(
        kern,
        grid=(nbatch,),
        in_specs=[
            pl.BlockSpec((length, D_HGRN), col(1)),
            pl.BlockSpec((length, D_HGRN), col(2)),
            pl.BlockSpec((length, D_HGRN), col(3)),
            pl.BlockSpec((length, D_HGRN), col(4)),
            pl.BlockSpec((length, D_HGRN), col(5)),
            pl.BlockSpec((None, 2, D_HGRN), lambda b: (l, 0, 0)),
            _layer_row_spec(l, D_HGRN),
            pl.BlockSpec((None, None, 2, HGRN_HEAD, D_HGRN), lambda b: (b, s0_layer, 0, 0, 0)),
        ],
        out_specs=[
            pl.BlockSpec((length, D_HGRN), lambda b: (b, 0)),
            pl.BlockSpec((None, 2, HGRN_HEAD, D_HGRN), lambda b: (b, 0, 0, 0)),
        ],
        out_shape=[
            jax.ShapeDtypeStruct((nbatch * length, D_HGRN), BF16),
            jax.ShapeDtypeStruct((nbatch, 2, HGRN_HEAD, D_HGRN), F32),
        ],
        scratch_shapes=[pltpu.VMEM((length, D_HGRN), F32) for _ in range(5)]
        + [pltpu.VMEM((length, D_HGRN), BF16) for _ in range(2)]
        + [pltpu.VMEM((HGRN_HEAD, D_HGRN), F32), pltpu.VMEM((length // HGRN_CHUNK, HGRN_HEAD, D_HGRN), BF16)],
        compiler_params=_cparams(("parallel",)),
        name=f"hgrn_mix_{length}",
    )(y_in, y_in, y_in, y_in, y_in, lb, gw, s0)


def _states_to_kernel_layout(s):
    return jnp.moveaxis(s, -1, -3).reshape(s.shape[:-3] + (HGRN_HEAD, D_HGRN))


def _states_from_kernel_layout(st):
    return jnp.moveaxis(st.reshape(st.shape[:-1] + (N_HGRN_HEADS, HGRN_HEAD)), -3, -1)


def _group_attention(q_heads, sinks, kv_parts):
    rows = q_heads[0].shape[0]
    qg = jnp.concatenate([(q * ATTN_SCALE).astype(BF16) for q in q_heads], axis=0)
    head = lax.broadcasted_iota(jnp.int32, (len(q_heads) * rows, 1), 0) // rows
    sink = jnp.full(head.shape, sinks[0], F32)
    for i in range(1, len(q_heads)):
        sink = jnp.where(head == i, sinks[i], sink)
    m = sink
    scores = []
    for k, _, bias in kv_parts:
        s = _dot_nt(qg, k)
        if bias is not None:
            s = s + bias
        m = jnp.maximum(m, jnp.max(s, axis=1, keepdims=True))
        scores.append(s)
    den = jnp.exp(sink - m)
    out = None
    for s, (_, v, _) in zip(scores, kv_parts):
        e = jnp.exp(s - m)
        den = den + jnp.sum(e, axis=1, keepdims=True)
        o = _dot(e.astype(BF16), v)
        out = o if out is None else out + o
    out = out * (1.0 / den)
    return [out[i * rows:(i + 1) * rows, :] for i in range(len(q_heads))]


def _ctx_attn_kernel(sink_ref, q_ref, k_ref, v_ref, o_ref, *, layer):
    outs = []
    for g in range(N_KV_HEADS):
        sl = slice(g * HEAD_DIM, (g + 1) * HEAD_DIM)
        heads = range(g * GQA, (g + 1) * GQA)
        outs += _group_attention(
            [q_ref[:, h * HEAD_DIM:(h + 1) * HEAD_DIM] for h in heads],
            [sink_ref[layer, h] for h in heads],
            [(k_ref[:, sl].astype(BF16), v_ref[:, sl].astype(BF16), None)])
    o_ref[...] = jnp.concatenate(outs, axis=1).astype(BF16)


def _context_attention(y_in, sink, l):
    qcol = (6 * 256) // D_ATTN
    kcol = (6 * 256 + D_ATTN) // D_KV
    return pl.pallas_call(
        functools.partial(_ctx_attn_kernel, layer=l),
        grid=(BATCH,),
        in_specs=[
            pl.BlockSpec(memory_space=pltpu.SMEM),
            pl.BlockSpec((SEQ, D_ATTN), lambda b: (b, qcol)),
            pl.BlockSpec((SEQ, D_KV), lambda b: (b, kcol)),
            pl.BlockSpec((SEQ, D_KV), lambda b: (b, kcol + 1)),
        ],
        out_specs=pl.BlockSpec((SEQ, D_ATTN), lambda b: (b, 0)),
        out_shape=jax.ShapeDtypeStruct((TP, D_ATTN), BF16),
        compiler_params=_cparams(("parallel",)),
        name="context_attention",
    )(sink, y_in, y_in, y_in)


def _rope(x, cos, sin):
    lane = lax.broadcasted_iota(jnp.int32, x.shape, 1)
    n_freq = HEAD_DIM // 4
    first = (lane % (2 * n_freq)) < n_freq
    swapped = jnp.where(first, pltpu.roll(x, LANES - n_freq, axis=1), pltpu.roll(x, n_freq, axis=1))
    return x * cos + swapped * sin


def _lat_attn_kernel(sink_ref, q_ref, k_ref, v_ref, kc_ref, vc_ref, cos_ref, sin_ref, bias_ref, o_ref, *, layer):
    j = pl.program_id(1)
    nb = DEC_SEQ // ATTN_BLOCK
    blk = ATTN_BLOCK
    q0 = pl.multiple_of(j * blk, blk)
    cos_q = cos_ref[pl.ds(q0, blk), :]
    sin_q = sin_ref[pl.ds(q0, blk), :]

    starts = [jnp.maximum(j - 1, 0), j, jnp.minimum(j + 1, nb - 1)]
    k_band, v_band = [], []
    for st in starts:
        r0 = pl.multiple_of(st * blk, blk)
        kb = _rope(k_ref[pl.ds(r0, blk), :], cos_ref[pl.ds(r0, blk), :], sin_ref[pl.ds(r0, blk), :])
        k_band.append(kb)
        v_band.append(v_ref[pl.ds(r0, blk), :])
    k_loc = jnp.concatenate(k_band, axis=0)
    v_loc = jnp.concatenate(v_band, axis=0)
    band_bias = bias_ref[...]

    heads_per_chunk = LANES // HEAD_DIM
    q_chunks = [_rope(q_ref[:, cg * LANES:(cg + 1) * LANES], cos_q, sin_q)
                for cg in range(N_Q_HEADS // heads_per_chunk)]

    def q_head(h):
        off = (h % heads_per_chunk) * HEAD_DIM
        return q_chunks[h // heads_per_chunk][:, off:off + HEAD_DIM]

    outs = []
    for g in range(N_KV_HEADS):
        sl = slice(g * HEAD_DIM, (g + 1) * HEAD_DIM)
        heads = range(g * GQA, (g + 1) * GQA)
        outs += _group_attention(
            [q_head(h) for h in heads],
            [sink_ref[layer, h] for h in heads],
            [(k_loc[:, sl].astype(BF16), v_loc[:, sl].astype(BF16), band_bias),
             (kc_ref[:, sl].astype(BF16), vc_ref[:, sl].astype(BF16), None)])
    o_ref[...] = jnp.concatenate(outs, axis=1).astype(BF16)


def _band_bias():
    blk = ATTN_BLOCK
    r = np.arange(GQA * blk)[:, None] % blk
    c = np.arange(3 * blk)[None, :]
    in_window = np.abs(r + blk - c) <= WINDOW
    exists = [c >= blk, c >= 0, c < 2 * blk]
    return jnp.asarray(np.stack([np.where(in_window & e, 0.0, NEG_BIG) for e in exists]).astype(np.float32))


def _latent_attention(y_in, sink, kc, vc, l, cos_t, sin_t, band_bias):
    nb = DEC_SEQ // ATTN_BLOCK
    qrow0 = TP // ATTN_BLOCK
    krow0 = TP // DEC_SEQ
    qcol = (6 * 256) // D_ATTN
    kcol = (6 * 256 + D_ATTN) // D_KV
    return pl.pallas_call(
        functools.partial(_lat_attn_kernel, layer=l),
        grid=(DEC_BATCH, nb),
        in_specs=[
            pl.BlockSpec(memory_space=pltpu.SMEM),
            pl.BlockSpec((ATTN_BLOCK, D_ATTN), lambda b, j: (qrow0 + b * nb + j, qcol)),
            pl.BlockSpec((DEC_SEQ, D_KV), lambda b, j: (krow0 + b, kcol)),
            pl.BlockSpec((DEC_SEQ, D_KV), lambda b, j: (krow0 + b, kcol + 1)),
            pl.BlockSpec((None, None, PAST_LEN, D_KV), lambda b, j: (b, l, 0, 0)),
            pl.BlockSpec((None, None, PAST_LEN, D_KV), lambda b, j: (b, l, 0, 0)),
            pl.BlockSpec((DEC_SEQ, LANES), lambda b, j: (0, 0)),
            pl.BlockSpec((DEC_SEQ, LANES), lambda b, j: (0, 0)),
            pl.BlockSpec((None, GQA * ATTN_BLOCK, 3 * ATTN_BLOCK),
                         lambda b, j: (jnp.where(j == 0, 0, jnp.where(j == nb - 1, 2, 1)), 0, 0)),
        ],
        out_specs=pl.BlockSpec((ATTN_BLOCK, D_ATTN), lambda b, j: (b * nb + j, 0)),
        out_shape=jax.ShapeDtypeStruct((TS, D_ATTN), BF16),
        compiler_params=_cparams(("parallel", "parallel")),
        name="latent_attention",
    )(sink, y_in, y_in, y_in, kc, vc, cos_t, sin_t, band_bias)


def _rope_tables():
    t = jnp.arange(DEC_SEQ)
    rows = (t // GRID_W).astype(F32)
    cols = (t % GRID_W).astype(F32)
    n_freq = HEAD_DIM // 4
    inv = ROPE_BASE ** (-jnp.arange(n_freq, dtype=F32) / n_freq)
    ar = rows[:, None] * inv
    ac = cols[:, None] * inv
    cos = jnp.concatenate([jnp.cos(ar), jnp.cos(ar), jnp.cos(ac), jnp.cos(ac)], axis=1)
    sin = jnp.concatenate([-jnp.sin(ar), jnp.sin(ar), -jnp.sin(ac), jnp.sin(ac)], axis=1)
    return jnp.tile(cos, (1, N_KV_HEADS)), jnp.tile(sin, (1, N_KV_HEADS))


def _top2_route(h2, w_ref, b_ref):
    logits = _dot(h2.astype(BF16), w_ref[...].astype(BF16)) + b_ref[...]
    lane = lax.broadcasted_iota(jnp.int32, logits.shape, 1)
    m1 = jnp.max(logits, axis=1, keepdims=True)
    i1 = jnp.min(jnp.where(logits == m1, lane, LANES), axis=1, keepdims=True)
    rest = jnp.where(lane == i1, -jnp.inf, logits)
    m2 = jnp.max(rest, axis=1, keepdims=True)
    i2 = jnp.min(jnp.where(rest == m2, lane, LANES), axis=1, keepdims=True)
    e = jnp.exp(m2 - m1)
    inv = 1.0 / (1.0 + e)
    idx = jnp.where(lane == 0, i1, jnp.where(lane == 1, i2, 0))
    return idx, jnp.where(lane == 0, inv, jnp.where(lane == 1, e * inv, 0.0))


def _outproj_kernel(a0_ref, a1_ref, b0_ref, b1_ref, c0_ref, c1_ref, *rest, route, split):
    is_ctx = pl.program_id(0) < TP // TM
    pick = lambda r0, r1, rows: jnp.where(is_ctx, r0[rows, :], r1[rows, :])
    if split:
        load_x = functools.partial(pick, rest[0], rest[1])
        rest = rest[2:]
    else:
        load_x = lambda rows, x_ref=rest[0]: x_ref[rows, :]
        rest = rest[1:]
    mod_ref, g_ref, beta_ref, w_ref = rest[:4]
    rest = rest[4:]
    if route:
        wr_ref, br_ref, x1_ref, h2_ref, idx_ref, p_ref, wb_ref = rest
    else:
        x1_ref, h2_ref, wb_ref = rest

    @pl.when(pl.program_id(0) == 0)
    def _():
        wb_ref[...] = w_ref[...].astype(BF16)

    for r0 in range(0, TM, ROW_CHUNK):
        rows = slice(r0, r0 + ROW_CHUNK)
        y = (_dot(pick(a0_ref, a1_ref, rows), wb_ref[0:D_FOURIER, :])
             + _dot(pick(b0_ref, b1_ref, rows), wb_ref[D_FOURIER:D_FOURIER + D_HGRN, :])
             + _dot(pick(c0_ref, c1_ref, rows), wb_ref[D_FOURIER + D_HGRN:D_MIX, :]))
        x1 = _ln(DEEPNORM_ALPHA * load_x(rows) + mod_ref[2:3, :] * y) * g_ref[...] + beta_ref[...]
        x1_ref[rows, :] = x1
        h2 = _ln(x1) * (1.0 + mod_ref[4:5, :]) + mod_ref[3:4, :]
        h2_ref[rows, :] = h2
        if route:
            idx_ref[rows, :], p_ref[rows, :] = _top2_route(h2, wr_ref, br_ref)


def _out_projection(mixed, x, mod, ln_g, ln_b, w_out, l, router=None):
    n_ctx = TP // TM
    row = lambda w: pl.BlockSpec((TM, w), lambda i: (i, 0))
    ctx = lambda w: pl.BlockSpec((TM, w), lambda i: (jnp.minimum(i, n_ctx - 1), 0))
    lat = lambda w: pl.BlockSpec((TM, w), lambda i: (jnp.maximum(i - n_ctx, 0), 0))
    vec = _layer_row_spec(l, D_MODEL)
    split = isinstance(x, tuple)
    x_specs = [ctx(D_MODEL), lat(D_MODEL)] if split else [row(D_MODEL)]
    x = x if split else (x,)
    in_specs = [
        ctx(D_FOURIER), lat(D_FOURIER), ctx(D_HGRN), lat(D_HGRN), ctx(D_ATTN), lat(D_ATTN), *x_specs,
        _mod_spec(l, TM),
        vec, vec,
        pl.BlockSpec((None, D_MIX, D_MODEL), lambda i: (l, 0, 0), pipeline_mode=pl.Buffered(1)),
    ]
    args = [mixed[0][0], mixed[0][1], mixed[1][0], mixed[1][1], mixed[2][0], mixed[2][1], *x, mod, ln_g, ln_b,
            w_out]
    out_specs = [row(D_MODEL), row(D_MODEL)]
    out_shape = [jax.ShapeDtypeStruct((T, D_MODEL), F32), jax.ShapeDtypeStruct((T, D_MODEL), F32)]
    if router is not None:
        w, b, ri = router
        in_specs += [pl.BlockSpec((None, D_MODEL, LANES), lambda i: (ri, 0, 0)), _layer_row_spec(ri, LANES)]
        args += [w, b]
        out_specs += [row(LANES), row(LANES)]
        out_shape += [jax.ShapeDtypeStruct((T, LANES), jnp.int32), jax.ShapeDtypeStruct((T, LANES), F32)]
    return pl.pallas_call(
        functools.partial(_outproj_kernel, route=router is not None, split=split),
        grid=(T // TM,),
        in_specs=in_specs,
        out_specs=out_specs,
        out_shape=out_shape,
        scratch_shapes=[pltpu.VMEM((D_MIX, D_MODEL), BF16)],
        compiler_params=_cparams(("arbitrary",)),
        name="out_projection_route" if router is not None else "out_projection",
    )(*args)


def _expert_changed(te_ref, i):
    return (i == 0) | (te_ref[i] != te_ref[jnp.maximum(i - 1, 0)])


def _post_ffn(x1, y, g2, ln_g, ln_b):
    return _ln(DEEPNORM_ALPHA * x1 + g2 * y) * ln_g + ln_b


def _for_used_rows(rows, out_ref, compute):
    tm = out_ref.shape[0]
    for n in range(FFN_ROW_STEP, tm + 1, FFN_ROW_STEP):
        @pl.when((rows > n - FFN_ROW_STEP) & (rows <= n))
        def _(n=n):
            compute(n)
            if n < tm:
                out_ref[n:, :] = jnp.zeros((tm - n, out_ref.shape[1]), out_ref.dtype)

    @pl.when(rows == 0)
    def _():
        out_ref[...] = jnp.zeros_like(out_ref)


def _ffn_up_kernel(te_ref, nv_ref, tr_ref, x_ref, wg_ref, wu_ref, h_ref, wb_ref):
    i = pl.program_id(1)
    fh = wg_ref.shape[-1]

    @pl.when(_expert_changed(te_ref, i))
    def _():
        wb_ref[:, :fh] = wg_ref[...].astype(BF16)
        wb_ref[:, fh:] = wu_ref[...].astype(BF16)

    def compute(n):
        ab = _dot(x_ref[0:n, :].astype(BF16), wb_ref[...])
        h_ref[0:n, :] = (_silu(ab[:, :fh]) * ab[:, fh:]).astype(BF16)

    _for_used_rows(tr_ref[i], h_ref, compute)


def _ffn_down_kernel(te_ref, nv_ref, tr_ref, h_ref, wd_ref, *rest, norm):
    if norm:
        x1_ref, mod_ref, g_ref, beta_ref, y_ref, wdb_ref = rest
    else:
        y_ref, wdb_ref = rest
    i = pl.program_id(1)

    @pl.when(_expert_changed(te_ref, i))
    def _():
        wdb_ref[...] = wd_ref[...].astype(BF16)

    if norm:
        for r0 in range(0, y_ref.shape[0], ROW_CHUNK):
            rows = slice(r0, r0 + ROW_CHUNK)
            y = _dot(h_ref[rows, :], wdb_ref[...])
            y_ref[rows, :] = _post_ffn(x1_ref[rows, :], y, mod_ref[5:6, :], g_ref[...], beta_ref[...])
    else:
        def compute(n):
            y_ref[0:n, :] = _dot(h_ref[0:n, :], wdb_ref[...])

        _for_used_rows(tr_ref[i], y_ref, compute)


def _grouped_ffn(x_rows, tile_expert, n_valid, tile_rows, w_gate, w_up, w_down, f_splits, tm, norm_args=None):
    r = x_rows.shape[0]
    nt = r // tm
    f = w_gate.shape[-1]
    fh = f // f_splits
    assert fh * f_splits == f and fh % LANES == 0
    used = lambda i, nv: jnp.minimum(i, nv[0] - 1)
    h = pl.pallas_call(
        _ffn_up_kernel,
        grid_spec=pltpu.PrefetchScalarGridSpec(
            num_scalar_prefetch=3,
            grid=(f_splits, nt),
            in_specs=[
                pl.BlockSpec((tm, D_MODEL), lambda j, i, te, nv, tr: (used(i, nv), 0)),
                pl.BlockSpec((None, D_MODEL, fh), lambda j, i, te, nv, tr: (te[i], 0, j)),
                pl.BlockSpec((None, D_MODEL, fh), lambda j, i, te, nv, tr: (te[i], 0, j)),
            ],
            out_specs=pl.BlockSpec((tm, fh), lambda j, i, te, nv, tr: (i, j)),
            scratch_shapes=[pltpu.VMEM((D_MODEL, 2 * fh), BF16)],
        ),
        out_shape=jax.ShapeDtypeStruct((r, f), BF16),
        compiler_params=_cparams(("arbitrary", "arbitrary")),
        name="ffn_up",
    )(tile_expert, n_valid, tile_rows, x_rows, w_gate, w_up)

    d_splits = 1 if norm_args is not None else 2
    dh = D_MODEL // d_splits
    in_specs = [
        pl.BlockSpec((tm, f), lambda j, i, te, nv, tr: (used(i, nv), 0)),
        pl.BlockSpec((None, f, dh), lambda j, i, te, nv, tr: (te[i], 0, j)),
    ]
    args = [tile_expert, n_valid, tile_rows, h, w_down]
    if norm_args is not None:
        assert r == T
        x1, mod, ln_g, ln_b, l = norm_args
        vec = _layer_row_spec(l, D_MODEL)
        in_specs += [
            pl.BlockSpec((tm, D_MODEL), lambda j, i, te, nv, tr: (i, 0)),
            _mod_spec(l, tm, lambda j, i, *_: i),
            vec, vec]
        args += [x1, mod, ln_g, ln_b]
    return pl.pallas_call(
        functools.partial(_ffn_down_kernel, norm=norm_args is not None),
        grid_spec=pltpu.PrefetchScalarGridSpec(
            num_scalar_prefetch=3,
            grid=(d_splits, nt),
            in_specs=in_specs,
            out_specs=pl.BlockSpec((tm, dh), lambda j, i, te, nv, tr: (i, j)),
            scratch_shapes=[pltpu.VMEM((f, dh), BF16)],
        ),
        out_shape=jax.ShapeDtypeStruct((r, D_MODEL), F32),
        compiler_params=_cparams(("arbitrary", "arbitrary")),
        name="ffn_down_norm" if norm_args is not None else "ffn_down",
    )(*args)


def _row_copy(src, dst, s, d, sem):
    return pltpu.make_async_copy(src.at[pl.ds(s, 1), :], dst.at[pl.ds(d, 1), :], sem)


def _dispatch_kernel(dest_ref, last_ref, x_ref, o_hbm, zero_scr, sem, zsem):
    i = pl.program_id(0)

    @pl.when(i == 0)
    def _():
        zero_scr[...] = jnp.zeros_like(zero_scr)

        def fill(tile):
            r0 = pl.multiple_of(tile * TM_MOE, TM_MOE)
            return pltpu.make_async_copy(zero_scr, o_hbm.at[pl.ds(r0, TM_MOE), :], zsem)

        n_tiles = o_hbm.shape[0] // TM_MOE
        min_tiles = (T * TOP_K) // TM_MOE
        jobs = [(last_ref[e] >= 0, last_ref[e]) for e in range(N_EXPERTS)]
        jobs += [(t >= last_ref[N_EXPERTS], t) for t in range(min_tiles, n_tiles)]
        for go, tile in jobs:
            @pl.when(go)
            def _(tile=tile):
                fill(tile).start()
        for go, tile in jobs:
            @pl.when(go)
            def _(tile=tile):
                fill(tile).wait()

    base = i * TM_LN

    def issue(r, carry):
        for k in range(TOP_K):
            _row_copy(x_ref, o_hbm, r, dest_ref[(base + r) * TOP_K + k], sem).start(priority=k % 2)
        return carry

    lax.fori_loop(0, TM_LN, issue, 0, unroll=8)
    for k in range(TOP_K):
        pltpu.make_async_copy(x_ref, o_hbm.at[pl.ds(0, TM_LN), :], sem).wait()


def _dispatch(h2, dest, last_tile, n_rows):
    return pl.pallas_call(
        _dispatch_kernel,
        grid_spec=pltpu.PrefetchScalarGridSpec(
            num_scalar_prefetch=2,
            grid=(T // TM_LN,),
            in_specs=[pl.BlockSpec((TM_LN, D_MODEL), lambda i, d, l: (i, 0))],
            out_specs=pl.BlockSpec(memory_space=pl.ANY),
            scratch_shapes=[pltpu.VMEM((TM_MOE, D_MODEL), F32), pltpu.SemaphoreType.DMA(()),
                            pltpu.SemaphoreType.DMA(())],
        ),
        out_shape=jax.ShapeDtypeStruct((n_rows, D_MODEL), F32),
        compiler_params=_cparams(("arbitrary",)),
        name="dispatch",
    )(dest, last_tile, h2)


def _combine_kernel(pos_ref, x1_ref, p_ref, y_hbm, mod_ref, g_ref, beta_ref, o_ref, buf, sem):
    i = pl.program_id(0)
    n = pl.num_programs(0)

    def fetch(step, slot):
        def issue(r, carry):
            for k in range(TOP_K):
                src = pos_ref[(step * TM_LN + r) * TOP_K + k]
                _row_copy(y_hbm, buf.at[slot, k], src, r, sem.at[slot, k]).start(priority=k % 2)
            return carry

        lax.fori_loop(0, TM_LN, issue, 0, unroll=8)

    @pl.when(i == 0)
    def _():
        fetch(0, 0)

    slot = i % 2

    @pl.when(i + 1 < n)
    def _():
        fetch(i + 1, 1 - slot)

    for k in range(TOP_K):
        pltpu.make_async_copy(y_hbm.at[pl.ds(0, TM_LN), :], buf.at[slot, k], sem.at[slot, k]).wait()
    y = p_ref[:, 0:1] * buf[slot, 0]
    for k in range(1, TOP_K):
        y = y + p_ref[:, k:k + 1] * buf[slot, k]
    o_ref[...] = _post_ffn(x1_ref[...], y, mod_ref[5:6, :], g_ref[...], beta_ref[...])


def _combine_norm(x1, p, y_rows, pos, mod, ln_g, ln_b, l):
    row = pl.BlockSpec((TM_LN, D_MODEL), lambda i, s: (i, 0))
    vec = _layer_row_spec(l, D_MODEL)
    return pl.pallas_call(
        _combine_kernel,
        grid_spec=pltpu.PrefetchScalarGridSpec(
            num_scalar_prefetch=1,
            grid=(T // TM_LN,),
            in_specs=[row, pl.BlockSpec((TM_LN, LANES), lambda i, s: (i, 0)),
                      pl.BlockSpec(memory_space=pl.ANY),
                      _mod_spec(l, TM_LN),
                      vec, vec],
            out_specs=row,
            scratch_shapes=[pltpu.VMEM((2, TOP_K, TM_LN, D_MODEL), F32),
                            pltpu.SemaphoreType.DMA((2, TOP_K))],
        ),
        out_shape=jax.ShapeDtypeStruct((T, D_MODEL), F32),
        compiler_params=_cparams(("arbitrary",)),
        name="combine_norm",
    )(pos, x1, p, y_rows, mod, ln_g, ln_b)


def _moe_plan(idx):
    n_assign = T * TOP_K
    n_tiles = n_assign // TM_MOE + N_EXPERTS
    e = idx[:, :TOP_K].reshape(n_assign)
    onehot = (e[:, None] == jnp.arange(N_EXPERTS, dtype=jnp.int32)[None, :]).astype(jnp.int32)
    csum = jnp.cumsum(onehot, axis=0)
    counts = csum[-1]
    tiles_e = (counts + TM_MOE - 1) // TM_MOE
    tile_end = jnp.cumsum(tiles_e)
    row0 = (tile_end - tiles_e) * TM_MOE
    dest = jnp.sum((csum - 1 + row0[None, :]) * onehot, axis=1)
    n_valid = tile_end[-1]
    tiles = jnp.arange(n_tiles, dtype=jnp.int32)
    tile_id = jnp.minimum(tiles, n_valid - 1)
    tile_expert = jnp.sum((tile_id[:, None] >= tile_end[None, :]).astype(jnp.int32), axis=1)
    tile_start = tile_end - tiles_e
    in_group = (tiles[:, None] >= tile_start[None, :]) & (tiles[:, None] < tile_end[None, :])
    left = counts[None, :] - (tiles[:, None] - tile_start[None, :]) * TM_MOE
    tile_rows = jnp.sum(jnp.where(in_group, jnp.clip(left, 0, TM_MOE), 0), axis=1)
    last_tile = jnp.concatenate([jnp.where(tiles_e > 0, tile_end - 1, -1), n_valid.reshape(1)])
    return (dest.astype(jnp.int32), tile_expert.astype(jnp.int32), n_valid.reshape(1).astype(jnp.int32),
            tile_rows.astype(jnp.int32), last_tile.astype(jnp.int32), n_tiles * TM_MOE)


def _moe_ffn(h2, x1, idx, p, mod, ln_g, ln_b, l, w_gate, w_up, w_down, expert0):
    dest, tile_expert, n_valid, tile_rows, last_tile, n_rows = _moe_plan(idx)
    x_rows = _dispatch(h2, dest, last_tile, n_rows)
    y_rows = _grouped_ffn(x_rows, tile_expert + expert0, n_valid, tile_rows, w_gate, w_up, w_down, 4, TM_MOE)
    return _combine_norm(x1, p, y_rows, dest, mod, ln_g, ln_b, l)


def _dense_ffn(h2, x1, mod, ln_g, ln_b, l, w_gate, w_up, w_down, index):
    nt = T // TM_FFN
    return _grouped_ffn(h2, jnp.full((nt,), index, jnp.int32), jnp.full((1,), nt, jnp.int32),
                        jnp.full((nt,), TM_FFN, jnp.int32), w_gate, w_up, w_down, 2, TM_FFN,
                        norm_args=(x1, mod, ln_g, ln_b, l))


def kernel(x_prompt, x_sample, c, cache_k, cache_v, state_hgrn, c_ctx, w_mod, b_mod, w_in, w_fourier, lb_logits, hgrn_norm, attn_sink, w_out, ln1_g, ln1_b, ln2_g, ln2_b, ffn_w_gate, ffn_w_up, ffn_w_down, router_w, router_b, moe_w_gate, moe_w_up, moe_w_down):
    lb_sm = jax.nn.softmax(lb_logits.astype(F32), axis=0)
    lower_bounds = jnp.clip(jnp.cumsum(lb_sm, axis=0) - lb_sm[0], 0.0, 1.0).reshape(DEPTH, 2, D_HGRN)
    gw = jnp.tile(hgrn_norm, (1, N_HGRN_HEADS)).reshape(DEPTH, 1, D_HGRN)
    per_layer = lambda v: v.reshape(DEPTH, 1, D_MODEL)
    ln1_g, ln1_b, ln2_g, ln2_b = per_layer(ln1_g), per_layer(ln1_b), per_layer(ln2_g), per_layer(ln2_b)
    n_moe = moe_w_gate.shape[0]
    router_wp = jnp.zeros((n_moe, D_MODEL, LANES), F32).at[:, :, :N_EXPERTS].set(router_w)
    router_bp = jnp.full((n_moe, 1, LANES), NEG_BIG, F32).at[:, 0, :N_EXPERTS].set(router_b)
    s0_lat = _states_to_kernel_layout(state_hgrn)
    s0_ctx = jnp.zeros((BATCH, 1, 2, HGRN_HEAD, D_HGRN), F32)

    cond = jnp.zeros((COND_PAD, D_MODEL), F32).at[0].set(c_ctx).at[1:N_COND].set(c)
    mod = _modulation(cond, w_mod, b_mod).reshape(DEPTH, COND_PAD, N_MOD, D_MODEL)

    ch_tabs = _channel_tables()
    pos_tabs_ctx = _dft_tables(SEQ)
    pos_tabs_lat = _dft_tables(DEC_SEQ)
    cos_t, sin_t = _rope_tables()
    band_bias = _band_bias()
    kc = cache_k.reshape(DEC_BATCH, DEPTH, PAST_LEN, D_KV)
    vc = cache_v.reshape(DEC_BATCH, DEPTH, PAST_LEN, D_KV)
    moe_wg = moe_w_gate.reshape(n_moe * N_EXPERTS, D_MODEL, D_FF_EXPERT)
    moe_wu = moe_w_up.reshape(n_moe * N_EXPERTS, D_MODEL, D_FF_EXPERT)
    moe_wd = moe_w_down.reshape(n_moe * N_EXPERTS, D_FF_EXPERT, D_MODEL)

    x = (x_prompt.reshape(TP, D_MODEL), x_sample.reshape(TS, D_MODEL))
    new_kv, new_s = [], []
    for l in range(DEPTH):
        y_in = _in_projection(x, mod, w_in, l)

        a_ctx = _fourier_mix(y_in, 0, BATCH, SEQ, ch_tabs, pos_tabs_ctx, w_fourier, l)
        a_lat = _fourier_mix(y_in, TP, DEC_BATCH, DEC_SEQ, ch_tabs, pos_tabs_lat, w_fourier, l)
        b_ctx, s_ctx = _hgrn_mix(y_in, 0, BATCH, SEQ, lower_bounds, gw, s0_ctx, l, 0)
        b_lat, _ = _hgrn_mix(y_in, TP, DEC_BATCH, DEC_SEQ, lower_bounds, gw, s0_lat, l, l)
        c_ctx_out = _context_attention(y_in, attn_sink, l)
        c_lat = _latent_attention(y_in, attn_sink, kc, vc, l, cos_t, sin_t, band_bias)

        k0 = 6 * 256 + D_ATTN
        new_kv.append(y_in[:TP, k0:k0 + 2 * D_KV])
        new_s.append(s_ctx)

        mixed = ((a_ctx, a_lat), (b_ctx, b_lat), (c_ctx_out, c_lat))
        i = l // 2
        if l % 2 == 0:
            x1, h2 = _out_projection(mixed, x, mod, ln1_g, ln1_b, w_out, l)
            x = _dense_ffn(h2, x1, mod, ln2_g, ln2_b, l, ffn_w_gate, ffn_w_up, ffn_w_down, i)
        else:
            x1, h2, idx, p = _out_projection(mixed, x, mod, ln1_g, ln1_b, w_out, l,
                                             router=(router_wp, router_bp, i))
            x = _moe_ffn(h2, x1, idx, p, mod, ln2_g, ln2_b, l, moe_wg, moe_wu, moe_wd, i * N_EXPERTS)

    xp = x[:TP].reshape(BATCH, SEQ, D_MODEL)
    xs = x[TP:].reshape(DEC_BATCH, DEC_SEQ, D_MODEL)
    kv = jnp.stack(new_kv, axis=0).reshape(DEPTH, BATCH, SEQ, 2, N_KV_HEADS, HEAD_DIM)
    kv = jnp.transpose(kv, (3, 1, 0, 2, 4, 5))
    states = _states_from_kernel_layout(jnp.stack(new_s, axis=1))
    return (xp, xs, kv[0], kv[1], states)
```

```python
import functools
import math

import jax
import jax.numpy as jnp
import numpy as np
from jax import lax
from jax.experimental import pallas as pl
from jax.experimental.pallas import tpu as pltpu

D_MODEL = 1024
BATCH = 16
SEQ = 256
DEPTH = 4
DEC_BATCH = 2
DEC_SEQ = 2048
PAST_LEN = 512
GRID_W = 64
D_FOURIER = 256
N_FOURIER_GROUPS = 4
FOURIER_GROUP = D_FOURIER // N_FOURIER_GROUPS
D_HGRN = 256
N_HGRN_HEADS = 4
HGRN_HEAD = D_HGRN // N_HGRN_HEADS
HGRN_CHUNK = 32
HGRN_GROUP = 128
HGRN_STATE_BLOCK = 8
HGRN_SAFE_DECAY = 120.0
N_Q_HEADS = 8
N_KV_HEADS = 2
GQA = N_Q_HEADS // N_KV_HEADS
HEAD_DIM = 64
D_ATTN = N_Q_HEADS * HEAD_DIM
D_KV = N_KV_HEADS * HEAD_DIM
D_MIX = D_FOURIER + D_HGRN + D_ATTN
WINDOW = 128
ATTN_BLOCK = 128
ATTN_SCALE = HEAD_DIM ** -0.5
LATENT_HEAD_STACK = 4
ROPE_BASE = 10000.0
NEG_BIG = -1e30
D_FF = 2816
N_EXPERTS = 8
TOP_K = 2
D_FF_EXPERT = 3584
DEEPNORM_ALPHA = (2 * DEPTH) ** 0.25
LN_EPS = 1e-5
RMS_EPS = 1e-6
N_MOD = 6
D_IN = 6 * 256 + D_ATTN + 2 * D_KV

TP = BATCH * SEQ
TS = DEC_BATCH * DEC_SEQ
T = TP + TS
N_COND = 1 + DEC_BATCH
COND_PAD = 8

TM = 512
TM_FFN = 512
TM_MOE = 1024
FFN_ROW_STEP = 256
TM_LN = 256
MXU_COLS = 256
ROW_CHUNK = 256
LANES = 128
VMEM_LIMIT = 56 * 1024 * 1024

F32 = jnp.float32
BF16 = jnp.bfloat16


def _cparams(sem, vmem=VMEM_LIMIT):
    return pltpu.CompilerParams(dimension_semantics=sem, vmem_limit_bytes=vmem)


def _ln(x):
    mu = jnp.mean(x, axis=-1, keepdims=True)
    xc = x - mu
    var = jnp.mean(xc * xc, axis=-1, keepdims=True)
    return xc * lax.rsqrt(var + LN_EPS)


def _silu(x):
    return x * jax.nn.sigmoid(x)


def _split3(a):
    p0 = a.astype(BF16)
    r1 = a - p0.astype(F32)
    p1 = r1.astype(BF16)
    r2 = r1 - p1.astype(F32)
    return p0, p1, r2.astype(BF16)


def _split2(a):
    hi = a.astype(BF16)
    return hi, (a - hi.astype(F32)).astype(BF16)


def _dot(a, b):
    return jnp.dot(a, b, preferred_element_type=F32)


def _dot_nt(a, b):
    return lax.dot_general(a, b, (((1,), (1,)), ((), ())), preferred_element_type=F32)


def _dot_tn(a, b):
    return lax.dot_general(a, b, (((0,), (0,)), ((), ())), preferred_element_type=F32)


def _cond_of_tile(i, tm):
    n_ctx = TP // tm
    return jnp.where(i < n_ctx, 0, 1 + (i - n_ctx) // (DEC_SEQ // tm))


def _mod_spec(l, tm, tile_of=lambda i, *_: i):
    return pl.BlockSpec((None, None, N_MOD, D_MODEL), lambda *a: (l, _cond_of_tile(tile_of(*a), tm), 0, 0))


def _layer_row_spec(l, width):
    return pl.BlockSpec((None, 1, width), lambda *_: (l, 0, 0))


def _mod_kernel(c_ref, w_ref, b_ref, o_ref):
    a = _silu(c_ref[...]).astype(BF16)
    o_ref[...] = _dot(a, w_ref[...].astype(BF16)) + b_ref[...]


def _modulation(cond, w_mod, b_mod):
    tn = 2048
    n_out = N_MOD * D_MODEL
    return pl.pallas_call(
        _mod_kernel,
        grid=(DEPTH, n_out // tn),
        in_specs=[
            pl.BlockSpec((COND_PAD, D_MODEL), lambda l, j: (0, 0)),
            pl.BlockSpec((None, D_MODEL, tn), lambda l, j: (l, 0, j)),
            pl.BlockSpec((None, 1, tn), lambda l, j: (l, 0, j)),
        ],
        out_specs=pl.BlockSpec((None, COND_PAD, tn), lambda l, j: (l, 0, j)),
        out_shape=jax.ShapeDtypeStruct((DEPTH, COND_PAD, n_out), F32),
        compiler_params=_cparams(("parallel", "parallel")),
        name="modulation",
    )(cond, w_mod, b_mod.reshape(DEPTH, 1, n_out))


def _inproj_kernel(*refs, split):
    if split:
        x0_ref, x1_ref, mod_ref, w_ref, o_ref, wb_ref = refs
        is_ctx = pl.program_id(0) < TP // TM
        load = lambda rows: jnp.where(is_ctx, x0_ref[rows, :], x1_ref[rows, :])
    else:
        x_ref, mod_ref, w_ref, o_ref, wb_ref = refs
        load = lambda rows: x_ref[rows, :]

    @pl.when(pl.program_id(0) == 0)
    def _():
        wb_ref[...] = w_ref[...].astype(BF16)

    for r0 in range(0, TM, ROW_CHUNK):
        rows = slice(r0, r0 + ROW_CHUNK)
        h = _ln(load(rows)) * (1.0 + mod_ref[1:2, :]) + mod_ref[0:1, :]
        o_ref[rows, :] = _dot(h.astype(BF16), wb_ref[...])


def _in_projection(x, mod, w_in, l):
    split = isinstance(x, tuple)
    n_ctx = TP // TM
    if split:
        x_specs = [pl.BlockSpec((TM, D_MODEL), lambda i: (jnp.minimum(i, n_ctx - 1), 0)),
                   pl.BlockSpec((TM, D_MODEL), lambda i: (jnp.maximum(i - n_ctx, 0), 0))]
    else:
        x_specs = [pl.BlockSpec((TM, D_MODEL), lambda i: (i, 0))]
        x = (x,)
    return pl.pallas_call(
        functools.partial(_inproj_kernel, split=split),
        grid=(T // TM,),
        in_specs=x_specs + [
            _mod_spec(l, TM),
            pl.BlockSpec((None, D_MODEL, D_IN), lambda i: (l, 0, 0), pipeline_mode=pl.Buffered(1)),
        ],
        out_specs=pl.BlockSpec((TM, D_IN), lambda i: (i, 0)),
        out_shape=jax.ShapeDtypeStruct((T, D_IN), F32),
        scratch_shapes=[pltpu.VMEM((D_MODEL, D_IN), BF16)],
        compiler_params=_cparams(("arbitrary",)),
        name="in_projection",
    )(*x, mod, w_in)


def _fourier_kernel(u_ref, ch_hi_ref, ch_lo_ref, tab_ref, wf_ref, o_ref, ab_ref, *, scale):
    @pl.when(pl.program_id(1) == 0)
    def _():
        uh, ul = _split2(u_ref[...])
        ch_hi = ch_hi_ref[...]
        ab = _dot(uh, ch_hi) + _dot(uh, ch_lo_ref[...]) + _dot(ul, ch_hi)
        stacked = jnp.concatenate([ab[:, :D_FOURIER], ab[:, D_FOURIER:]], axis=0)
        ab_ref[...] = stacked.astype(BF16)

    z = (_dot(tab_ref[...], ab_ref[...]) * scale).astype(BF16)
    o_ref[...] = _dot(z, wf_ref[...].astype(BF16)).astype(BF16)


def _dft_tables(n):
    blk = 32
    t = jnp.arange(n, dtype=jnp.int32)[None, :]
    ang1 = (((jnp.arange(n // blk, dtype=jnp.int32) * blk)[:, None] * t) % n).astype(F32) * (2.0 * math.pi / n)
    ang0 = ((jnp.arange(blk, dtype=jnp.int32)[:, None] * t) % n).astype(F32) * (2.0 * math.pi / n)
    c1, s1 = jnp.cos(ang1)[:, None, :], jnp.sin(ang1)[:, None, :]
    c0, s0 = jnp.cos(ang0)[None, :, :], jnp.sin(ang0)[None, :, :]
    cos = (c1 * c0 - s1 * s0).reshape(n, n)
    sin = (s1 * c0 + c1 * s0).reshape(n, n)
    return jnp.concatenate([cos, -sin], axis=1).astype(BF16)


def _channel_tables():
    a = jnp.arange(D_FOURIER, dtype=jnp.int32)
    same = (a[:, None] // FOURIER_GROUP) == (a[None, :] // FOURIER_GROUP)
    prod = ((a[:, None] % FOURIER_GROUP) * (a[None, :] % FOURIER_GROUP)) % FOURIER_GROUP
    ang = prod.astype(F32) * (2.0 * math.pi / FOURIER_GROUP)
    c = jnp.where(same, jnp.cos(ang), 0.0)
    s = jnp.where(same, jnp.sin(ang), 0.0)
    return _split2(jnp.concatenate([c, s], axis=1))


def _fourier_mix(y_in, row0, nbatch, length, ch_tabs, pos_tabs, w_fourier, l):
    tr = min(length, 256)
    blk0 = row0 // length
    kern = functools.partial(_fourier_kernel, scale=1.0 / math.sqrt(length * FOURIER_GROUP))
    return pl.pallas_call(
        kern,
        grid=(nbatch, length // tr),
        in_specs=[
            pl.BlockSpec((length, D_FOURIER), lambda b, r: (blk0 + b, 0)),
            pl.BlockSpec((D_FOURIER, 2 * D_FOURIER), lambda b, r: (0, 0)),
            pl.BlockSpec((D_FOURIER, 2 * D_FOURIER), lambda b, r: (0, 0)),
            pl.BlockSpec((tr, 2 * length), lambda b, r: (r, 0)),
            pl.BlockSpec((None, D_FOURIER, D_FOURIER), lambda b, r: (l, 0, 0)),
        ],
        out_specs=pl.BlockSpec((tr, D_FOURIER), lambda b, r: (b * (length // tr) + r, 0)),
        out_shape=jax.ShapeDtypeStruct((nbatch * length, D_FOURIER), BF16),
        scratch_shapes=[pltpu.VMEM((2 * length, D_FOURIER), BF16)],
        compiler_params=_cparams(("parallel", "arbitrary")),
        name=f"fourier_mix_{length}",
    )(y_in, ch_tabs[0], ch_tabs[1], pos_tabs, w_fourier)


def _expand_state(cst):
    full = jnp.concatenate([cst] * N_HGRN_HEADS, axis=0)
    r = lax.broadcasted_iota(jnp.int32, full.shape, 0) // HGRN_HEAD
    c = lax.broadcasted_iota(jnp.int32, full.shape, 1) // HGRN_HEAD
    return jnp.where(r == c, full, jnp.zeros_like(full))


def _compress_state(full):
    lane_head = lax.broadcasted_iota(jnp.int32, (HGRN_HEAD, D_HGRN), 1) // HGRN_HEAD
    out = jnp.zeros((HGRN_HEAD, D_HGRN), full.dtype)
    for h in range(N_HGRN_HEADS):
        out = out + jnp.where(lane_head == h, full[h * HGRN_HEAD:(h + 1) * HGRN_HEAD, :], 0.0)
    return out


def _hgrn_pairwise_scan(d, nc, g_scr, k_scr, q_scr, v_ref, o_scr, st_scr, ones_bd):
    c = HGRN_CHUNK
    last = c - 1 if d == 0 else 0

    def chunk(ci, carry):
        row = lax.broadcasted_iota(jnp.int32, (c, c), 0)
        col = lax.broadcasted_iota(jnp.int32, (c, c), 1)
        tri = jnp.where((col <= row) if d == 0 else (col >= row), 1.0, 0.0).astype(BF16)
        cc = ci if d == 0 else nc - 1 - ci
        rows = pl.ds(pl.multiple_of(cc * c, c), c)
        g = g_scr[rows, :]
        kk = k_scr[rows, :]
        q = q_scr[rows, :]
        v = v_ref[rows, :]
        g0, g1, g2 = _split3(g)
        b = _dot(tri, g0) + _dot(tri, g1) + _dot(tri, g2)
        btot = b[last:last + 1, :]
        st = st_scr[...]
        o_inter = _dot_nt((q * jnp.exp(b)).astype(BF16), _expand_state(st.astype(BF16)))
        s_idx = lax.broadcasted_iota(jnp.int32, (c, c, D_HGRN), 0)
        t_idx = lax.broadcasted_iota(jnp.int32, (c, c, D_HGRN), 1)
        causal = (s_idx <= t_idx) if d == 0 else (s_idx >= t_idx)
        diff = b[None, :, :] - b[:, None, :]
        pair = q[None, :, :] * jnp.exp(jnp.minimum(diff, 0.0)) * kk[:, None, :]
        pair = jnp.where(causal, pair, 0.0).astype(BF16)
        attn = _dot(pair.reshape(c * c, D_HGRN), ones_bd).reshape(c, c, D_HGRN)
        o = o_inter + jnp.sum(attn * v[:, None, :], axis=0)
        if d == 0:
            o_scr[rows, :] = o
        else:
            o_scr[rows, :] = o_scr[rows, :] + o
        kd = kk * jnp.exp(btot - b)
        upd = _dot_tn(v.astype(BF16), kd.astype(BF16))
        st_scr[...] = jnp.exp(btot) * st + _compress_state(upd)
        return carry

    lax.fori_loop(0, nc, chunk, 0)


def _hgrn_factored_scan(d, nc, g_scr, k_scr, q_scr, v_ref, o_scr, st_scr, qd_scr, kd_scr, dec_scr, stb_scr):
    c = HGRN_CHUNK
    grp = HGRN_GROUP
    length = nc * c

    def group(gi, carry):
        rows = pl.ds(pl.multiple_of(gi * grp, grp), grp)
        r = lax.broadcasted_iota(jnp.int32, (grp, grp), 0)
        s = lax.broadcasted_iota(jnp.int32, (grp, grp), 1)
        same_chunk = (r // c) == (s // c)
        ordered = (s <= r) if d == 0 else (s >= r)
        tri = jnp.where(same_chunk & ordered, 1.0, 0.0).astype(BF16)
        ones_chunk = jnp.where(same_chunk, 1.0, 0.0).astype(BF16)
        g0, g1, g2 = _split3(g_scr[rows, :])
        b = _dot(tri, g0) + _dot(tri, g1) + _dot(tri, g2)
        tot = _dot(ones_chunk, g0) + _dot(ones_chunk, g1) + _dot(ones_chunk, g2)
        kk = k_scr[rows, :]
        q = q_scr[rows, :]
        half = 0.5 * tot
        qc = (q * jnp.exp(b - half)).astype(BF16)
        kh = (kk * jnp.exp(half - b)).astype(BF16)
        qd_scr[rows, :] = (q * jnp.exp(b)).astype(BF16)
        kd_scr[rows, :] = (kk * jnp.exp(tot - b)).astype(BF16)
        dec_scr[rows, :] = jnp.exp(tot)
        hs_row = lax.broadcasted_iota(jnp.int32, (N_HGRN_HEADS * grp, D_HGRN), 0) // grp
        hs_col = lax.broadcasted_iota(jnp.int32, (N_HGRN_HEADS * grp, D_HGRN), 1) // HGRN_HEAD
        same_head = hs_row == hs_col
        k_bd = jnp.where(same_head, jnp.concatenate([kh] * N_HGRN_HEADS, axis=0), 0.0)
        v_bd = jnp.where(same_head, jnp.concatenate([v_ref[rows, :].astype(BF16)] * N_HGRN_HEADS, axis=0), 0.0)
        t_idx = lax.broadcasted_iota(jnp.int32, (grp, N_HGRN_HEADS * grp), 0)
        s_idx = lax.broadcasted_iota(jnp.int32, (grp, N_HGRN_HEADS * grp), 1) % grp
        keep = ((t_idx // c) == (s_idx // c)) & ((s_idx <= t_idx) if d == 0 else (s_idx >= t_idx))
        attn = jnp.where(keep, _dot_nt(qc, k_bd), 0.0)
        o_intra = _dot(attn.astype(BF16), v_bd)
        if d == 0:
            o_scr[rows, :] = o_intra
        else:
            o_scr[rows, :] = o_scr[rows, :] + o_intra
        return carry

    lax.fori_loop(0, length // grp, group, 0, unroll=min(4, length // grp))

    def state_step(ci, carry):
        cc = ci if d == 0 else nc - 1 - ci
        r0 = pl.multiple_of(cc * c, c)
        rows = pl.ds(r0, c)
        st = st_scr[...]
        stb_scr[cc] = st.astype(BF16)
        upd = _dot_tn(v_ref[rows, :].astype(BF16), kd_scr[rows, :])
        st_scr[...] = dec_scr[pl.ds(r0, 1), :] * st + _compress_state(upd)
        return carry

    lax.fori_loop(0, nc, state_step, 0, unroll=4)

    nb = HGRN_STATE_BLOCK
    blk_rows = nb * c

    def inter(bi, carry):
        rows = pl.ds(pl.multiple_of(bi * blk_rows, blk_rows), blk_rows)
        row_chunk = lax.broadcasted_iota(jnp.int32, (blk_rows, nb * D_HGRN), 0) // c
        col_chunk = lax.broadcasted_iota(jnp.int32, (blk_rows, nb * D_HGRN), 1) // D_HGRN
        q_bd = jnp.where(row_chunk == col_chunk, jnp.concatenate([qd_scr[rows, :]] * nb, axis=1), 0.0)
        states = jnp.concatenate([_expand_state(stb_scr[bi * nb + j]) for j in range(nb)], axis=1)
        o_scr[rows, :] = o_scr[rows, :] + _dot_nt(q_bd, states)
        return carry

    lax.fori_loop(0, nc // nb, inter, 0)


def _hgrn_kernel(hq_ref, hff_ref, hfb_ref, hi_ref, hg_ref, lb_ref, gw_ref, s0_ref,
                 o_ref, sfin_ref, q_scr, g_scr, k_scr, o_scr, dec_scr, qd_scr, kd_scr, st_scr, stb_scr,
                 *, length):
    c = HGRN_CHUNK
    nc = length // c
    q_scr[...] = _silu(hq_ref[...])
    r256 = lax.broadcasted_iota(jnp.int32, (D_HGRN, D_HGRN), 0) // HGRN_HEAD
    c256 = lax.broadcasted_iota(jnp.int32, (D_HGRN, D_HGRN), 1) // HGRN_HEAD
    ones_bd = jnp.where(r256 == c256, 1.0, 0.0).astype(BF16)

    for d in range(2):
        z_ref = hff_ref if d == 0 else hfb_ref
        one_minus_f = (1.0 - lb_ref[d:d + 1, :]) * jax.nn.sigmoid(-z_ref[...])
        g_scr[...] = jnp.log1p(-one_minus_f)
        k_scr[...] = one_minus_f
        st_scr[...] = s0_ref[d]
        chunk_decay = jnp.sum(g_scr[...].reshape(nc, c, D_HGRN), axis=1)
        safe = jnp.min(chunk_decay) >= -HGRN_SAFE_DECAY
        lax.cond(
            safe,
            functools.partial(_hgrn_factored_scan, d, nc, g_scr, k_scr, q_scr, hi_ref, o_scr, st_scr,
                              qd_scr, kd_scr, dec_scr, stb_scr),
            functools.partial(_hgrn_pairwise_scan, d, nc, g_scr, k_scr, q_scr, hi_ref, o_scr, st_scr, ones_bd))
        sfin_ref[d] = st_scr[...]

    o = o_scr[...]
    s0p, s1p, s2p = _split3(o * o)
    ms = (_dot(s0p, ones_bd) + _dot(s1p, ones_bd) + _dot(s2p, ones_bd)) * (1.0 / HGRN_HEAD)
    y = o * lax.rsqrt(ms + RMS_EPS) * gw_ref[...] * _silu(hg_ref[...])
    o_ref[...] = y.astype(BF16)


def _hgrn_mix(y_in, row0, nbatch, length, lb, gw, s0, l, s0_layer):
    blk0 = row0 // length
    col = lambda j: (lambda b: (blk0 + b, j))
    kern = functools.partial(_hgrn_kernel, length=length)
    return pl.pallas_call(
        kern,
        grid=(nbatch,),
        in_specs=[
            pl.BlockSpec((length, D_HGRN), col(1)),
            pl.BlockSpec((length, D_HGRN), col(2)),
            pl.BlockSpec((length, D_HGRN), col(3)),
            pl.BlockSpec((length, D_HGRN), col(4)),
            pl.BlockSpec((length, D_HGRN), col(5)),
            pl.BlockSpec((None, 2, D_HGRN), lambda b: (l, 0, 0)),
            _layer_row_spec(l, D_HGRN),
            pl.BlockSpec((None, None, 2, HGRN_HEAD, D_HGRN), lambda b: (b, s0_layer, 0, 0, 0)),
        ],
        out_specs=[
            pl.BlockSpec((length, D_HGRN), lambda b: (b, 0)),
            pl.BlockSpec((None, 2, HGRN_HEAD, D_HGRN), lambda b: (b, 0, 0, 0)),
        ],
        out_shape=[
            jax.ShapeDtypeStruct((nbatch * length, D_HGRN), BF16),
            jax.ShapeDtypeStruct((nbatch, 2, HGRN_HEAD, D_HGRN), F32),
        ],
        scratch_shapes=[pltpu.VMEM((length, D_HGRN), F32) for _ in range(5)]
        + [pltpu.VMEM((length, D_HGRN), BF16) for _ in range(2)]
        + [pltpu.VMEM((HGRN_HEAD, D_HGRN), F32), pltpu.VMEM((length // HGRN_CHUNK, HGRN_HEAD, D_HGRN), BF16)],
        compiler_params=_cparams(("parallel",)),
        name=f"hgrn_mix_{length}",
    )(y_in, y_in, y_in, y_in, y_in, lb, gw, s0)


def _states_to_kernel_layout(s):
    return jnp.moveaxis(s, -1, -3).reshape(s.shape[:-3] + (HGRN_HEAD, D_HGRN))


def _states_from_kernel_layout(st):
    return jnp.moveaxis(st.reshape(st.shape[:-1] + (N_HGRN_HEADS, HGRN_HEAD)), -3, -1)


def _group_attention(q_heads, sinks, kv_parts):
    rows = q_heads[0].shape[0]
    qg = jnp.concatenate([(q * ATTN_SCALE).astype(BF16) for q in q_heads], axis=0)
    head = lax.broadcasted_iota(jnp.int32, (len(q_heads) * rows, 1), 0) // rows
    sink = jnp.full(head.shape, sinks[0], F32)
    for i in range(1, len(q_heads)):
        sink = jnp.where(head == i, sinks[i], sink)
    m = sink
    scores = []
    for k, _, bias in kv_parts:
        s = _dot_nt(qg, k)
        if bias is not None:
            s = s + bias
        m = jnp.maximum(m, jnp.max(s, axis=1, keepdims=True))
        scores.append(s)
    den = jnp.exp(sink - m)
    out = None
    for s, (_, v, _) in zip(scores, kv_parts):
        e = jnp.exp(s - m)
        den = den + jnp.sum(e, axis=1, keepdims=True)
        o = _dot(e.astype(BF16), v)
        out = o if out is None else out + o
    out = out * (1.0 / den)
    return [out[i * rows:(i + 1) * rows, :] for i in range(len(q_heads))]


def _ctx_attn_kernel(sink_ref, q_ref, k_ref, v_ref, o_ref, *, layer):
    outs = []
    for g in range(N_KV_HEADS):
        sl = slice(g * HEAD_DIM, (g + 1) * HEAD_DIM)
        heads = range(g * GQA, (g + 1) * GQA)
        outs += _group_attention(
            [q_ref[:, h * HEAD_DIM:(h + 1) * HEAD_DIM] for h in heads],
            [sink_ref[layer, h] for h in heads],
            [(k_ref[:, sl].astype(BF16), v_ref[:, sl].astype(BF16), None)])
    o_ref[...] = jnp.concatenate(outs, axis=1).astype(BF16)


def _context_attention(y_in, sink, l):
    qcol = (6 * 256) // D_ATTN
    kcol = (6 * 256 + D_ATTN) // D_KV
    return pl.pallas_call(
        functools.partial(_ctx_attn_kernel, layer=l),
        grid=(BATCH,),
        in_specs=[
            pl.BlockSpec(memory_space=pltpu.SMEM),
            pl.BlockSpec((SEQ, D_ATTN), lambda b: (b, qcol)),
            pl.BlockSpec((SEQ, D_KV), lambda b: (b, kcol)),
            pl.BlockSpec((SEQ, D_KV), lambda b: (b, kcol + 1)),
        ],
        out_specs=pl.BlockSpec((SEQ, D_ATTN), lambda b: (b, 0)),
        out_shape=jax.ShapeDtypeStruct((TP, D_ATTN), BF16),
        compiler_params=_cparams(("parallel",)),
        name="context_attention",
    )(sink, y_in, y_in, y_in)


def _rope(x, cos, sin):
    lane = lax.broadcasted_iota(jnp.int32, x.shape, 1)
    n_freq = HEAD_DIM // 4
    first = (lane % (2 * n_freq)) < n_freq
    swapped = jnp.where(first, pltpu.roll(x, LANES - n_freq, axis=1), pltpu.roll(x, n_freq, axis=1))
    return x * cos + swapped * sin


def _lat_attn_kernel(sink_ref, q_ref, k_ref, v_ref, kc_ref, vc_ref, cos_ref, sin_ref, bias_ref, o_ref, *, layer):
    j = pl.program_id(1)
    nb = DEC_SEQ // ATTN_BLOCK
    blk = ATTN_BLOCK
    q0 = pl.multiple_of(j * blk, blk)
    cos_q = cos_ref[pl.ds(q0, blk), :]
    sin_q = sin_ref[pl.ds(q0, blk), :]

    starts = [jnp.maximum(j - 1, 0), j, jnp.minimum(j + 1, nb - 1)]
    k_band, v_band = [], []
    for st in starts:
        r0 = pl.multiple_of(st * blk, blk)
        kb = _rope(k_ref[pl.ds(r0, blk), :], cos_ref[pl.ds(r0, blk), :], sin_ref[pl.ds(r0, blk), :])
        k_band.append(kb)
        v_band.append(v_ref[pl.ds(r0, blk), :])
    k_loc = jnp.concatenate(k_band, axis=0)
    v_loc = jnp.concatenate(v_band, axis=0)
    band_bias = bias_ref[...]

    heads_per_chunk = LANES // HEAD_DIM
    q_chunks = [_rope(q_ref[:, cg * LANES:(cg + 1) * LANES], cos_q, sin_q)
                for cg in range(N_Q_HEADS // heads_per_chunk)]

    def q_head(h):
        off = (h % heads_per_chunk) * HEAD_DIM
        return q_chunks[h // heads_per_chunk][:, off:off + HEAD_DIM]

    outs = []
    for h0 in range(0, N_Q_HEADS, LATENT_HEAD_STACK):
        g = h0 // GQA
        sl = slice(g * HEAD_DIM, (g + 1) * HEAD_DIM)
        heads = range(h0, h0 + LATENT_HEAD_STACK)
        outs += _group_attention(
            [q_head(h) for h in heads],
            [sink_ref[layer, h] for h in heads],
            [(k_loc[:, sl].astype(BF16), v_loc[:, sl].astype(BF16), band_bias[:LATENT_HEAD_STACK * blk, :]),
             (kc_ref[:, sl].astype(BF16), vc_ref[:, sl].astype(BF16), None)])
    o_ref[...] = jnp.concatenate(outs, axis=1).astype(BF16)


def _band_bias():
    blk = ATTN_BLOCK
    r = np.arange(GQA * blk)[:, None] % blk
    c = np.arange(3 * blk)[None, :]
    in_window = np.abs(r + blk - c) <= WINDOW
    exists = [c >= blk, c >= 0, c < 2 * blk]
    return jnp.asarray(np.stack([np.where(in_window & e, 0.0, NEG_BIG) for e in exists]).astype(np.float32))


def _latent_attention(y_in, sink, kc, vc, l, cos_t, sin_t, band_bias):
    nb = DEC_SEQ // ATTN_BLOCK
    qrow0 = TP // ATTN_BLOCK
    krow0 = TP // DEC_SEQ
    qcol = (6 * 256) // D_ATTN
    kcol = (6 * 256 + D_ATTN) // D_KV
    return pl.pallas_call(
        functools.partial(_lat_attn_kernel, layer=l),
        grid=(DEC_BATCH, nb),
        in_specs=[
            pl.BlockSpec(memory_space=pltpu.SMEM),
            pl.BlockSpec((ATTN_BLOCK, D_ATTN), lambda b, j: (qrow0 + b * nb + j, qcol)),
            pl.BlockSpec((DEC_SEQ, D_KV), lambda b, j: (krow0 + b, kcol)),
            pl.BlockSpec((DEC_SEQ, D_KV), lambda b, j: (krow0 + b, kcol + 1)),
            pl.BlockSpec((None, None, PAST_LEN, D_KV), lambda b, j: (b, l, 0, 0)),
            pl.BlockSpec((None, None, PAST_LEN, D_KV), lambda b, j: (b, l, 0, 0)),
            pl.BlockSpec((DEC_SEQ, LANES), lambda b, j: (0, 0)),
            pl.BlockSpec((DEC_SEQ, LANES), lambda b, j: (0, 0)),
            pl.BlockSpec((None, GQA * ATTN_BLOCK, 3 * ATTN_BLOCK),
                         lambda b, j: (jnp.where(j == 0, 0, jnp.where(j == nb - 1, 2, 1)), 0, 0)),
        ],
        out_specs=pl.BlockSpec((ATTN_BLOCK, D_ATTN), lambda b, j: (b * nb + j, 0)),
        out_shape=jax.ShapeDtypeStruct((TS, D_ATTN), BF16),
        compiler_params=_cparams(("parallel", "parallel")),
        name="latent_attention",
    )(sink, y_in, y_in, y_in, kc, vc, cos_t, sin_t, band_bias)


def _rope_tables():
    t = jnp.arange(DEC_SEQ)
    rows = (t // GRID_W).astype(F32)
    cols = (t % GRID_W).astype(F32)
    n_freq = HEAD_DIM // 4
    inv = ROPE_BASE ** (-jnp.arange(n_freq, dtype=F32) / n_freq)
    ar = rows[:, None] * inv
    ac = cols[:, None] * inv
    cos = jnp.concatenate([jnp.cos(ar), jnp.cos(ar), jnp.cos(ac), jnp.cos(ac)], axis=1)
    sin = jnp.concatenate([-jnp.sin(ar), jnp.sin(ar), -jnp.sin(ac), jnp.sin(ac)], axis=1)
    return jnp.tile(cos, (1, N_KV_HEADS)), jnp.tile(sin, (1, N_KV_HEADS))


def _top2_route(h2, w_ref, b_ref):
    logits = _dot(h2.astype(BF16), w_ref[...].astype(BF16)) + b_ref[...]
    lane = lax.broadcasted_iota(jnp.int32, logits.shape, 1)
    m1 = jnp.max(logits, axis=1, keepdims=True)
    i1 = jnp.min(jnp.where(logits == m1, lane, LANES), axis=1, keepdims=True)
    rest = jnp.where(lane == i1, -jnp.inf, logits)
    m2 = jnp.max(rest, axis=1, keepdims=True)
    i2 = jnp.min(jnp.where(rest == m2, lane, LANES), axis=1, keepdims=True)
    e = jnp.exp(m2 - m1)
    inv = 1.0 / (1.0 + e)
    idx = jnp.where(lane == 0, i1, jnp.where(lane == 1, i2, 0))
    return idx, jnp.where(lane == 0, inv, jnp.where(lane == 1, e * inv, 0.0))


def _outproj_kernel(a0_ref, a1_ref, b0_ref, b1_ref, c0_ref, c1_ref, *rest, route, split):
    is_ctx = pl.program_id(0) < TP // TM
    pick = lambda r0, r1, rows: jnp.where(is_ctx, r0[rows, :], r1[rows, :])
    if split:
        load_x = functools.partial(pick, rest[0], rest[1])
        rest = rest[2:]
    else:
        load_x = lambda rows, x_ref=rest[0]: x_ref[rows, :]
        rest = rest[1:]
    mod_ref, g_ref, beta_ref, w_ref = rest[:4]
    rest = rest[4:]
    if route:
        wr_ref, br_ref, x1_ref, h2_ref, idx_ref, p_ref, wb_ref = rest
    else:
        x1_ref, h2_ref, wb_ref = rest

    @pl.when(pl.program_id(0) == 0)
    def _():
        wb_ref[...] = w_ref[...].astype(BF16)

    for r0 in range(0, TM, ROW_CHUNK):
        rows = slice(r0, r0 + ROW_CHUNK)
        y = (_dot(pick(a0_ref, a1_ref, rows), wb_ref[0:D_FOURIER, :])
             + _dot(pick(b0_ref, b1_ref, rows), wb_ref[D_FOURIER:D_FOURIER + D_HGRN, :])
             + _dot(pick(c0_ref, c1_ref, rows), wb_ref[D_FOURIER + D_HGRN:D_MIX, :]))
        x1 = _ln(DEEPNORM_ALPHA * load_x(rows) + mod_ref[2:3, :] * y) * g_ref[...] + beta_ref[...]
        x1_ref[rows, :] = x1
        h2 = _ln(x1) * (1.0 + mod_ref[4:5, :]) + mod_ref[3:4, :]
        h2_ref[rows, :] = h2
        if route:
            idx_ref[rows, :], p_ref[rows, :] = _top2_route(h2, wr_ref, br_ref)


def _out_projection(mixed, x, mod, ln_g, ln_b, w_out, l, router=None):
    n_ctx = TP // TM
    row = lambda w: pl.BlockSpec((TM, w), lambda i: (i, 0))
    ctx = lambda w: pl.BlockSpec((TM, w), lambda i: (jnp.minimum(i, n_ctx - 1), 0))
    lat = lambda w: pl.BlockSpec((TM, w), lambda i: (jnp.maximum(i - n_ctx, 0), 0))
    vec = _layer_row_spec(l, D_MODEL)
    split = isinstance(x, tuple)
    x_specs = [ctx(D_MODEL), lat(D_MODEL)] if split else [row(D_MODEL)]
    x = x if split else (x,)
    in_specs = [
        ctx(D_FOURIER), lat(D_FOURIER), ctx(D_HGRN), lat(D_HGRN), ctx(D_ATTN), lat(D_ATTN), *x_specs,
        _mod_spec(l, TM),
        vec, vec,
        pl.BlockSpec((None, D_MIX, D_MODEL), lambda i: (l, 0, 0), pipeline_mode=pl.Buffered(1)),
    ]
    args = [mixed[0][0], mixed[0][1], mixed[1][0], mixed[1][1], mixed[2][0], mixed[2][1], *x, mod, ln_g, ln_b,
            w_out]
    out_specs = [row(D_MODEL), row(D_MODEL)]
    out_shape = [jax.ShapeDtypeStruct((T, D_MODEL), F32), jax.ShapeDtypeStruct((T, D_MODEL), F32)]
    if router is not None:
        w, b, ri = router
        in_specs += [pl.BlockSpec((None, D_MODEL, LANES), lambda i: (ri, 0, 0)), _layer_row_spec(ri, LANES)]
        args += [w, b]
        out_specs += [row(LANES), row(LANES)]
        out_shape += [jax.ShapeDtypeStruct((T, LANES), jnp.int32), jax.ShapeDtypeStruct((T, LANES), F32)]
    return pl.pallas_call(
        functools.partial(_outproj_kernel, route=router is not None, split=split),
        grid=(T // TM,),
        in_specs=in_specs,
        out_specs=out_specs,
        out_shape=out_shape,
        scratch_shapes=[pltpu.VMEM((D_MIX, D_MODEL), BF16)],
        compiler_params=_cparams(("arbitrary",)),
        name="out_projection_route" if router is not None else "out_projection",
    )(*args)


def _expert_changed(te_ref, i):
    return (i == 0) | (te_ref[i] != te_ref[jnp.maximum(i - 1, 0)])


def _post_ffn(x1, y, g2, ln_g, ln_b):
    return _ln(DEEPNORM_ALPHA * x1 + g2 * y) * ln_g + ln_b


def _for_used_rows(rows, out_ref, compute):
    tm = out_ref.shape[0]
    for n in range(FFN_ROW_STEP, tm + 1, FFN_ROW_STEP):
        @pl.when((rows > n - FFN_ROW_STEP) & (rows <= n))
        def _(n=n):
            compute(n)
            if n < tm:
                out_ref[n:, :] = jnp.zeros((tm - n, out_ref.shape[1]), out_ref.dtype)

    @pl.when(rows == 0)
    def _():
        out_ref[...] = jnp.zeros_like(out_ref)


def _cast_columns(w_refs, c0, c1):
    parts, base = [], 0
    for w_ref in w_refs:
        width = w_ref.shape[-1]
        lo, hi = max(c0 - base, 0), min(c1 - base, width)
        if lo < hi:
            parts.append(w_ref[:, lo:hi].astype(BF16))
        base += width
    return parts[0] if len(parts) == 1 else jnp.concatenate(parts, axis=1)


def _dot_streamed(x, w_refs):
    total = sum(w_ref.shape[-1] for w_ref in w_refs)
    outs = [_dot(x, _cast_columns(w_refs, c0, min(c0 + MXU_COLS, total))) for c0 in range(0, total, MXU_COLS)]
    return outs[0] if len(outs) == 1 else jnp.concatenate(outs, axis=1)


def _ffn_up_kernel(te_ref, nv_ref, tr_ref, x_ref, wg_ref, wu_ref, h_ref, *scratch, stream):
    i = pl.program_id(1)
    fh = wg_ref.shape[-1]
    if not stream:
        wb_ref, = scratch

        @pl.when(_expert_changed(te_ref, i))
        def _():
            wb_ref[:, :fh] = wg_ref[...].astype(BF16)
            wb_ref[:, fh:] = wu_ref[...].astype(BF16)

    def compute(n):
        x = x_ref[0:n, :].astype(BF16)
        ab = _dot_streamed(x, (wg_ref, wu_ref)) if stream else _dot(x, wb_ref[...])
        h_ref[0:n, :] = (_silu(ab[:, :fh]) * ab[:, fh:]).astype(BF16)

    _for_used_rows(tr_ref[i], h_ref, compute)


def _ffn_down_kernel(te_ref, nv_ref, tr_ref, h_ref, wd_ref, *rest, norm):
    i = pl.program_id(1)
    if norm:
        x1_ref, mod_ref, g_ref, beta_ref, y_ref, wdb_ref = rest

        @pl.when(_expert_changed(te_ref, i))
        def _():
            wdb_ref[...] = wd_ref[...].astype(BF16)

        for r0 in range(0, y_ref.shape[0], ROW_CHUNK):
            rows = slice(r0, r0 + ROW_CHUNK)
            y = _dot(h_ref[rows, :], wdb_ref[...])
            y_ref[rows, :] = _post_ffn(x1_ref[rows, :], y, mod_ref[5:6, :], g_ref[...], beta_ref[...])
    else:
        y_ref, = rest

        def compute(n):
            y_ref[0:n, :] = _dot_streamed(h_ref[0:n, :], (wd_ref,))

        _for_used_rows(tr_ref[i], y_ref, compute)


def _grouped_ffn(x_rows, tile_expert, n_valid, tile_rows, w_gate, w_up, w_down, f_splits, tm, norm_args=None):
    r = x_rows.shape[0]
    nt = r // tm
    f = w_gate.shape[-1]
    fh = f // f_splits
    assert fh * f_splits == f and fh % LANES == 0
    used = lambda i, nv: jnp.minimum(i, nv[0] - 1)
    stream = norm_args is None
    h = pl.pallas_call(
        functools.partial(_ffn_up_kernel, stream=stream),
        grid_spec=pltpu.PrefetchScalarGridSpec(
            num_scalar_prefetch=3,
            grid=(f_splits, nt),
            in_specs=[
                pl.BlockSpec((tm, D_MODEL), lambda j, i, te, nv, tr: (used(i, nv), 0)),
                pl.BlockSpec((None, D_MODEL, fh), lambda j, i, te, nv, tr: (te[i], 0, j)),
                pl.BlockSpec((None, D_MODEL, fh), lambda j, i, te, nv, tr: (te[i], 0, j)),
            ],
            out_specs=pl.BlockSpec((tm, fh), lambda j, i, te, nv, tr: (i, j)),
            scratch_shapes=[] if stream else [pltpu.VMEM((D_MODEL, 2 * fh), BF16)],
        ),
        out_shape=jax.ShapeDtypeStruct((r, f), BF16),
        compiler_params=_cparams(("arbitrary", "arbitrary")),
        name="ffn_up",
    )(tile_expert, n_valid, tile_rows, x_rows, w_gate, w_up)

    d_splits = 1 if norm_args is not None else 2
    dh = D_MODEL // d_splits
    in_specs = [
        pl.BlockSpec((tm, f), lambda j, i, te, nv, tr: (used(i, nv), 0)),
        pl.BlockSpec((None, f, dh), lambda j, i, te, nv, tr: (te[i], 0, j)),
    ]
    args = [tile_expert, n_valid, tile_rows, h, w_down]
    if norm_args is not None:
        assert r == T
        x1, mod, ln_g, ln_b, l = norm_args
        vec = _layer_row_spec(l, D_MODEL)
        in_specs += [
            pl.BlockSpec((tm, D_MODEL), lambda j, i, te, nv, tr: (i, 0)),
            _mod_spec(l, tm, lambda j, i, *_: i),
            vec, vec]
        args += [x1, mod, ln_g, ln_b]
    return pl.pallas_call(
        functools.partial(_ffn_down_kernel, norm=norm_args is not None),
        grid_spec=pltpu.PrefetchScalarGridSpec(
            num_scalar_prefetch=3,
            grid=(d_splits, nt),
            in_specs=in_specs,
            out_specs=pl.BlockSpec((tm, dh), lambda j, i, te, nv, tr: (i, j)),
            scratch_shapes=[] if stream else [pltpu.VMEM((f, dh), BF16)],
        ),
        out_shape=jax.ShapeDtypeStruct((r, D_MODEL), F32),
        compiler_params=_cparams(("arbitrary", "arbitrary")),
        name="ffn_down_norm" if norm_args is not None else "ffn_down",
    )(*args)


def _row_copy(src, dst, s, d, sem):
    return pltpu.make_async_copy(src.at[pl.ds(s, 1), :], dst.at[pl.ds(d, 1), :], sem)


def _dispatch_kernel(dest_ref, last_ref, x_ref, o_hbm, zero_scr, sem, zsem):
    i = pl.program_id(0)

    @pl.when(i == 0)
    def _():
        zero_scr[...] = jnp.zeros_like(zero_scr)

        def fill(tile):
            r0 = pl.multiple_of(tile * TM_MOE, TM_MOE)
            return pltpu.make_async_copy(zero_scr, o_hbm.at[pl.ds(r0, TM_MOE), :], zsem)

        n_tiles = o_hbm.shape[0] // TM_MOE
        min_tiles = (T * TOP_K) // TM_MOE
        jobs = [(last_ref[e] >= 0, last_ref[e]) for e in range(N_EXPERTS)]
        jobs += [(t >= last_ref[N_EXPERTS], t) for t in range(min_tiles, n_tiles)]
        for go, tile in jobs:
            @pl.when(go)
            def _(tile=tile):
                fill(tile).start()
        for go, tile in jobs:
            @pl.when(go)
            def _(tile=tile):
                fill(tile).wait()

    base = i * TM_LN

    def issue(r, carry):
        for k in range(TOP_K):
            _row_copy(x_ref, o_hbm, r, dest_ref[(base + r) * TOP_K + k], sem).start(priority=k % 2)
        return carry

    lax.fori_loop(0, TM_LN, issue, 0, unroll=8)
    for k in range(TOP_K):
        pltpu.make_async_copy(x_ref, o_hbm.at[pl.ds(0, TM_LN), :], sem).wait()


def _dispatch(h2, dest, last_tile, n_rows):
    return pl.pallas_call(
        _dispatch_kernel,
        grid_spec=pltpu.PrefetchScalarGridSpec(
            num_scalar_prefetch=2,
            grid=(T // TM_LN,),
            in_specs=[pl.BlockSpec((TM_LN, D_MODEL), lambda i, d, l: (i, 0))],
            out_specs=pl.BlockSpec(memory_space=pl.ANY),
            scratch_shapes=[pltpu.VMEM((TM_MOE, D_MODEL), F32), pltpu.SemaphoreType.DMA(()),
                            pltpu.SemaphoreType.DMA(())],
        ),
        out_shape=jax.ShapeDtypeStruct((n_rows, D_MODEL), F32),
        compiler_params=_cparams(("arbitrary",)),
        name="dispatch",
    )(dest, last_tile, h2)


def _combine_kernel(pos_ref, x1_ref, p_ref, y_hbm, mod_ref, g_ref, beta_ref, o_ref, buf, sem):
    i = pl.program_id(0)
    n = pl.num_programs(0)

    def fetch(step, slot):
        def issue(r, carry):
            for k in range(TOP_K):
                src = pos_ref[(step * TM_LN + r) * TOP_K + k]
                _row_copy(y_hbm, buf.at[slot, k], src, r, sem.at[slot, k]).start(priority=k % 2)
            return carry

        lax.fori_loop(0, TM_LN, issue, 0, unroll=8)

    @pl.when(i == 0)
    def _():
        fetch(0, 0)

    slot = i % 2

    @pl.when(i + 1 < n)
    def _():
        fetch(i + 1, 1 - slot)

    for k in range(TOP_K):
        pltpu.make_async_copy(y_hbm.at[pl.ds(0, TM_LN), :], buf.at[slot, k], sem.at[slot, k]).wait()
    y = p_ref[:, 0:1] * buf[slot, 0]
    for k in range(1, TOP_K):
        y = y + p_ref[:, k:k + 1] * buf[slot, k]
    o_ref[...] = _post_ffn(x1_ref[...], y, mod_ref[5:6, :], g_ref[...], beta_ref[...])


def _combine_norm(x1, p, y_rows, pos, mod, ln_g, ln_b, l):
    row = pl.BlockSpec((TM_LN, D_MODEL), lambda i, s: (i, 0))
    vec = _layer_row_spec(l, D_MODEL)
    return pl.pallas_call(
        _combine_kernel,
        grid_spec=pltpu.PrefetchScalarGridSpec(
            num_scalar_prefetch=1,
            grid=(T // TM_LN,),
            in_specs=[row, pl.BlockSpec((TM_LN, LANES), lambda i, s: (i, 0)),
                      pl.BlockSpec(memory_space=pl.ANY),
                      _mod_spec(l, TM_LN),
                      vec, vec],
            out_specs=row,
            scratch_shapes=[pltpu.VMEM((2, TOP_K, TM_LN, D_MODEL), F32),
                            pltpu.SemaphoreType.DMA((2, TOP_K))],
        ),
        out_shape=jax.ShapeDtypeStruct((T, D_MODEL), F32),
        compiler_params=_cparams(("arbitrary",)),
        name="combine_norm",
    )(pos, x1, p, y_rows, mod, ln_g, ln_b)


def _moe_plan(idx):
    n_assign = T * TOP_K
    n_tiles = n_assign // TM_MOE + N_EXPERTS
    e = idx[:, :TOP_K].reshape(n_assign)
    onehot = (e[:, None] == jnp.arange(N_EXPERTS, dtype=jnp.int32)[None, :]).astype(jnp.int32)
    csum = jnp.cumsum(onehot, axis=0)
    counts = csum[-1]
    tiles_e = (counts + TM_MOE - 1) // TM_MOE
    tile_end = jnp.cumsum(tiles_e)
    row0 = (tile_end - tiles_e) * TM_MOE
    dest = jnp.sum((csum - 1 + row0[None, :]) * onehot, axis=1)
    n_valid = tile_end[-1]
    tiles = jnp.arange(n_tiles, dtype=jnp.int32)
    tile_id = jnp.minimum(tiles, n_valid - 1)
    tile_expert = jnp.sum((tile_id[:, None] >= tile_end[None, :]).astype(jnp.int32), axis=1)
    tile_start = tile_end - tiles_e
    in_group = (tiles[:, None] >= tile_start[None, :]) & (tiles[:, None] < tile_end[None, :])
    left = counts[None, :] - (tiles[:, None] - tile_start[None, :]) * TM_MOE
    tile_rows = jnp.sum(jnp.where(in_group, jnp.clip(left, 0, TM_MOE), 0), axis=1)
    last_tile = jnp.concatenate([jnp.where(tiles_e > 0, tile_end - 1, -1), n_valid.reshape(1)])
    return (dest.astype(jnp.int32), tile_expert.astype(jnp.int32), n_valid.reshape(1).astype(jnp.int32),
            tile_rows.astype(jnp.int32), last_tile.astype(jnp.int32), n_tiles * TM_MOE)


def _moe_ffn(h2, x1, idx, p, mod, ln_g, ln_b, l, w_gate, w_up, w_down, expert0):
    dest, tile_expert, n_valid, tile_rows, last_tile, n_rows = _moe_plan(idx)
    x_rows = _dispatch(h2, dest, last_tile, n_rows)
    y_rows = _grouped_ffn(x_rows, tile_expert + expert0, n_valid, tile_rows, w_gate, w_up, w_down, 4, TM_MOE)
    return _combine_norm(x1, p, y_rows, dest, mod, ln_g, ln_b, l)


def _dense_ffn(h2, x1, mod, ln_g, ln_b, l, w_gate, w_up, w_down, index):
    nt = T // TM_FFN
    return _grouped_ffn(h2, jnp.full((nt,), index, jnp.int32), jnp.full((1,), nt, jnp.int32),
                        jnp.full((nt,), TM_FFN, jnp.int32), w_gate, w_up, w_down, 2, TM_FFN,
                        norm_args=(x1, mod, ln_g, ln_b, l))


def kernel(x_prompt, x_sample, c, cache_k, cache_v, state_hgrn, c_ctx, w_mod, b_mod, w_in, w_fourier, lb_logits, hgrn_norm, attn_sink, w_out, ln1_g, ln1_b, ln2_g, ln2_b, ffn_w_gate, ffn_w_up, ffn_w_down, router_w, router_b, moe_w_gate, moe_w_up, moe_w_down):
    lb_sm = jax.nn.softmax(lb_logits.astype(F32), axis=0)
    lower_bounds = jnp.clip(jnp.cumsum(lb_sm, axis=0) - lb_sm[0], 0.0, 1.0).reshape(DEPTH, 2, D_HGRN)
    gw = jnp.tile(hgrn_norm, (1, N_HGRN_HEADS)).reshape(DEPTH, 1, D_HGRN)
    per_layer = lambda v: v.reshape(DEPTH, 1, D_MODEL)
    ln1_g, ln1_b, ln2_g, ln2_b = per_layer(ln1_g), per_layer(ln1_b), per_layer(ln2_g), per_layer(ln2_b)
    n_moe = moe_w_gate.shape[0]
    router_wp = jnp.zeros((n_moe, D_MODEL, LANES), F32).at[:, :, :N_EXPERTS].set(router_w)
    router_bp = jnp.full((n_moe, 1, LANES), NEG_BIG, F32).at[:, 0, :N_EXPERTS].set(router_b)
    s0_lat = _states_to_kernel_layout(state_hgrn)
    s0_ctx = jnp.zeros((BATCH, 1, 2, HGRN_HEAD, D_HGRN), F32)

    cond = jnp.zeros((COND_PAD, D_MODEL), F32).at[0].set(c_ctx).at[1:N_COND].set(c)
    mod = _modulation(cond, w_mod, b_mod).reshape(DEPTH, COND_PAD, N_MOD, D_MODEL)

    ch_tabs = _channel_tables()
    pos_tabs_ctx = _dft_tables(SEQ)
    pos_tabs_lat = _dft_tables(DEC_SEQ)
    cos_t, sin_t = _rope_tables()
    band_bias = _band_bias()
    kc = cache_k.reshape(DEC_BATCH, DEPTH, PAST_LEN, D_KV)
    vc = cache_v.reshape(DEC_BATCH, DEPTH, PAST_LEN, D_KV)
    moe_wg = moe_w_gate.reshape(n_moe * N_EXPERTS, D_MODEL, D_FF_EXPERT)
    moe_wu = moe_w_up.reshape(n_moe * N_EXPERTS, D_MODEL, D_FF_EXPERT)
    moe_wd = moe_w_down.reshape(n_moe * N_EXPERTS, D_FF_EXPERT, D_MODEL)

    x = (x_prompt.reshape(TP, D_MODEL), x_sample.reshape(TS, D_MODEL))
    new_kv, new_s = [], []
    for l in range(DEPTH):
        y_in = _in_projection(x, mod, w_in, l)

        a_ctx = _fourier_mix(y_in, 0, BATCH, SEQ, ch_tabs, pos_tabs_ctx, w_fourier, l)
        a_lat = _fourier_mix(y_in, TP, DEC_BATCH, DEC_SEQ, ch_tabs, pos_tabs_lat, w_fourier, l)
        b_ctx, s_ctx = _hgrn_mix(y_in, 0, BATCH, SEQ, lower_bounds, gw, s0_ctx, l, 0)
        b_lat, _ = _hgrn_mix(y_in, TP, DEC_BATCH, DEC_SEQ, lower_bounds, gw, s0_lat, l, l)
        c_ctx_out = _context_attention(y_in, attn_sink, l)
        c_lat = _latent_attention(y_in, attn_sink, kc, vc, l, cos_t, sin_t, band_bias)

        k0 = 6 * 256 + D_ATTN
        new_kv.append(y_in[:TP, k0:k0 + 2 * D_KV])
        new_s.append(s_ctx)

        mixed = ((a_ctx, a_lat), (b_ctx, b_lat), (c_ctx_out, c_lat))
        i = l // 2
        if l % 2 == 0:
            x1, h2 = _out_projection(mixed, x, mod, ln1_g, ln1_b, w_out, l)
            x = _dense_ffn(h2, x1, mod, ln2_g, ln2_b, l, ffn_w_gate, ffn_w_up, ffn_w_down, i)
        else:
            x1, h2, idx, p = _out_projection(mixed, x, mod, ln1_g, ln1_b, w_out, l,
                                             router=(router_wp, router_bp, i))
            x = _moe_ffn(h2, x1, idx, p, mod, ln2_g, ln2_b, l, moe_wg, moe_wu, moe_wd, i * N_EXPERTS)

    xp = x[:TP].reshape(BATCH, SEQ, D_MODEL)
    xs = x[TP:].reshape(DEC_BATCH, DEC_SEQ, D_MODEL)
    kv = jnp.stack(new_kv, axis=0).reshape(DEPTH, BATCH, SEQ, 2, N_KV_HEADS, HEAD_DIM)
    kv = jnp.transpose(kv, (3, 1, 0, 2, 4, 5))
    states = _states_from_kernel_layout(jnp.stack(new_s, axis=1))
    return (xp, xs, kv[0], kv[1], states)
```

```python
import functools
import math

import jax
import jax.numpy as jnp
import numpy as np
from jax import lax
from jax.experimental import pallas as pl
from jax.experimental.pallas import tpu as pltpu

D_MODEL = 1024
BATCH = 16
SEQ = 256
DEPTH = 4
DEC_BATCH = 2
DEC_SEQ = 2048
PAST_LEN = 512
GRID_W = 64
D_FOURIER = 256
N_FOURIER_GROUPS = 4
FOURIER_GROUP = D_FOURIER // N_FOURIER_GROUPS
D_HGRN = 256
N_HGRN_HEADS = 4
HGRN_HEAD = D_HGRN // N_HGRN_HEADS
HGRN_CHUNK = 32
HGRN_GROUP = 128
HGRN_STATE_BLOCK = 8
HGRN_SAFE_DECAY = 120.0
N_Q_HEADS = 8
N_KV_HEADS = 2
GQA = N_Q_HEADS // N_KV_HEADS
HEAD_DIM = 64
D_ATTN = N_Q_HEADS * HEAD_DIM
D_KV = N_KV_HEADS * HEAD_DIM
D_MIX = D_FOURIER + D_HGRN + D_ATTN
WINDOW = 128
ATTN_BLOCK = 128
ATTN_SCALE = HEAD_DIM ** -0.5
LATENT_HEAD_STACK = 4
ROPE_BASE = 10000.0
NEG_BIG = -1e30
D_FF = 2816
N_EXPERTS = 8
TOP_K = 2
D_FF_EXPERT = 3584
DEEPNORM_ALPHA = (2 * DEPTH) ** 0.25
LN_EPS = 1e-5
RMS_EPS = 1e-6
N_MOD = 6
D_IN = 6 * 256 + D_ATTN + 2 * D_KV

TP = BATCH * SEQ
TS = DEC_BATCH * DEC_SEQ
T = TP + TS
N_COND = 1 + DEC_BATCH
COND_PAD = 8

TM = 512
TM_FFN = 512
TM_MOE = 1024
FFN_ROW_STEP = 256
TM_LN = 256
MXU_COLS = 256
ROW_CHUNK = 256
LANES = 128
VMEM_LIMIT = 56 * 1024 * 1024

F32 = jnp.float32
BF16 = jnp.bfloat16


def _cparams(sem, vmem=VMEM_LIMIT):
    return pltpu.CompilerParams(dimension_semantics=sem, vmem_limit_bytes=vmem)


def _ln(x):
    mu = jnp.mean(x, axis=-1, keepdims=True)
    xc = x - mu
    var = jnp.mean(xc * xc, axis=-1, keepdims=True)
    return xc * lax.rsqrt(var + LN_EPS)


def _silu(x):
    return x * jax.nn.sigmoid(x)


def _split3(a):
    p0 = a.astype(BF16)
    r1 = a - p0.astype(F32)
    p1 = r1.astype(BF16)
    r2 = r1 - p1.astype(F32)
    return p0, p1, r2.astype(BF16)


def _split2(a):
    hi = a.astype(BF16)
    return hi, (a - hi.astype(F32)).astype(BF16)


def _dot(a, b):
    return jnp.dot(a, b, preferred_element_type=F32)


def _dot_nt(a, b):
    return lax.dot_general(a, b, (((1,), (1,)), ((), ())), preferred_element_type=F32)


def _dot_tn(a, b):
    return lax.dot_general(a, b, (((0,), (0,)), ((), ())), preferred_element_type=F32)


def _cond_of_tile(i, tm):
    n_ctx = TP // tm
    return jnp.where(i < n_ctx, 0, 1 + (i - n_ctx) // (DEC_SEQ // tm))


def _mod_spec(l, tm, tile_of=lambda i, *_: i):
    return pl.BlockSpec((None, None, N_MOD, D_MODEL), lambda *a: (l, _cond_of_tile(tile_of(*a), tm), 0, 0))


def _layer_row_spec(l, width):
    return pl.BlockSpec((None, 1, width), lambda *_: (l, 0, 0))


def _mod_kernel(c_ref, w_ref, b_ref, o_ref):
    a = _silu(c_ref[...]).astype(BF16)
    o_ref[...] = _dot(a, w_ref[...].astype(BF16)) + b_ref[...]


def _modulation(cond, w_mod, b_mod):
    tn = 2048
    n_out = N_MOD * D_MODEL
    return pl.pallas_call(
        _mod_kernel,
        grid=(DEPTH, n_out // tn),
        in_specs=[
            pl.BlockSpec((COND_PAD, D_MODEL), lambda l, j: (0, 0)),
            pl.BlockSpec((None, D_MODEL, tn), lambda l, j: (l, 0, j)),
            pl.BlockSpec((None, 1, tn), lambda l, j: (l, 0, j)),
        ],
        out_specs=pl.BlockSpec((None, COND_PAD, tn), lambda l, j: (l, 0, j)),
        out_shape=jax.ShapeDtypeStruct((DEPTH, COND_PAD, n_out), F32),
        compiler_params=_cparams(("parallel", "parallel")),
        name="modulation",
    )(cond, w_mod, b_mod.reshape(DEPTH, 1, n_out))


def _inproj_kernel(*refs, split):
    if split:
        x0_ref, x1_ref, mod_ref, w_ref, o_ref, wb_ref = refs
        is_ctx = pl.program_id(0) < TP // TM
        load = lambda rows: jnp.where(is_ctx, x0_ref[rows, :], x1_ref[rows, :])
    else:
        x_ref, mod_ref, w_ref, o_ref, wb_ref = refs
        load = lambda rows: x_ref[rows, :]

    @pl.when(pl.program_id(0) == 0)
    def _():
        wb_ref[...] = w_ref[...].astype(BF16)

    for r0 in range(0, TM, ROW_CHUNK):
        rows = slice(r0, r0 + ROW_CHUNK)
        h = _ln(load(rows)) * (1.0 + mod_ref[1:2, :]) + mod_ref[0:1, :]
        o_ref[rows, :] = _dot(h.astype(BF16), wb_ref[...])


def _in_projection(x, mod, w_in, l):
    split = isinstance(x, tuple)
    n_ctx = TP // TM
    if split:
        x_specs = [pl.BlockSpec((TM, D_MODEL), lambda i: (jnp.minimum(i, n_ctx - 1), 0)),
                   pl.BlockSpec((TM, D_MODEL), lambda i: (jnp.maximum(i - n_ctx, 0), 0))]
    else:
        x_specs = [pl.BlockSpec((TM, D_MODEL), lambda i: (i, 0))]
        x = (x,)
    return pl.pallas_call(
        functools.partial(_inproj_kernel, split=split),
        grid=(T // TM,),
        in_specs=x_specs + [
            _mod_spec(l, TM),
            pl.BlockSpec((None, D_MODEL, D_IN), lambda i: (l, 0, 0), pipeline_mode=pl.Buffered(1)),
        ],
        out_specs=pl.BlockSpec((TM, D_IN), lambda i: (i, 0)),
        out_shape=jax.ShapeDtypeStruct((T, D_IN), F32),
        scratch_shapes=[pltpu.VMEM((D_MODEL, D_IN), BF16)],
        compiler_params=_cparams(("arbitrary",)),
        name="in_projection",
    )(*x, mod, w_in)


def _fourier_kernel(u_ref, ch_hi_ref, ch_lo_ref, tab_ref, wf_ref, o_ref, ab_ref, *, scale):
    @pl.when(pl.program_id(1) == 0)
    def _():
        uh, ul = _split2(u_ref[...])
        ch_hi = ch_hi_ref[...]
        ab = _dot(uh, ch_hi) + _dot(uh, ch_lo_ref[...]) + _dot(ul, ch_hi)
        stacked = jnp.concatenate([ab[:, :D_FOURIER], ab[:, D_FOURIER:]], axis=0)
        ab_ref[...] = stacked.astype(BF16)

    z = (_dot(tab_ref[...], ab_ref[...]) * scale).astype(BF16)
    o_ref[...] = _dot(z, wf_ref[...].astype(BF16)).astype(BF16)


def _dft_tables(n):
    blk = 32
    t = jnp.arange(n, dtype=jnp.int32)[None, :]
    ang1 = (((jnp.arange(n // blk, dtype=jnp.int32) * blk)[:, None] * t) % n).astype(F32) * (2.0 * math.pi / n)
    ang0 = ((jnp.arange(blk, dtype=jnp.int32)[:, None] * t) % n).astype(F32) * (2.0 * math.pi / n)
    c1, s1 = jnp.cos(ang1)[:, None, :], jnp.sin(ang1)[:, None, :]
    c0, s0 = jnp.cos(ang0)[None, :, :], jnp.sin(ang0)[None, :, :]
    cos = (c1 * c0 - s1 * s0).reshape(n, n)
    sin = (s1 * c0 + c1 * s0).reshape(n, n)
    return jnp.concatenate([cos, -sin], axis=1).astype(BF16)


def _channel_tables():
    a = jnp.arange(D_FOURIER, dtype=jnp.int32)
    same = (a[:, None] // FOURIER_GROUP) == (a[None, :] // FOURIER_GROUP)
    prod = ((a[:, None] % FOURIER_GROUP) * (a[None, :] % FOURIER_GROUP)) % FOURIER_GROUP
    ang = prod.astype(F32) * (2.0 * math.pi / FOURIER_GROUP)
    c = jnp.where(same, jnp.cos(ang), 0.0)
    s = jnp.where(same, jnp.sin(ang), 0.0)
    return _split2(jnp.concatenate([c, s], axis=1))


def _fourier_mix(y_in, row0, nbatch, length, ch_tabs, pos_tabs, w_fourier, l):
    tr = min(length, 256)
    blk0 = row0 // length
    kern = functools.partial(_fourier_kernel, scale=1.0 / math.sqrt(length * FOURIER_GROUP))
    return pl.pallas_call(
        kern,
        grid=(nbatch, length // tr),
        in_specs=[
            pl.BlockSpec((length, D_FOURIER), lambda b, r: (blk0 + b, 0)),
            pl.BlockSpec((D_FOURIER, 2 * D_FOURIER), lambda b, r: (0, 0)),
            pl.BlockSpec((D_FOURIER, 2 * D_FOURIER), lambda b, r: (0, 0)),
            pl.BlockSpec((tr, 2 * length), lambda b, r: (r, 0)),
            pl.BlockSpec((None, D_FOURIER, D_FOURIER), lambda b, r: (l, 0, 0)),
        ],
        out_specs=pl.BlockSpec((tr, D_FOURIER), lambda b, r: (b * (length // tr) + r, 0)),
        out_shape=jax.ShapeDtypeStruct((nbatch * length, D_FOURIER), BF16),
        scratch_shapes=[pltpu.VMEM((2 * length, D_FOURIER), BF16)],
        compiler_params=_cparams(("parallel", "arbitrary")),
        name=f"fourier_mix_{length}",
    )(y_in, ch_tabs[0], ch_tabs[1], pos_tabs, w_fourier)


def _expand_state(cst):
    full = jnp.concatenate([cst] * N_HGRN_HEADS, axis=0)
    r = lax.broadcasted_iota(jnp.int32, full.shape, 0) // HGRN_HEAD
    c = lax.broadcasted_iota(jnp.int32, full.shape, 1) // HGRN_HEAD
    return jnp.where(r == c, full, jnp.zeros_like(full))


def _compress_state(full):
    lane_head = lax.broadcasted_iota(jnp.int32, (HGRN_HEAD, D_HGRN), 1) // HGRN_HEAD
    out = jnp.zeros((HGRN_HEAD, D_HGRN), full.dtype)
    for h in range(N_HGRN_HEADS):
        out = out + jnp.where(lane_head == h, full[h * HGRN_HEAD:(h + 1) * HGRN_HEAD, :], 0.0)
    return out


def _hgrn_pairwise_scan(d, nc, g_scr, k_scr, q_scr, v_ref, o_scr, st_scr, ones_bd):
    c = HGRN_CHUNK
    last = c - 1 if d == 0 else 0

    def chunk(ci, carry):
        row = lax.broadcasted_iota(jnp.int32, (c, c), 0)
        col = lax.broadcasted_iota(jnp.int32, (c, c), 1)
        tri = jnp.where((col <= row) if d == 0 else (col >= row), 1.0, 0.0).astype(BF16)
        cc = ci if d == 0 else nc - 1 - ci
        rows = pl.ds(pl.multiple_of(cc * c, c), c)
        g = g_scr[rows, :]
        kk = k_scr[rows, :]
        q = q_scr[rows, :]
        v = v_ref[rows, :]
        g0, g1, g2 = _split3(g)
        b = _dot(tri, g0) + _dot(tri, g1) + _dot(tri, g2)
        btot = b[last:last + 1, :]
        st = st_scr[...]
        o_inter = _dot_nt((q * jnp.exp(b)).astype(BF16), _expand_state(st.astype(BF16)))
        s_idx = lax.broadcasted_iota(jnp.int32, (c, c, D_HGRN), 0)
        t_idx = lax.broadcasted_iota(jnp.int32, (c, c, D_HGRN), 1)
        causal = (s_idx <= t_idx) if d == 0 else (s_idx >= t_idx)
        diff = b[None, :, :] - b[:, None, :]
        pair = q[None, :, :] * jnp.exp(jnp.minimum(diff, 0.0)) * kk[:, None, :]
        pair = jnp.where(causal, pair, 0.0).astype(BF16)
        attn = _dot(pair.reshape(c * c, D_HGRN), ones_bd).reshape(c, c, D_HGRN)
        o = o_inter + jnp.sum(attn * v[:, None, :], axis=0)
        if d == 0:
            o_scr[rows, :] = o
        else:
            o_scr[rows, :] = o_scr[rows, :] + o
        kd = kk * jnp.exp(btot - b)
        upd = _dot_tn(v.astype(BF16), kd.astype(BF16))
        st_scr[...] = jnp.exp(btot) * st + _compress_state(upd)
        return carry

    lax.fori_loop(0, nc, chunk, 0)


def _hgrn_factored_scan(d, nc, g_scr, k_scr, q_scr, v_ref, o_scr, st_scr, qd_scr, kd_scr, dec_scr, stb_scr):
    c = HGRN_CHUNK
    grp = HGRN_GROUP
    length = nc * c

    def group(gi, carry):
        rows = pl.ds(pl.multiple_of(gi * grp, grp), grp)
        r = lax.broadcasted_iota(jnp.int32, (grp, grp), 0)
        s = lax.broadcasted_iota(jnp.int32, (grp, grp), 1)
        same_chunk = (r // c) == (s // c)
        ordered = (s <= r) if d == 0 else (s >= r)
        tri = jnp.where(same_chunk & ordered, 1.0, 0.0).astype(BF16)
        ones_chunk = jnp.where(same_chunk, 1.0, 0.0).astype(BF16)
        g0, g1, g2 = _split3(g_scr[rows, :])
        b = _dot(tri, g0) + _dot(tri, g1) + _dot(tri, g2)
        tot = _dot(ones_chunk, g0) + _dot(ones_chunk, g1) + _dot(ones_chunk, g2)
        kk = k_scr[rows, :]
        q = q_scr[rows, :]
        half = 0.5 * tot
        qc = (q * jnp.exp(b - half)).astype(BF16)
        kh = (kk * jnp.exp(half - b)).astype(BF16)
        qd_scr[rows, :] = (q * jnp.exp(b)).astype(BF16)
        kd_scr[rows, :] = (kk * jnp.exp(tot - b)).astype(BF16)
        dec_scr[rows, :] = jnp.exp(tot)
        hs_row = lax.broadcasted_iota(jnp.int32, (N_HGRN_HEADS * grp, D_HGRN), 0) // grp
        hs_col = lax.broadcasted_iota(jnp.int32, (N_HGRN_HEADS * grp, D_HGRN), 1) // HGRN_HEAD
        same_head = hs_row == hs_col
        k_bd = jnp.where(same_head, jnp.concatenate([kh] * N_HGRN_HEADS, axis=0), 0.0)
        v_bd = jnp.where(same_head, jnp.concatenate([v_ref[rows, :].astype(BF16)] * N_HGRN_HEADS, axis=0), 0.0)
        t_idx = lax.broadcasted_iota(jnp.int32, (grp, N_HGRN_HEADS * grp), 0)
        s_idx = lax.broadcasted_iota(jnp.int32, (grp, N_HGRN_HEADS * grp), 1) % grp
        keep = ((t_idx // c) == (s_idx // c)) & ((s_idx <= t_idx) if d == 0 else (s_idx >= t_idx))
        attn = jnp.where(keep, _dot_nt(qc, k_bd), 0.0)
        o_intra = _dot(attn.astype(BF16), v_bd)
        if d == 0:
            o_scr[rows, :] = o_intra
        else:
            o_scr[rows, :] = o_scr[rows, :] + o_intra
        return carry

    lax.fori_loop(0, length // grp, group, 0, unroll=min(4, length // grp))

    def state_step(ci, carry):
        cc = ci if d == 0 else nc - 1 - ci
        r0 = pl.multiple_of(cc * c, c)
        rows = pl.ds(r0, c)
        st = st_scr[...]
        stb_scr[cc] = st.astype(BF16)
        upd = _dot_tn(v_ref[rows, :].astype(BF16), kd_scr[rows, :])
        st_scr[...] = dec_scr[pl.ds(r0, 1), :] * st + _compress_state(upd)
        return carry

    lax.fori_loop(0, nc, state_step, 0, unroll=4)

    nb = HGRN_STATE_BLOCK
    blk_rows = nb * c

    def inter(bi, carry):
        rows = pl.ds(pl.multiple_of(bi * blk_rows, blk_rows), blk_rows)
        row_chunk = lax.broadcasted_iota(jnp.int32, (blk_rows, nb * D_HGRN), 0) // c
        col_chunk = lax.broadcasted_iota(jnp.int32, (blk_rows, nb * D_HGRN), 1) // D_HGRN
        q_bd = jnp.where(row_chunk == col_chunk, jnp.concatenate([qd_scr[rows, :]] * nb, axis=1), 0.0)
        states = jnp.concatenate([_expand_state(stb_scr[bi * nb + j]) for j in range(nb)], axis=1)
        o_scr[rows, :] = o_scr[rows, :] + _dot_nt(q_bd, states)
        return carry

    lax.fori_loop(0, nc // nb, inter, 0)


def _hgrn_kernel(hq_ref, hff_ref, hfb_ref, hi_ref, hg_ref, lb_ref, gw_ref, s0_ref,
                 o_ref, sfin_ref, q_scr, g_scr, k_scr, o_scr, dec_scr, qd_scr, kd_scr, st_scr, stb_scr,
                 *, length):
    c = HGRN_CHUNK
    nc = length // c
    q_scr[...] = _silu(hq_ref[...])
    r256 = lax.broadcasted_iota(jnp.int32, (D_HGRN, D_HGRN), 0) // HGRN_HEAD
    c256 = lax.broadcasted_iota(jnp.int32, (D_HGRN, D_HGRN), 1) // HGRN_HEAD
    ones_bd = jnp.where(r256 == c256, 1.0, 0.0).astype(BF16)

    for d in range(2):
        z_ref = hff_ref if d == 0 else hfb_ref
        one_minus_f = (1.0 - lb_ref[d:d + 1, :]) * jax.nn.sigmoid(-z_ref[...])
        g_scr[...] = jnp.log1p(-one_minus_f)
        k_scr[...] = one_minus_f
        st_scr[...] = s0_ref[d]
        chunk_decay = jnp.sum(g_scr[...].reshape(nc, c, D_HGRN), axis=1)
        safe = jnp.min(chunk_decay) >= -HGRN_SAFE_DECAY
        lax.cond(
            safe,
            functools.partial(_hgrn_factored_scan, d, nc, g_scr, k_scr, q_scr, hi_ref, o_scr, st_scr,
                              qd_scr, kd_scr, dec_scr, stb_scr),
            functools.partial(_hgrn_pairwise_scan, d, nc, g_scr, k_scr, q_scr, hi_ref, o_scr, st_scr, ones_bd))
        sfin_ref[d] = st_scr[...]

    o = o_scr[...]
    s0p, s1p, s2p = _split3(o * o)
    ms = (_dot(s0p, ones_bd) + _dot(s1p, ones_bd) + _dot(s2p, ones_bd)) * (1.0 / HGRN_HEAD)
    y = o * lax.rsqrt(ms + RMS_EPS) * gw_ref[...] * _silu(hg_ref[...])
    o_ref[...] = y.astype(BF16)


def _hgrn_mix(y_in, row0, nbatch, length, lb, gw, s0, l, s0_layer):
    blk0 = row0 // length
    col = lambda j: (lambda b: (blk0 + b, j))
    kern = functools.partial(_hgrn_kernel, length=length)
    return pl.pallas_call(
        kern,
        grid=(nbatch,),
        in_specs=[
            pl.BlockSpec((length, D_HGRN), col(1)),
            pl.BlockSpec((length, D_HGRN), col(2)),
            pl.BlockSpec((length, D_HGRN), col(3)),
            pl.BlockSpec((length, D_HGRN), col(4)),
            pl.BlockSpec((length, D_HGRN), col(5)),
            pl.BlockSpec((None, 2, D_HGRN), lambda b: (l, 0, 0)),
            _layer_row_spec(l, D_HGRN),
            pl.BlockSpec((None, None, 2, HGRN_HEAD, D_HGRN), lambda b: (b, s0_layer, 0, 0, 0)),
        ],
        out_specs=[
            pl.BlockSpec((length, D_HGRN), lambda b: (b, 0)),
            pl.BlockSpec((None, 2, HGRN_HEAD, D_HGRN), lambda b: (b, 0, 0, 0)),
        ],
        out_shape=[
            jax.ShapeDtypeStruct((nbatch * length, D_HGRN), BF16),
            jax.ShapeDtypeStruct((nbatch, 2, HGRN_HEAD, D_HGRN), F32),
        ],
        scratch_shapes=[pltpu.VMEM((length, D_HGRN), F32) for _ in range(5)]
        + [pltpu.VMEM((length, D_HGRN), BF16) for _ in range(2)]
        + [pltpu.VMEM((HGRN_HEAD, D_HGRN), F32), pltpu.VMEM((length // HGRN_CHUNK, HGRN_HEAD, D_HGRN), BF16)],
        compiler_params=_cparams(("parallel",)),
        name=f"hgrn_mix_{length}",
    )(y_in, y_in, y_in, y_in, y_in, lb, gw, s0)


def _states_to_kernel_layout(s):
    return jnp.moveaxis(s, -1, -3).reshape(s.shape[:-3] + (HGRN_HEAD, D_HGRN))


def _states_from_kernel_layout(st):
    return jnp.moveaxis(st.reshape(st.shape[:-1] + (N_HGRN_HEADS, HGRN_HEAD)), -3, -1)


def _group_attention(q_heads, sinks, kv_parts):
    rows = q_heads[0].shape[0]
    qg = jnp.concatenate([(q * ATTN_SCALE).astype(BF16) for q in q_heads], axis=0)
    head = lax.broadcasted_iota(jnp.int32, (len(q_heads) * rows, 1), 0) // rows
    sink = jnp.full(head.shape, sinks[0], F32)
    for i in range(1, len(q_heads)):
        sink = jnp.where(head == i, sinks[i], sink)
    m = sink
    scores = []
    for k, _, bias in kv_parts:
        s = _dot_nt(qg, k)
        if bias is not None:
            s = s + bias
        m = jnp.maximum(m, jnp.max(s, axis=1, keepdims=True))
        scores.append(s)
    den = jnp.exp(sink - m)
    out = None
    for s, (_, v, _) in zip(scores, kv_parts):
        e = jnp.exp(s - m)
        den = den + jnp.sum(e, axis=1, keepdims=True)
        o = _dot(e.astype(BF16), v)
        out = o if out is None else out + o
    out = out * (1.0 / den)
    return [out[i * rows:(i + 1) * rows, :] for i in range(len(q_heads))]


def _ctx_attn_kernel(sink_ref, q_ref, k_ref, v_ref, o_ref, *, layer):
    outs = []
    for g in range(N_KV_HEADS):
        sl = slice(g * HEAD_DIM, (g + 1) * HEAD_DIM)
        heads = range(g * GQA, (g + 1) * GQA)
        outs += _group_attention(
            [q_ref[:, h * HEAD_DIM:(h + 1) * HEAD_DIM] for h in heads],
            [sink_ref[layer, h] for h in heads],
            [(k_ref[:, sl].astype(BF16), v_ref[:, sl].astype(BF16), None)])
    o_ref[...] = jnp.concatenate(outs, axis=1).astype(BF16)


def _context_attention(y_in, sink, l):
    qcol = (6 * 256) // D_ATTN
    kcol = (6 * 256 + D_ATTN) // D_KV
    return pl.pallas_call(
        functools.partial(_ctx_attn_kernel, layer=l),
        grid=(BATCH,),
        in_specs=[
            pl.BlockSpec(memory_space=pltpu.SMEM),
            pl.BlockSpec((SEQ, D_ATTN), lambda b: (b, qcol)),
            pl.BlockSpec((SEQ, D_KV), lambda b: (b, kcol)),
            pl.BlockSpec((SEQ, D_KV), lambda b: (b, kcol + 1)),
        ],
        out_specs=pl.BlockSpec((SEQ, D_ATTN), lambda b: (b, 0)),
        out_shape=jax.ShapeDtypeStruct((TP, D_ATTN), BF16),
        compiler_params=_cparams(("parallel",)),
        name="context_attention",
    )(sink, y_in, y_in, y_in)


def _rope(x, cos, sin):
    lane = lax.broadcasted_iota(jnp.int32, x.shape, 1)
    n_freq = HEAD_DIM // 4
    first = (lane % (2 * n_freq)) < n_freq
    swapped = jnp.where(first, pltpu.roll(x, LANES - n_freq, axis=1), pltpu.roll(x, n_freq, axis=1))
    return x * cos + swapped * sin


def _lat_attn_kernel(sink_ref, q_ref, k_ref, v_ref, kc_ref, vc_ref, cos_ref, sin_ref, bias_ref, o_ref, *, layer):
    j = pl.program_id(1)
    nb = DEC_SEQ // ATTN_BLOCK
    blk = ATTN_BLOCK
    q0 = pl.multiple_of(j * blk, blk)
    cos_q = cos_ref[pl.ds(q0, blk), :]
    sin_q = sin_ref[pl.ds(q0, blk), :]

    starts = [jnp.maximum(j - 1, 0), j, jnp.minimum(j + 1, nb - 1)]
    k_band, v_band = [], []
    for st in starts:
        r0 = pl.multiple_of(st * blk, blk)
        kb = _rope(k_ref[pl.ds(r0, blk), :], cos_ref[pl.ds(r0, blk), :], sin_ref[pl.ds(r0, blk), :])
        k_band.append(kb)
        v_band.append(v_ref[pl.ds(r0, blk), :])
    k_loc = jnp.concatenate(k_band, axis=0)
    v_loc = jnp.concatenate(v_band, axis=0)
    band_bias = bias_ref[...]

    heads_per_chunk = LANES // HEAD_DIM
    q_chunks = [_rope(q_ref[:, cg * LANES:(cg + 1) * LANES], cos_q, sin_q)
                for cg in range(N_Q_HEADS // heads_per_chunk)]

    def q_head(h):
        off = (h % heads_per_chunk) * HEAD_DIM
        return q_chunks[h // heads_per_chunk][:, off:off + HEAD_DIM]

    outs = []
    for h0 in range(0, N_Q_HEADS, LATENT_HEAD_STACK):
        g = h0 // GQA
        sl = slice(g * HEAD_DIM, (g + 1) * HEAD_DIM)
        heads = range(h0, h0 + LATENT_HEAD_STACK)
        outs += _group_attention(
            [q_head(h) for h in heads],
            [sink_ref[layer, h] for h in heads],
            [(k_loc[:, sl].astype(BF16), v_loc[:, sl].astype(BF16), band_bias[:LATENT_HEAD_STACK * blk, :]),
             (kc_ref[:, sl].astype(BF16), vc_ref[:, sl].astype(BF16), None)])
    o_ref[...] = jnp.concatenate(outs, axis=1).astype(BF16)


def _band_bias():
    blk = ATTN_BLOCK
    r = np.arange(GQA * blk)[:, None] % blk
    c = np.arange(3 * blk)[None, :]
    in_window = np.abs(r + blk - c) <= WINDOW
    exists = [c >= blk, c >= 0, c < 2 * blk]
    return jnp.asarray(np.stack([np.where(in_window & e, 0.0, NEG_BIG) for e in exists]).astype(np.float32))


def _latent_attention(y_in, sink, kc, vc, l, cos_t, sin_t, band_bias):
    nb = DEC_SEQ // ATTN_BLOCK
    qrow0 = TP // ATTN_BLOCK
    krow0 = TP // DEC_SEQ
    qcol = (6 * 256) // D_ATTN
    kcol = (6 * 256 + D_ATTN) // D_KV
    return pl.pallas_call(
        functools.partial(_lat_attn_kernel, layer=l),
        grid=(DEC_BATCH, nb),
        in_specs=[
            pl.BlockSpec(memory_space=pltpu.SMEM),
            pl.BlockSpec((ATTN_BLOCK, D_ATTN), lambda b, j: (qrow0 + b * nb + j, qcol)),
            pl.BlockSpec((DEC_SEQ, D_KV), lambda b, j: (krow0 + b, kcol)),
            pl.BlockSpec((DEC_SEQ, D_KV), lambda b, j: (krow0 + b, kcol + 1)),
            pl.BlockSpec((None, None, PAST_LEN, D_KV), lambda b, j: (b, l, 0, 0)),
            pl.BlockSpec((None, None, PAST_LEN, D_KV), lambda b, j: (b, l, 0, 0)),
            pl.BlockSpec((DEC_SEQ, LANES), lambda b, j: (0, 0)),
            pl.BlockSpec((DEC_SEQ, LANES), lambda b, j: (0, 0)),
            pl.BlockSpec((None, GQA * ATTN_BLOCK, 3 * ATTN_BLOCK),
                         lambda b, j: (jnp.where(j == 0, 0, jnp.where(j == nb - 1, 2, 1)), 0, 0)),
        ],
        out_specs=pl.BlockSpec((ATTN_BLOCK, D_ATTN), lambda b, j: (b * nb + j, 0)),
        out_shape=jax.ShapeDtypeStruct((TS, D_ATTN), BF16),
        compiler_params=_cparams(("parallel", "parallel")),
        name="latent_attention",
    )(sink, y_in, y_in, y_in, kc, vc, cos_t, sin_t, band_bias)


def _rope_tables():
    t = jnp.arange(DEC_SEQ)
    rows = (t // GRID_W).astype(F32)
    cols = (t % GRID_W).astype(F32)
    n_freq = HEAD_DIM // 4
    inv = ROPE_BASE ** (-jnp.arange(n_freq, dtype=F32) / n_freq)
    ar = rows[:, None] * inv
    ac = cols[:, None] * inv
    cos = jnp.concatenate([jnp.cos(ar), jnp.cos(ar), jnp.cos(ac), jnp.cos(ac)], axis=1)
    sin = jnp.concatenate([-jnp.sin(ar), jnp.sin(ar), -jnp.sin(ac), jnp.sin(ac)], axis=1)
    return jnp.tile(cos, (1, N_KV_HEADS)), jnp.tile(sin, (1, N_KV_HEADS))


def _top2_route(h2, w_ref, b_ref):
    logits = _dot(h2.astype(BF16), w_ref[...].astype(BF16)) + b_ref[...]
    lane = lax.broadcasted_iota(jnp.int32, logits.shape, 1)
    m1 = jnp.max(logits, axis=1, keepdims=True)
    i1 = jnp.min(jnp.where(logits == m1, lane, LANES), axis=1, keepdims=True)
    rest = jnp.where(lane == i1, -jnp.inf, logits)
    m2 = jnp.max(rest, axis=1, keepdims=True)
    i2 = jnp.min(jnp.where(rest == m2, lane, LANES), axis=1, keepdims=True)
    e = jnp.exp(m2 - m1)
    inv = 1.0 / (1.0 + e)
    idx = jnp.where(lane == 0, i1, jnp.where(lane == 1, i2, 0))
    return idx, jnp.where(lane == 0, inv, jnp.where(lane == 1, e * inv, 0.0))


def _outproj_kernel(a0_ref, a1_ref, b0_ref, b1_ref, c0_ref, c1_ref, *rest, route, split):
    is_ctx = pl.program_id(0) < TP // TM
    pick = lambda r0, r1, rows: jnp.where(is_ctx, r0[rows, :], r1[rows, :])
    if split:
        load_x = functools.partial(pick, rest[0], rest[1])
        rest = rest[2:]
    else:
        load_x = lambda rows, x_ref=rest[0]: x_ref[rows, :]
        rest = rest[1:]
    mod_ref, g_ref, beta_ref, w_ref = rest[:4]
    rest = rest[4:]
    if route:
        wr_ref, br_ref, x1_ref, h2_ref, idx_ref, p_ref, wb_ref = rest
    else:
        x1_ref, h2_ref, wb_ref = rest

    @pl.when(pl.program_id(0) == 0)
    def _():
        wb_ref[...] = w_ref[...].astype(BF16)

    for r0 in range(0, TM, ROW_CHUNK):
        rows = slice(r0, r0 + ROW_CHUNK)
        y = (_dot(pick(a0_ref, a1_ref, rows), wb_ref[0:D_FOURIER, :])
             + _dot(pick(b0_ref, b1_ref, rows), wb_ref[D_FOURIER:D_FOURIER + D_HGRN, :])
             + _dot(pick(c0_ref, c1_ref, rows), wb_ref[D_FOURIER + D_HGRN:D_MIX, :]))
        x1 = _ln(DEEPNORM_ALPHA * load_x(rows) + mod_ref[2:3, :] * y) * g_ref[...] + beta_ref[...]
        x1_ref[rows, :] = x1
        h2 = _ln(x1) * (1.0 + mod_ref[4:5, :]) + mod_ref[3:4, :]
        h2_ref[rows, :] = h2
        if route:
            idx_ref[rows, :], p_ref[rows, :] = _top2_route(h2, wr_ref, br_ref)


def _out_projection(mixed, x, mod, ln_g, ln_b, w_out, l, router=None):
    n_ctx = TP // TM
    row = lambda w: pl.BlockSpec((TM, w), lambda i: (i, 0))
    ctx = lambda w: pl.BlockSpec((TM, w), lambda i: (jnp.minimum(i, n_ctx - 1), 0))
    lat = lambda w: pl.BlockSpec((TM, w), lambda i: (jnp.maximum(i - n_ctx, 0), 0))
    vec = _layer_row_spec(l, D_MODEL)
    split = isinstance(x, tuple)
    x_specs = [ctx(D_MODEL), lat(D_MODEL)] if split else [row(D_MODEL)]
    x = x if split else (x,)
    in_specs = [
        ctx(D_FOURIER), lat(D_FOURIER), ctx(D_HGRN), lat(D_HGRN), ctx(D_ATTN), lat(D_ATTN), *x_specs,
        _mod_spec(l, TM),
        vec, vec,
        pl.BlockSpec((None, D_MIX, D_MODEL), lambda i: (l, 0, 0), pipeline_mode=pl.Buffered(1)),
    ]
    args = [mixed[0][0], mixed[0][1], mixed[1][0], mixed[1][1], mixed[2][0], mixed[2][1], *x, mod, ln_g, ln_b,
            w_out]
    out_specs = [row(D_MODEL), row(D_MODEL)]
    out_shape = [jax.ShapeDtypeStruct((T, D_MODEL), F32), jax.ShapeDtypeStruct((T, D_MODEL), F32)]
    if router is not None:
        w, b, ri = router
        in_specs += [pl.BlockSpec((None, D_MODEL, LANES), lambda i: (ri, 0, 0)), _layer_row_spec(ri, LANES)]
        args += [w, b]
        out_specs += [row(LANES), row(LANES)]
        out_shape += [jax.ShapeDtypeStruct((T, LANES), jnp.int32), jax.ShapeDtypeStruct((T, LANES), F32)]
    return pl.pallas_call(
        functools.partial(_outproj_kernel, route=router is not None, split=split),
        grid=(T // TM,),
        in_specs=in_specs,
        out_specs=out_specs,
        out_shape=out_shape,
        scratch_shapes=[pltpu.VMEM((D_MIX, D_MODEL), BF16)],
        compiler_params=_cparams(("arbitrary",)),
        name="out_projection_route" if router is not None else "out_projection",
    )(*args)


def _expert_changed(te_ref, i):
    return (i == 0) | (te_ref[i] != te_ref[jnp.maximum(i - 1, 0)])


def _post_ffn(x1, y, g2, ln_g, ln_b):
    return _ln(DEEPNORM_ALPHA * x1 + g2 * y) * ln_g + ln_b


def _for_used_rows(rows, out_ref, compute):
    tm = out_ref.shape[0]
    for n in range(FFN_ROW_STEP, tm + 1, FFN_ROW_STEP):
        @pl.when((rows > n - FFN_ROW_STEP) & (rows <= n))
        def _(n=n):
            compute(n)
            if n < tm:
                out_ref[n:, :] = jnp.zeros((tm - n, out_ref.shape[1]), out_ref.dtype)

    @pl.when(rows == 0)
    def _():
        out_ref[...] = jnp.zeros_like(out_ref)


def _cast_columns(w_refs, c0, c1):
    parts, base = [], 0
    for w_ref in w_refs:
        width = w_ref.shape[-1]
        lo, hi = max(c0 - base, 0), min(c1 - base, width)
        if lo < hi:
            parts.append(w_ref[:, lo:hi].astype(BF16))
        base += width
    return parts[0] if len(parts) == 1 else jnp.concatenate(parts, axis=1)


def _dot_streamed(x, w_refs):
    total = sum(w_ref.shape[-1] for w_ref in w_refs)
    outs = [_dot(x, _cast_columns(w_refs, c0, min(c0 + MXU_COLS, total))) for c0 in range(0, total, MXU_COLS)]
    return outs[0] if len(outs) == 1 else jnp.concatenate(outs, axis=1)


def _ffn_up_kernel(te_ref, nv_ref, tr_ref, x_ref, wg_ref, wu_ref, h_ref, *scratch, stream):
    i = pl.program_id(1)
    fh = wg_ref.shape[-1]
    if not stream:
        wb_ref, = scratch

        @pl.when(_expert_changed(te_ref, i))
        def _():
            wb_ref[:, :fh] = wg_ref[...].astype(BF16)
            wb_ref[:, fh:] = wu_ref[...].astype(BF16)

    def compute(n):
        x = x_ref[0:n, :].astype(BF16)
        if stream:
            for c0 in range(0, fh, MXU_COLS):
                cols = slice(c0, min(c0 + MXU_COLS, fh))
                a = _dot(x, wg_ref[:, cols].astype(BF16))
                b = _dot(x, wu_ref[:, cols].astype(BF16))
                h_ref[0:n, cols] = (_silu(a) * b).astype(BF16)
        else:
            ab = _dot(x, wb_ref[...])
            h_ref[0:n, :] = (_silu(ab[:, :fh]) * ab[:, fh:]).astype(BF16)

    _for_used_rows(tr_ref[i], h_ref, compute)


def _ffn_down_kernel(te_ref, nv_ref, tr_ref, h_ref, wd_ref, *rest, norm):
    i = pl.program_id(1)
    if norm:
        x1_ref, mod_ref, g_ref, beta_ref, y_ref, wdb_ref = rest

        @pl.when(_expert_changed(te_ref, i))
        def _():
            wdb_ref[...] = wd_ref[...].astype(BF16)

        for r0 in range(0, y_ref.shape[0], ROW_CHUNK):
            rows = slice(r0, r0 + ROW_CHUNK)
            y = _dot(h_ref[rows, :], wdb_ref[...])
            y_ref[rows, :] = _post_ffn(x1_ref[rows, :], y, mod_ref[5:6, :], g_ref[...], beta_ref[...])
    else:
        y_ref, = rest

        def compute(n):
            y_ref[0:n, :] = _dot_streamed(h_ref[0:n, :], (wd_ref,))

        _for_used_rows(tr_ref[i], y_ref, compute)


def _grouped_ffn(x_rows, tile_expert, n_valid, tile_rows, w_gate, w_up, w_down, f_splits, tm, norm_args=None):
    r = x_rows.shape[0]
    nt = r // tm
    f = w_gate.shape[-1]
    fh = f // f_splits
    assert fh * f_splits == f and fh % LANES == 0
    used = lambda i, nv: jnp.minimum(i, nv[0] - 1)
    stream = norm_args is None
    h = pl.pallas_call(
        functools.partial(_ffn_up_kernel, stream=stream),
        grid_spec=pltpu.PrefetchScalarGridSpec(
            num_scalar_prefetch=3,
            grid=(f_splits, nt),
            in_specs=[
                pl.BlockSpec((tm, D_MODEL), lambda j, i, te, nv, tr: (used(i, nv), 0)),
                pl.BlockSpec((None, D_MODEL, fh), lambda j, i, te, nv, tr: (te[i], 0, j)),
                pl.BlockSpec((None, D_MODEL, fh), lambda j, i, te, nv, tr: (te[i], 0, j)),
            ],
            out_specs=pl.BlockSpec((tm, fh), lambda j, i, te, nv, tr: (i, j)),
            scratch_shapes=[] if stream else [pltpu.VMEM((D_MODEL, 2 * fh), BF16)],
        ),
        out_shape=jax.ShapeDtypeStruct((r, f), BF16),
        compiler_params=_cparams(("arbitrary", "arbitrary")),
        name="ffn_up",
    )(tile_expert, n_valid, tile_rows, x_rows, w_gate, w_up)

    d_splits = 1 if norm_args is not None else 2
    dh = D_MODEL // d_splits
    in_specs = [
        pl.BlockSpec((tm, f), lambda j, i, te, nv, tr: (used(i, nv), 0)),
        pl.BlockSpec((None, f, dh), lambda j, i, te, nv, tr: (te[i], 0, j)),
    ]
    args = [tile_expert, n_valid, tile_rows, h, w_down]
    if norm_args is not None:
        assert r == T
        x1, mod, ln_g, ln_b, l = norm_args
        vec = _layer_row_spec(l, D_MODEL)
        in_specs += [
            pl.BlockSpec((tm, D_MODEL), lambda j, i, te, nv, tr: (i, 0)),
            _mod_spec(l, tm, lambda j, i, *_: i),
            vec, vec]
        args += [x1, mod, ln_g, ln_b]
    return pl.pallas_call(
        functools.partial(_ffn_down_kernel, norm=norm_args is not None),
        grid_spec=pltpu.PrefetchScalarGridSpec(
            num_scalar_prefetch=3,
            grid=(d_splits, nt),
            in_specs=in_specs,
            out_specs=pl.BlockSpec((tm, dh), lambda j, i, te, nv, tr: (i, j)),
            scratch_shapes=[] if stream else [pltpu.VMEM((f, dh), BF16)],
        ),
        out_shape=jax.ShapeDtypeStruct((r, D_MODEL), F32),
        compiler_params=_cparams(("arbitrary", "arbitrary")),
        name="ffn_down_norm" if norm_args is not None else "ffn_down",
    )(*args)


def _row_copy(src, dst, s, d, sem):
    return pltpu.make_async_copy(src.at[pl.ds(s, 1), :], dst.at[pl.ds(d, 1), :], sem)


def _dispatch_kernel(dest_ref, last_ref, x_ref, o_hbm, zero_scr, sem, zsem):
    i = pl.program_id(0)

    @pl.when(i == 0)
    def _():
        zero_scr[...] = jnp.zeros_like(zero_scr)

        def fill(tile):
            r0 = pl.multiple_of(tile * TM_MOE, TM_MOE)
            return pltpu.make_async_copy(zero_scr, o_hbm.at[pl.ds(r0, TM_MOE), :], zsem)

        n_tiles = o_hbm.shape[0] // TM_MOE
        min_tiles = (T * TOP_K) // TM_MOE
        jobs = [(last_ref[e] >= 0, last_ref[e]) for e in range(N_EXPERTS)]
        jobs += [(t >= last_ref[N_EXPERTS], t) for t in range(min_tiles, n_tiles)]
        for go, tile in jobs:
            @pl.when(go)
            def _(tile=tile):
                fill(tile).start()
        for go, tile in jobs:
            @pl.when(go)
            def _(tile=tile):
                fill(tile).wait()

    base = i * TM_LN

    def issue(r, carry):
        for k in range(TOP_K):
            _row_copy(x_ref, o_hbm, r, dest_ref[(base + r) * TOP_K + k], sem).start(priority=k % 2)
        return carry

    lax.fori_loop(0, TM_LN, issue, 0, unroll=8)
    for k in range(TOP_K):
        pltpu.make_async_copy(x_ref, o_hbm.at[pl.ds(0, TM_LN), :], sem).wait()


def _dispatch(h2, dest, last_tile, n_rows):
    return pl.pallas_call(
        _dispatch_kernel,
        grid_spec=pltpu.PrefetchScalarGridSpec(
            num_scalar_prefetch=2,
            grid=(T // TM_LN,),
            in_specs=[pl.BlockSpec((TM_LN, D_MODEL), lambda i, d, l: (i, 0))],
            out_specs=pl.BlockSpec(memory_space=pl.ANY),
            scratch_shapes=[pltpu.VMEM((TM_MOE, D_MODEL), F32), pltpu.SemaphoreType.DMA(()),
                            pltpu.SemaphoreType.DMA(())],
        ),
        out_shape=jax.ShapeDtypeStruct((n_rows, D_MODEL), F32),
        compiler_params=_cparams(("arbitrary",)),
        name="dispatch",
    )(dest, last_tile, h2)


def _combine_kernel(pos_ref, x1_ref, p_ref, y_hbm, mod_ref, g_ref, beta_ref, o_ref, buf, sem):
    i = pl.program_id(0)
    n = pl.num_programs(0)

    def fetch(step, slot):
        def issue(r, carry):
            for k in range(TOP_K):
                src = pos_ref[(step * TM_LN + r) * TOP_K + k]
                _row_copy(y_hbm, buf.at[slot, k], src, r, sem.at[slot, k]).start(priority=k % 2)
            return carry

        lax.fori_loop(0, TM_LN, issue, 0, unroll=8)

    @pl.when(i == 0)
    def _():
        fetch(0, 0)

    slot = i % 2

    @pl.when(i + 1 < n)
    def _():
        fetch(i + 1, 1 - slot)

    for k in range(TOP_K):
        pltpu.make_async_copy(y_hbm.at[pl.ds(0, TM_LN), :], buf.at[slot, k], sem.at[slot, k]).wait()
    y = p_ref[:, 0:1] * buf[slot, 0]
    for k in range(1, TOP_K):
        y = y + p_ref[:, k:k + 1] * buf[slot, k]
    o_ref[...] = _post_ffn(x1_ref[...], y, mod_ref[5:6, :], g_ref[...], beta_ref[...])


def _combine_norm(x1, p, y_rows, pos, mod, ln_g, ln_b, l):
    row = pl.BlockSpec((TM_LN, D_MODEL), lambda i, s: (i, 0))
    vec = _layer_row_spec(l, D_MODEL)
    return pl.pallas_call(
        _combine_kernel,
        grid_spec=pltpu.PrefetchScalarGridSpec(
            num_scalar_prefetch=1,
            grid=(T // TM_LN,),
            in_specs=[row, pl.BlockSpec((TM_LN, LANES), lambda i, s: (i, 0)),
                      pl.BlockSpec(memory_space=pl.ANY),
                      _mod_spec(l, TM_LN),
                      vec, vec],
            out_specs=row,
            scratch_shapes=[pltpu.VMEM((2, TOP_K, TM_LN, D_MODEL), F32),
                            pltpu.SemaphoreType.DMA((2, TOP_K))],
        ),
        out_shape=jax.ShapeDtypeStruct((T, D_MODEL), F32),
        compiler_params=_cparams(("arbitrary",)),
        name="combine_norm",
    )(pos, x1, p, y_rows, mod, ln_g, ln_b)


def _moe_plan(idx):
    n_assign = T * TOP_K
    n_tiles = n_assign // TM_MOE + N_EXPERTS
    e = idx[:, :TOP_K].reshape(n_assign)
    onehot = (e[:, None] == jnp.arange(N_EXPERTS, dtype=jnp.int32)[None, :]).astype(jnp.int32)
    csum = jnp.cumsum(onehot, axis=0)
    counts = csum[-1]
    tiles_e = (counts + TM_MOE - 1) // TM_MOE
    tile_end = jnp.cumsum(tiles_e)
    row0 = (tile_end - tiles_e) * TM_MOE
    dest = jnp.sum((csum - 1 + row0[None, :]) * onehot, axis=1)
    n_valid = tile_end[-1]
    tiles = jnp.arange(n_tiles, dtype=jnp.int32)
    tile_id = jnp.minimum(tiles, n_valid - 1)
    tile_expert = jnp.sum((tile_id[:, None] >= tile_end[None, :]).astype(jnp.int32), axis=1)
    tile_start = tile_end - tiles_e
    in_group = (tiles[:, None] >= tile_start[None, :]) & (tiles[:, None] < tile_end[None, :])
    left = counts[None, :] - (tiles[:, None] - tile_start[None, :]) * TM_MOE
    tile_rows = jnp.sum(jnp.where(in_group, jnp.clip(left, 0, TM_MOE), 0), axis=1)
    last_tile = jnp.concatenate([jnp.where(tiles_e > 0, tile_end - 1, -1), n_valid.reshape(1)])
    return (dest.astype(jnp.int32), tile_expert.astype(jnp.int32), n_valid.reshape(1).astype(jnp.int32),
            tile_rows.astype(jnp.int32), last_tile.astype(jnp.int32), n_tiles * TM_MOE)


def _moe_ffn(h2, x1, idx, p, mod, ln_g, ln_b, l, w_gate, w_up, w_down, expert0):
    dest, tile_expert, n_valid, tile_rows, last_tile, n_rows = _moe_plan(idx)
    x_rows = _dispatch(h2, dest, last_tile, n_rows)
    y_rows = _grouped_ffn(x_rows, tile_expert + expert0, n_valid, tile_rows, w_gate, w_up, w_down, 2, TM_MOE)
    return _combine_norm(x1, p, y_rows, dest, mod, ln_g, ln_b, l)


def _dense_ffn(h2, x1, mod, ln_g, ln_b, l, w_gate, w_up, w_down, index):
    nt = T // TM_FFN
    return _grouped_ffn(h2, jnp.full((nt,), index, jnp.int32), jnp.full((1,), nt, jnp.int32),
                        jnp.full((nt,), TM_FFN, jnp.int32), w_gate, w_up, w_down, 2, TM_FFN,
                        norm_args=(x1, mod, ln_g, ln_b, l))


def kernel(x_prompt, x_sample, c, cache_k, cache_v, state_hgrn, c_ctx, w_mod, b_mod, w_in, w_fourier, lb_logits, hgrn_norm, attn_sink, w_out, ln1_g, ln1_b, ln2_g, ln2_b, ffn_w_gate, ffn_w_up, ffn_w_down, router_w, router_b, moe_w_gate, moe_w_up, moe_w_down):
    lb_sm = jax.nn.softmax(lb_logits.astype(F32), axis=0)
    lower_bounds = jnp.clip(jnp.cumsum(lb_sm, axis=0) - lb_sm[0], 0.0, 1.0).reshape(DEPTH, 2, D_HGRN)
    gw = jnp.tile(hgrn_norm, (1, N_HGRN_HEADS)).reshape(DEPTH, 1, D_HGRN)
    per_layer = lambda v: v.reshape(DEPTH, 1, D_MODEL)
    ln1_g, ln1_b, ln2_g, ln2_b = per_layer(ln1_g), per_layer(ln1_b), per_layer(ln2_g), per_layer(ln2_b)
    n_moe = moe_w_gate.shape[0]
    router_wp = jnp.zeros((n_moe, D_MODEL, LANES), F32).at[:, :, :N_EXPERTS].set(router_w)
    router_bp = jnp.full((n_moe, 1, LANES), NEG_BIG, F32).at[:, 0, :N_EXPERTS].set(router_b)
    s0_lat = _states_to_kernel_layout(state_hgrn)
    s0_ctx = jnp.zeros((BATCH, 1, 2, HGRN_HEAD, D_HGRN), F32)

    cond = jnp.zeros((COND_PAD, D_MODEL), F32).at[0].set(c_ctx).at[1:N_COND].set(c)
    mod = _modulation(cond, w_mod, b_mod).reshape(DEPTH, COND_PAD, N_MOD, D_MODEL)

    ch_tabs = _channel_tables()
    pos_tabs_ctx = _dft_tables(SEQ)
    pos_tabs_lat = _dft_tables(DEC_SEQ)
    cos_t, sin_t = _rope_tables()
    band_bias = _band_bias()
    kc = cache_k.reshape(DEC_BATCH, DEPTH, PAST_LEN, D_KV)
    vc = cache_v.reshape(DEC_BATCH, DEPTH, PAST_LEN, D_KV)
    moe_wg = moe_w_gate.reshape(n_moe * N_EXPERTS, D_MODEL, D_FF_EXPERT)
    moe_wu = moe_w_up.reshape(n_moe * N_EXPERTS, D_MODEL, D_FF_EXPERT)
    moe_wd = moe_w_down.reshape(n_moe * N_EXPERTS, D_FF_EXPERT, D_MODEL)

    x = (x_prompt.reshape(TP, D_MODEL), x_sample.reshape(TS, D_MODEL))
    new_kv, new_s = [], []
    for l in range(DEPTH):
        y_in = _in_projection(x, mod, w_in, l)

        a_ctx = _fourier_mix(y_in, 0, BATCH, SEQ, ch_tabs, pos_tabs_ctx, w_fourier, l)
        a_lat = _fourier_mix(y_in, TP, DEC_BATCH, DEC_SEQ, ch_tabs, pos_tabs_lat, w_fourier, l)
        b_ctx, s_ctx = _hgrn_mix(y_in, 0, BATCH, SEQ, lower_bounds, gw, s0_ctx, l, 0)
        b_lat, _ = _hgrn_mix(y_in, TP, DEC_BATCH, DEC_SEQ, lower_bounds, gw, s0_lat, l, l)
        c_ctx_out = _context_attention(y_in, attn_sink, l)
        c_lat = _latent_attention(y_in, attn_sink, kc, vc, l, cos_t, sin_t, band_bias)

        k0 = 6 * 256 + D_ATTN
        new_kv.append(y_in[:TP, k0:k0 + 2 * D_KV])
        new_s.append(s_ctx)

        mixed = ((a_ctx, a_lat), (b_ctx, b_lat), (c_ctx_out, c_lat))
        i = l // 2
        if l % 2 == 0:
            x1, h2 = _out_projection(mixed, x, mod, ln1_g, ln1_b, w_out, l)
            x = _dense_ffn(h2, x1, mod, ln2_g, ln2_b, l, ffn_w_gate, ffn_w_up, ffn_w_down, i)
        else:
            x1, h2, idx, p = _out_projection(mixed, x, mod, ln1_g, ln1_b, w_out, l,
                                             router=(router_wp, router_bp, i))
            x = _moe_ffn(h2, x1, idx, p, mod, ln2_g, ln2_b, l, moe_wg, moe_wu, moe_wd, i * N_EXPERTS)

    xp = x[:TP].reshape(BATCH, SEQ, D_MODEL)
    xs = x[TP:].reshape(DEC_BATCH, DEC_SEQ, D_MODEL)
    kv = jnp.stack(new_kv, axis=0).reshape(DEPTH, BATCH, SEQ, 2, N_KV_HEADS, HEAD_DIM)
    kv = jnp.transpose(kv, (3, 1, 0, 2, 4, 5))
    states = _states_from_kernel_layout(jnp.stack(new_s, axis=1))
    return (xp, xs, kv[0], kv[1], states)
```

```python
import functools
import math

import jax
import jax.numpy as jnp
import numpy as np
from jax import lax
from jax.experimental import pallas as pl
from jax.experimental.pallas import tpu as pltpu

D_MODEL = 1024
BATCH = 16
SEQ = 256
DEPTH = 4
DEC_BATCH = 2
DEC_SEQ = 2048
PAST_LEN = 512
GRID_W = 64
D_FOURIER = 256
N_FOURIER_GROUPS = 4
FOURIER_GROUP = D_FOURIER // N_FOURIER_GROUPS
D_HGRN = 256
N_HGRN_HEADS = 4
HGRN_HEAD = D_HGRN // N_HGRN_HEADS
HGRN_CHUNK = 32
HGRN_GROUP = 128
HGRN_STATE_BLOCK = 8
HGRN_SAFE_DECAY = 120.0
N_Q_HEADS = 8
N_KV_HEADS = 2
GQA = N_Q_HEADS // N_KV_HEADS
HEAD_DIM = 64
D_ATTN = N_Q_HEADS * HEAD_DIM
D_KV = N_KV_HEADS * HEAD_DIM
D_MIX = D_FOURIER + D_HGRN + D_ATTN
WINDOW = 128
ATTN_BLOCK = 128
ATTN_SCALE = HEAD_DIM ** -0.5
LATENT_HEAD_STACK = 4
ROPE_BASE = 10000.0
NEG_BIG = -1e30
D_FF = 2816
N_EXPERTS = 8
TOP_K = 2
D_FF_EXPERT = 3584
DEEPNORM_ALPHA = (2 * DEPTH) ** 0.25
LN_EPS = 1e-5
RMS_EPS = 1e-6
N_MOD = 6
D_IN = 6 * 256 + D_ATTN + 2 * D_KV

TP = BATCH * SEQ
TS = DEC_BATCH * DEC_SEQ
T = TP + TS
N_COND = 1 + DEC_BATCH
COND_PAD = 8

TM = 512
TM_FFN = 512
TM_MOE = 1024
FFN_ROW_STEP = 256
TM_LN = 512
MXU_COLS = 256
ROW_CHUNK = 256
LANES = 128
VMEM_LIMIT = 56 * 1024 * 1024

F32 = jnp.float32
BF16 = jnp.bfloat16


def _cparams(sem, vmem=VMEM_LIMIT):
    return pltpu.CompilerParams(dimension_semantics=sem, vmem_limit_bytes=vmem)


def _ln(x):
    mu = jnp.mean(x, axis=-1, keepdims=True)
    xc = x - mu
    var = jnp.mean(xc * xc, axis=-1, keepdims=True)
    return xc * lax.rsqrt(var + LN_EPS)


def _silu(x):
    return x * jax.nn.sigmoid(x)


def _split3(a):
    p0 = a.astype(BF16)
    r1 = a - p0.astype(F32)
    p1 = r1.astype(BF16)
    r2 = r1 - p1.astype(F32)
    return p0, p1, r2.astype(BF16)


def _split2(a):
    hi = a.astype(BF16)
    return hi, (a - hi.astype(F32)).astype(BF16)


def _dot(a, b):
    return jnp.dot(a, b, preferred_element_type=F32)


def _dot_nt(a, b):
    return lax.dot_general(a, b, (((1,), (1,)), ((), ())), preferred_element_type=F32)


def _dot_tn(a, b):
    return lax.dot_general(a, b, (((0,), (0,)), ((), ())), preferred_element_type=F32)


def _cond_of_tile(i, tm):
    n_ctx = TP // tm
    return jnp.where(i < n_ctx, 0, 1 + (i - n_ctx) // (DEC_SEQ // tm))


def _mod_spec(l, tm, tile_of=lambda i, *_: i):
    return pl.BlockSpec((None, None, N_MOD, D_MODEL), lambda *a: (l, _cond_of_tile(tile_of(*a), tm), 0, 0))


def _layer_row_spec(l, width):
    return pl.BlockSpec((None, 1, width), lambda *_: (l, 0, 0))


def _mod_kernel(c_ref, w_ref, b_ref, o_ref):
    a = _silu(c_ref[...]).astype(BF16)
    o_ref[...] = _dot(a, w_ref[...].astype(BF16)) + b_ref[...]


def _modulation(cond, w_mod, b_mod):
    tn = 2048
    n_out = N_MOD * D_MODEL
    return pl.pallas_call(
        _mod_kernel,
        grid=(DEPTH, n_out // tn),
        in_specs=[
            pl.BlockSpec((COND_PAD, D_MODEL), lambda l, j: (0, 0)),
            pl.BlockSpec((None, D_MODEL, tn), lambda l, j: (l, 0, j)),
            pl.BlockSpec((None, 1, tn), lambda l, j: (l, 0, j)),
        ],
        out_specs=pl.BlockSpec((None, COND_PAD, tn), lambda l, j: (l, 0, j)),
        out_shape=jax.ShapeDtypeStruct((DEPTH, COND_PAD, n_out), F32),
        compiler_params=_cparams(("parallel", "parallel")),
        name="modulation",
    )(cond, w_mod, b_mod.reshape(DEPTH, 1, n_out))


def _inproj_kernel(*refs, split):
    if split:
        x0_ref, x1_ref, mod_ref, w_ref, o_ref, wb_ref = refs
        is_ctx = pl.program_id(0) < TP // TM
        load = lambda rows: jnp.where(is_ctx, x0_ref[rows, :], x1_ref[rows, :])
    else:
        x_ref, mod_ref, w_ref, o_ref, wb_ref = refs
        load = lambda rows: x_ref[rows, :]

    @pl.when(pl.program_id(0) == 0)
    def _():
        wb_ref[...] = w_ref[...].astype(BF16)

    for r0 in range(0, TM, ROW_CHUNK):
        rows = slice(r0, r0 + ROW_CHUNK)
        h = _ln(load(rows)) * (1.0 + mod_ref[1:2, :]) + mod_ref[0:1, :]
        o_ref[rows, :] = _dot(h.astype(BF16), wb_ref[...])


def _in_projection(x, mod, w_in, l):
    split = isinstance(x, tuple)
    n_ctx = TP // TM
    if split:
        x_specs = [pl.BlockSpec((TM, D_MODEL), lambda i: (jnp.minimum(i, n_ctx - 1), 0)),
                   pl.BlockSpec((TM, D_MODEL), lambda i: (jnp.maximum(i - n_ctx, 0), 0))]
    else:
        x_specs = [pl.BlockSpec((TM, D_MODEL), lambda i: (i, 0))]
        x = (x,)
    return pl.pallas_call(
        functools.partial(_inproj_kernel, split=split),
        grid=(T // TM,),
        in_specs=x_specs + [
            _mod_spec(l, TM),
            pl.BlockSpec((None, D_MODEL, D_IN), lambda i: (l, 0, 0), pipeline_mode=pl.Buffered(1)),
        ],
        out_specs=pl.BlockSpec((TM, D_IN), lambda i: (i, 0)),
        out_shape=jax.ShapeDtypeStruct((T, D_IN), F32),
        scratch_shapes=[pltpu.VMEM((D_MODEL, D_IN), BF16)],
        compiler_params=_cparams(("arbitrary",)),
        name="in_projection",
    )(*x, mod, w_in)


def _fourier_kernel(u_ref, ch_hi_ref, ch_lo_ref, tab_ref, wf_ref, o_ref, ab_ref, *, scale):
    @pl.when(pl.program_id(1) == 0)
    def _():
        uh, ul = _split2(u_ref[...])
        ch_hi = ch_hi_ref[...]
        ab = _dot(uh, ch_hi) + _dot(uh, ch_lo_ref[...]) + _dot(ul, ch_hi)
        stacked = jnp.concatenate([ab[:, :D_FOURIER], ab[:, D_FOURIER:]], axis=0)
        ab_ref[...] = stacked.astype(BF16)

    z = (_dot(tab_ref[...], ab_ref[...]) * scale).astype(BF16)
    o_ref[...] = _dot(z, wf_ref[...].astype(BF16)).astype(BF16)


def _dft_tables(n):
    blk = 32
    t = jnp.arange(n, dtype=jnp.int32)[None, :]
    ang1 = (((jnp.arange(n // blk, dtype=jnp.int32) * blk)[:, None] * t) % n).astype(F32) * (2.0 * math.pi / n)
    ang0 = ((jnp.arange(blk, dtype=jnp.int32)[:, None] * t) % n).astype(F32) * (2.0 * math.pi / n)
    c1, s1 = jnp.cos(ang1)[:, None, :], jnp.sin(ang1)[:, None, :]
    c0, s0 = jnp.cos(ang0)[None, :, :], jnp.sin(ang0)[None, :, :]
    cos = (c1 * c0 - s1 * s0).reshape(n, n)
    sin = (s1 * c0 + c1 * s0).reshape(n, n)
    return jnp.concatenate([cos, -sin], axis=1).astype(BF16)


def _channel_tables():
    a = jnp.arange(D_FOURIER, dtype=jnp.int32)
    same = (a[:, None] // FOURIER_GROUP) == (a[None, :] // FOURIER_GROUP)
    prod = ((a[:, None] % FOURIER_GROUP) * (a[None, :] % FOURIER_GROUP)) % FOURIER_GROUP
    ang = prod.astype(F32) * (2.0 * math.pi / FOURIER_GROUP)
    c = jnp.where(same, jnp.cos(ang), 0.0)
    s = jnp.where(same, jnp.sin(ang), 0.0)
    return _split2(jnp.concatenate([c, s], axis=1))


def _fourier_mix(y_in, row0, nbatch, length, ch_tabs, pos_tabs, w_fourier, l):
    tr = min(length, 256)
    blk0 = row0 // length
    kern = functools.partial(_fourier_kernel, scale=1.0 / math.sqrt(length * FOURIER_GROUP))
    return pl.pallas_call(
        kern,
        grid=(nbatch, length // tr),
        in_specs=[
            pl.BlockSpec((length, D_FOURIER), lambda b, r: (blk0 + b, 0)),
            pl.BlockSpec((D_FOURIER, 2 * D_FOURIER), lambda b, r: (0, 0)),
            pl.BlockSpec((D_FOURIER, 2 * D_FOURIER), lambda b, r: (0, 0)),
            pl.BlockSpec((tr, 2 * length), lambda b, r: (r, 0)),
            pl.BlockSpec((None, D_FOURIER, D_FOURIER), lambda b, r: (l, 0, 0)),
        ],
        out_specs=pl.BlockSpec((tr, D_FOURIER), lambda b, r: (b * (length // tr) + r, 0)),
        out_shape=jax.ShapeDtypeStruct((nbatch * length, D_FOURIER), BF16),
        scratch_shapes=[pltpu.VMEM((2 * length, D_FOURIER), BF16)],
        compiler_params=_cparams(("parallel", "arbitrary")),
        name=f"fourier_mix_{length}",
    )(y_in, ch_tabs[0], ch_tabs[1], pos_tabs, w_fourier)


def _expand_state(cst):
    full = jnp.concatenate([cst] * N_HGRN_HEADS, axis=0)
    r = lax.broadcasted_iota(jnp.int32, full.shape, 0) // HGRN_HEAD
    c = lax.broadcasted_iota(jnp.int32, full.shape, 1) // HGRN_HEAD
    return jnp.where(r == c, full, jnp.zeros_like(full))


def _compress_state(full):
    lane_head = lax.broadcasted_iota(jnp.int32, (HGRN_HEAD, D_HGRN), 1) // HGRN_HEAD
    out = jnp.zeros((HGRN_HEAD, D_HGRN), full.dtype)
    for h in range(N_HGRN_HEADS):
        out = out + jnp.where(lane_head == h, full[h * HGRN_HEAD:(h + 1) * HGRN_HEAD, :], 0.0)
    return out


def _hgrn_pairwise_scan(d, nc, g_scr, k_scr, q_scr, v_ref, o_scr, st_scr, ones_bd):
    c = HGRN_CHUNK
    last = c - 1 if d == 0 else 0

    def chunk(ci, carry):
        row = lax.broadcasted_iota(jnp.int32, (c, c), 0)
        col = lax.broadcasted_iota(jnp.int32, (c, c), 1)
        tri = jnp.where((col <= row) if d == 0 else (col >= row), 1.0, 0.0).astype(BF16)
        cc = ci if d == 0 else nc - 1 - ci
        rows = pl.ds(pl.multiple_of(cc * c, c), c)
        g = g_scr[rows, :]
        kk = k_scr[rows, :]
        q = q_scr[rows, :]
        v = v_ref[rows, :]
        g0, g1, g2 = _split3(g)
        b = _dot(tri, g0) + _dot(tri, g1) + _dot(tri, g2)
        btot = b[last:last + 1, :]
        st = st_scr[...]
        o_inter = _dot_nt((q * jnp.exp(b)).astype(BF16), _expand_state(st.astype(BF16)))
        s_idx = lax.broadcasted_iota(jnp.int32, (c, c, D_HGRN), 0)
        t_idx = lax.broadcasted_iota(jnp.int32, (c, c, D_HGRN), 1)
        causal = (s_idx <= t_idx) if d == 0 else (s_idx >= t_idx)
        diff = b[None, :, :] - b[:, None, :]
        pair = q[None, :, :] * jnp.exp(jnp.minimum(diff, 0.0)) * kk[:, None, :]
        pair = jnp.where(causal, pair, 0.0).astype(BF16)
        attn = _dot(pair.reshape(c * c, D_HGRN), ones_bd).reshape(c, c, D_HGRN)
        o = o_inter + jnp.sum(attn * v[:, None, :], axis=0)
        if d == 0:
            o_scr[rows, :] = o
        else:
            o_scr[rows, :] = o_scr[rows, :] + o
        kd = kk * jnp.exp(btot - b)
        upd = _dot_tn(v.astype(BF16), kd.astype(BF16))
        st_scr[...] = jnp.exp(btot) * st + _compress_state(upd)
        return carry

    lax.fori_loop(0, nc, chunk, 0)


def _hgrn_factored_scan(d, nc, g_scr, k_scr, q_scr, v_ref, o_scr, st_scr, qd_scr, kd_scr, dec_scr, stb_scr):
    c = HGRN_CHUNK
    grp = HGRN_GROUP
    length = nc * c

    def group(gi, carry):
        rows = pl.ds(pl.multiple_of(gi * grp, grp), grp)
        r = lax.broadcasted_iota(jnp.int32, (grp, grp), 0)
        s = lax.broadcasted_iota(jnp.int32, (grp, grp), 1)
        same_chunk = (r // c) == (s // c)
        ordered = (s <= r) if d == 0 else (s >= r)
        tri = jnp.where(same_chunk & ordered, 1.0, 0.0).astype(BF16)
        ones_chunk = jnp.where(same_chunk, 1.0, 0.0).astype(BF16)
        g0, g1, g2 = _split3(g_scr[rows, :])
        b = _dot(tri, g0) + _dot(tri, g1) + _dot(tri, g2)
        tot = _dot(ones_chunk, g0) + _dot(ones_chunk, g1) + _dot(ones_chunk, g2)
        kk = k_scr[rows, :]
        q = q_scr[rows, :]
        half = 0.5 * tot
        qc = (q * jnp.exp(b - half)).astype(BF16)
        kh = (kk * jnp.exp(half - b)).astype(BF16)
        qd_scr[rows, :] = (q * jnp.exp(b)).astype(BF16)
        kd_scr[rows, :] = (kk * jnp.exp(tot - b)).astype(BF16)
        dec_scr[rows, :] = jnp.exp(tot)
        hs_row = lax.broadcasted_iota(jnp.int32, (N_HGRN_HEADS * grp, D_HGRN), 0) // grp
        hs_col = lax.broadcasted_iota(jnp.int32, (N_HGRN_HEADS * grp, D_HGRN), 1) // HGRN_HEAD
        same_head = hs_row == hs_col
        k_bd = jnp.where(same_head, jnp.concatenate([kh] * N_HGRN_HEADS, axis=0), 0.0)
        v_bd = jnp.where(same_head, jnp.concatenate([v_ref[rows, :].astype(BF16)] * N_HGRN_HEADS, axis=0), 0.0)
        t_idx = lax.broadcasted_iota(jnp.int32, (grp, N_HGRN_HEADS * grp), 0)
        s_idx = lax.broadcasted_iota(jnp.int32, (grp, N_HGRN_HEADS * grp), 1) % grp
        keep = ((t_idx // c) == (s_idx // c)) & ((s_idx <= t_idx) if d == 0 else (s_idx >= t_idx))
        attn = jnp.where(keep, _dot_nt(qc, k_bd), 0.0)
        o_intra = _dot(attn.astype(BF16), v_bd)
        if d == 0:
            o_scr[rows, :] = o_intra
        else:
            o_scr[rows, :] = o_scr[rows, :] + o_intra
        return carry

    lax.fori_loop(0, length // grp, group, 0, unroll=min(4, length // grp))

    def state_step(ci, carry):
        cc = ci if d == 0 else nc - 1 - ci
        r0 = pl.multiple_of(cc * c, c)
        rows = pl.ds(r0, c)
        st = st_scr[...]
        stb_scr[cc] = st.astype(BF16)
        upd = _dot_tn(v_ref[rows, :].astype(BF16), kd_scr[rows, :])
        st_scr[...] = dec_scr[pl.ds(r0, 1), :] * st + _compress_state(upd)
        return carry

    lax.fori_loop(0, nc, state_step, 0, unroll=4)

    nb = HGRN_STATE_BLOCK
    blk_rows = nb * c

    def inter(bi, carry):
        rows = pl.ds(pl.multiple_of(bi * blk_rows, blk_rows), blk_rows)
        row_chunk = lax.broadcasted_iota(jnp.int32, (blk_rows, nb * D_HGRN), 0) // c
        col_chunk = lax.broadcasted_iota(jnp.int32, (blk_rows, nb * D_HGRN), 1) // D_HGRN
        q_bd = jnp.where(row_chunk == col_chunk, jnp.concatenate([qd_scr[rows, :]] * nb, axis=1), 0.0)
        states = jnp.concatenate([_expand_state(stb_scr[bi * nb + j]) for j in range(nb)], axis=1)
        o_scr[rows, :] = o_scr[rows, :] + _dot_nt(q_bd, states)
        return carry

    lax.fori_loop(0, nc // nb, inter, 0)


def _hgrn_kernel(hq_ref, hff_ref, hfb_ref, hi_ref, hg_ref, lb_ref, gw_ref, s0_ref,
                 o_ref, sfin_ref, q_scr, g_scr, k_scr, o_scr, dec_scr, qd_scr, kd_scr, st_scr, stb_scr,
                 *, length):
    c = HGRN_CHUNK
    nc = length // c
    q_scr[...] = _silu(hq_ref[...])
    r256 = lax.broadcasted_iota(jnp.int32, (D_HGRN, D_HGRN), 0) // HGRN_HEAD
    c256 = lax.broadcasted_iota(jnp.int32, (D_HGRN, D_HGRN), 1) // HGRN_HEAD
    ones_bd = jnp.where(r256 == c256, 1.0, 0.0).astype(BF16)

    for d in range(2):
        z_ref = hff_ref if d == 0 else hfb_ref
        one_minus_f = (1.0 - lb_ref[d:d + 1, :]) * jax.nn.sigmoid(-z_ref[...])
        g_scr[...] = jnp.log1p(-one_minus_f)
        k_scr[...] = one_minus_f
        st_scr[...] = s0_ref[d]
        chunk_decay = jnp.sum(g_scr[...].reshape(nc, c, D_HGRN), axis=1)
        safe = jnp.min(chunk_decay) >= -HGRN_SAFE_DECAY
        lax.cond(
            safe,
            functools.partial(_hgrn_factored_scan, d, nc, g_scr, k_scr, q_scr, hi_ref, o_scr, st_scr,
                              qd_scr, kd_scr, dec_scr, stb_scr),
            functools.partial(_hgrn_pairwise_scan, d, nc, g_scr, k_scr, q_scr, hi_ref, o_scr, st_scr, ones_bd))
        sfin_ref[d] = st_scr[...]

    o = o_scr[...]
    s0p, s1p, s2p = _split3(o * o)
    ms = (_dot(s0p, ones_bd) + _dot(s1p, ones_bd) + _dot(s2p, ones_bd)) * (1.0 / HGRN_HEAD)
    y = o * lax.rsqrt(ms + RMS_EPS) * gw_ref[...] * _silu(hg_ref[...])
    o_ref[...] = y.astype(BF16)


def _hgrn_mix(y_in, row0, nbatch, length, lb, gw, s0, l, s0_layer):
    blk0 = row0 // length
    col = lambda j: (lambda b: (blk0 + b, j))
    kern = functools.partial(_hgrn_kernel, length=length)
    return pl.pallas_call(
        kern,
        grid=(nbatch,),
        in_specs=[
            pl.BlockSpec((length, D_HGRN), col(1)),
            pl.BlockSpec((length, D_HGRN), col(2)),
            pl.BlockSpec((length, D_HGRN), col(3)),
            pl.BlockSpec((length, D_HGRN), col(4)),
            pl.BlockSpec((length, D_HGRN), col(5)),
            pl.BlockSpec((None, 2, D_HGRN), lambda b: (l, 0, 0)),
            _layer_row_spec(l, D_HGRN),
            pl.BlockSpec((None, None, 2, HGRN_HEAD, D_HGRN), lambda b: (b, s0_layer, 0, 0, 0)),
        ],
        out_specs=[
            pl.BlockSpec((length, D_HGRN), lambda b: (b, 0)),
            pl.BlockSpec((None, 2, HGRN_HEAD, D_HGRN), lambda b: (b, 0, 0, 0)),
        ],
        out_shape=[
            jax.ShapeDtypeStruct((nbatch * length, D_HGRN), BF16),
            jax.ShapeDtypeStruct((nbatch, 2, HGRN_HEAD, D_HGRN), F32),
        ],
        scratch_shapes=[pltpu.VMEM((length, D_HGRN), F32) for _ in range(5)]
        + [pltpu.VMEM((length, D_HGRN), BF16) for _ in range(2)]
        + [pltpu.VMEM((HGRN_HEAD, D_HGRN), F32), pltpu.VMEM((length // HGRN_CHUNK, HGRN_HEAD, D_HGRN), BF16)],
        compiler_params=_cparams(("parallel",)),
        name=f"hgrn_mix_{length}",
    )(y_in, y_in, y_in, y_in, y_in, lb, gw, s0)


def _states_to_kernel_layout(s):
    return jnp.moveaxis(s, -1, -3).reshape(s.shape[:-3] + (HGRN_HEAD, D_HGRN))


def _states_from_kernel_layout(st):
    return jnp.moveaxis(st.reshape(st.shape[:-1] + (N_HGRN_HEADS, HGRN_HEAD)), -3, -1)


def _group_attention(q_heads, sinks, kv_parts):
    rows = q_heads[0].shape[0]
    qg = jnp.concatenate([(q * ATTN_SCALE).astype(BF16) for q in q_heads], axis=0)
    head = lax.broadcasted_iota(jnp.int32, (len(q_heads) * rows, 1), 0) // rows
    sink = jnp.full(head.shape, sinks[0], F32)
    for i in range(1, len(q_heads)):
        sink = jnp.where(head == i, sinks[i], sink)
    m = sink
    scores = []
    for k, _, bias in kv_parts:
        s = _dot_nt(qg, k)
        if bias is not None:
            s = s + bias
        m = jnp.maximum(m, jnp.max(s, axis=1, keepdims=True))
        scores.append(s)
    den = jnp.exp(sink - m)
    out = None
    for s, (_, v, _) in zip(scores, kv_parts):
        e = jnp.exp(s - m)
        den = den + jnp.sum(e, axis=1, keepdims=True)
        o = _dot(e.astype(BF16), v)
        out = o if out is None else out + o
    out = out * (1.0 / den)
    return [out[i * rows:(i + 1) * rows, :] for i in range(len(q_heads))]


def _ctx_attn_kernel(sink_ref, q_ref, k_ref, v_ref, o_ref, *, layer):
    outs = []
    for g in range(N_KV_HEADS):
        sl = slice(g * HEAD_DIM, (g + 1) * HEAD_DIM)
        heads = range(g * GQA, (g + 1) * GQA)
        outs += _group_attention(
            [q_ref[:, h * HEAD_DIM:(h + 1) * HEAD_DIM] for h in heads],
            [sink_ref[layer, h] for h in heads],
            [(k_ref[:, sl].astype(BF16), v_ref[:, sl].astype(BF16), None)])
    o_ref[...] = jnp.concatenate(outs, axis=1).astype(BF16)


def _context_attention(y_in, sink, l):
    qcol = (6 * 256) // D_ATTN
    kcol = (6 * 256 + D_ATTN) // D_KV
    return pl.pallas_call(
        functools.partial(_ctx_attn_kernel, layer=l),
        grid=(BATCH,),
        in_specs=[
            pl.BlockSpec(memory_space=pltpu.SMEM),
            pl.BlockSpec((SEQ, D_ATTN), lambda b: (b, qcol)),
            pl.BlockSpec((SEQ, D_KV), lambda b: (b, kcol)),
            pl.BlockSpec((SEQ, D_KV), lambda b: (b, kcol + 1)),
        ],
        out_specs=pl.BlockSpec((SEQ, D_ATTN), lambda b: (b, 0)),
        out_shape=jax.ShapeDtypeStruct((TP, D_ATTN), BF16),
        compiler_params=_cparams(("parallel",)),
        name="context_attention",
    )(sink, y_in, y_in, y_in)


def _rope(x, cos, sin):
    lane = lax.broadcasted_iota(jnp.int32, x.shape, 1)
    n_freq = HEAD_DIM // 4
    first = (lane % (2 * n_freq)) < n_freq
    swapped = jnp.where(first, pltpu.roll(x, LANES - n_freq, axis=1), pltpu.roll(x, n_freq, axis=1))
    return x * cos + swapped * sin


def _lat_attn_kernel(sink_ref, q_ref, k_ref, v_ref, kc_ref, vc_ref, cos_ref, sin_ref, bias_ref, o_ref, *, layer):
    j = pl.program_id(1)
    nb = DEC_SEQ // ATTN_BLOCK
    blk = ATTN_BLOCK
    q0 = pl.multiple_of(j * blk, blk)
    cos_q = cos_ref[pl.ds(q0, blk), :]
    sin_q = sin_ref[pl.ds(q0, blk), :]

    starts = [jnp.maximum(j - 1, 0), j, jnp.minimum(j + 1, nb - 1)]
    k_band, v_band = [], []
    for st in starts:
        r0 = pl.multiple_of(st * blk, blk)
        kb = _rope(k_ref[pl.ds(r0, blk), :], cos_ref[pl.ds(r0, blk), :], sin_ref[pl.ds(r0, blk), :])
        k_band.append(kb)
        v_band.append(v_ref[pl.ds(r0, blk), :])
    k_loc = jnp.concatenate(k_band, axis=0)
    v_loc = jnp.concatenate(v_band, axis=0)
    band_bias = bias_ref[...]

    heads_per_chunk = LANES // HEAD_DIM
    q_chunks = [_rope(q_ref[:, cg * LANES:(cg + 1) * LANES], cos_q, sin_q)
                for cg in range(N_Q_HEADS // heads_per_chunk)]

    def q_head(h):
        off = (h % heads_per_chunk) * HEAD_DIM
        return q_chunks[h // heads_per_chunk][:, off:off + HEAD_DIM]

    outs = []
    for h0 in range(0, N_Q_HEADS, LATENT_HEAD_STACK):
        g = h0 // GQA
        sl = slice(g * HEAD_DIM, (g + 1) * HEAD_DIM)
        heads = range(h0, h0 + LATENT_HEAD_STACK)
        outs += _group_attention(
            [q_head(h) for h in heads],
            [sink_ref[layer, h] for h in heads],
            [(k_loc[:, sl].astype(BF16), v_loc[:, sl].astype(BF16), band_bias[:LATENT_HEAD_STACK * blk, :]),
             (kc_ref[:, sl].astype(BF16), vc_ref[:, sl].astype(BF16), None)])
    o_ref[...] = jnp.concatenate(outs, axis=1).astype(BF16)


def _band_bias():
    blk = ATTN_BLOCK
    r = np.arange(GQA * blk)[:, None] % blk
    c = np.arange(3 * blk)[None, :]
    in_window = np.abs(r + blk - c) <= WINDOW
    exists = [c >= blk, c >= 0, c < 2 * blk]
    return jnp.asarray(np.stack([np.where(in_window & e, 0.0, NEG_BIG) for e in exists]).astype(np.float32))


def _latent_attention(y_in, sink, kc, vc, l, cos_t, sin_t, band_bias):
    nb = DEC_SEQ // ATTN_BLOCK
    qrow0 = TP // ATTN_BLOCK
    krow0 = TP // DEC_SEQ
    qcol = (6 * 256) // D_ATTN
    kcol = (6 * 256 + D_ATTN) // D_KV
    return pl.pallas_call(
        functools.partial(_lat_attn_kernel, layer=l),
        grid=(DEC_BATCH, nb),
        in_specs=[
            pl.BlockSpec(memory_space=pltpu.SMEM),
            pl.BlockSpec((ATTN_BLOCK, D_ATTN), lambda b, j: (qrow0 + b * nb + j, qcol)),
            pl.BlockSpec((DEC_SEQ, D_KV), lambda b, j: (krow0 + b, kcol)),
            pl.BlockSpec((DEC_SEQ, D_KV), lambda b, j: (krow0 + b, kcol + 1)),
            pl.BlockSpec((None, None, PAST_LEN, D_KV), lambda b, j: (b, l, 0, 0)),
            pl.BlockSpec((None, None, PAST_LEN, D_KV), lambda b, j: (b, l, 0, 0)),
            pl.BlockSpec((DEC_SEQ, LANES), lambda b, j: (0, 0)),
            pl.BlockSpec((DEC_SEQ, LANES), lambda b, j: (0, 0)),
            pl.BlockSpec((None, GQA * ATTN_BLOCK, 3 * ATTN_BLOCK),
                         lambda b, j: (jnp.where(j == 0, 0, jnp.where(j == nb - 1, 2, 1)), 0, 0)),
        ],
        out_specs=pl.BlockSpec((ATTN_BLOCK, D_ATTN), lambda b, j: (b * nb + j, 0)),
        out_shape=jax.ShapeDtypeStruct((TS, D_ATTN), BF16),
        compiler_params=_cparams(("parallel", "parallel")),
        name="latent_attention",
    )(sink, y_in, y_in, y_in, kc, vc, cos_t, sin_t, band_bias)


def _rope_tables():
    t = jnp.arange(DEC_SEQ)
    rows = (t // GRID_W).astype(F32)
    cols = (t % GRID_W).astype(F32)
    n_freq = HEAD_DIM // 4
    inv = ROPE_BASE ** (-jnp.arange(n_freq, dtype=F32) / n_freq)
    ar = rows[:, None] * inv
    ac = cols[:, None] * inv
    cos = jnp.concatenate([jnp.cos(ar), jnp.cos(ar), jnp.cos(ac), jnp.cos(ac)], axis=1)
    sin = jnp.concatenate([-jnp.sin(ar), jnp.sin(ar), -jnp.sin(ac), jnp.sin(ac)], axis=1)
    return jnp.tile(cos, (1, N_KV_HEADS)), jnp.tile(sin, (1, N_KV_HEADS))


def _top2_route(h2, w_ref, b_ref):
    logits = _dot(h2.astype(BF16), w_ref[...].astype(BF16)) + b_ref[...]
    lane = lax.broadcasted_iota(jnp.int32, logits.shape, 1)
    m1 = jnp.max(logits, axis=1, keepdims=True)
    i1 = jnp.min(jnp.where(logits == m1, lane, LANES), axis=1, keepdims=True)
    rest = jnp.where(lane == i1, -jnp.inf, logits)
    m2 = jnp.max(rest, axis=1, keepdims=True)
    i2 = jnp.min(jnp.where(rest == m2, lane, LANES), axis=1, keepdims=True)
    e = jnp.exp(m2 - m1)
    inv = 1.0 / (1.0 + e)
    idx = jnp.where(lane == 0, i1, jnp.where(lane == 1, i2, 0))
    return idx, jnp.where(lane == 0, inv, jnp.where(lane == 1, e * inv, 0.0))


def _outproj_kernel(a0_ref, a1_ref, b0_ref, b1_ref, c0_ref, c1_ref, *rest, route, split):
    is_ctx = pl.program_id(0) < TP // TM
    pick = lambda r0, r1, rows: jnp.where(is_ctx, r0[rows, :], r1[rows, :])
    if split:
        load_x = functools.partial(pick, rest[0], rest[1])
        rest = rest[2:]
    else:
        load_x = lambda rows, x_ref=rest[0]: x_ref[rows, :]
        rest = rest[1:]
    mod_ref, g_ref, beta_ref, w_ref = rest[:4]
    rest = rest[4:]
    if route:
        wr_ref, br_ref, x1_ref, h2_ref, idx_ref, p_ref, wb_ref = rest
    else:
        x1_ref, h2_ref, wb_ref = rest

    @pl.when(pl.program_id(0) == 0)
    def _():
        wb_ref[...] = w_ref[...].astype(BF16)

    for r0 in range(0, TM, ROW_CHUNK):
        rows = slice(r0, r0 + ROW_CHUNK)
        y = (_dot(pick(a0_ref, a1_ref, rows), wb_ref[0:D_FOURIER, :])
             + _dot(pick(b0_ref, b1_ref, rows), wb_ref[D_FOURIER:D_FOURIER + D_HGRN, :])
             + _dot(pick(c0_ref, c1_ref, rows), wb_ref[D_FOURIER + D_HGRN:D_MIX, :]))
        x1 = _ln(DEEPNORM_ALPHA * load_x(rows) + mod_ref[2:3, :] * y) * g_ref[...] + beta_ref[...]
        x1_ref[rows, :] = x1
        h2 = _ln(x1) * (1.0 + mod_ref[4:5, :]) + mod_ref[3:4, :]
        h2_ref[rows, :] = h2
        if route:
            idx_ref[rows, :], p_ref[rows, :] = _top2_route(h2, wr_ref, br_ref)


def _out_projection(mixed, x, mod, ln_g, ln_b, w_out, l, router=None):
    n_ctx = TP // TM
    row = lambda w: pl.BlockSpec((TM, w), lambda i: (i, 0))
    ctx = lambda w: pl.BlockSpec((TM, w), lambda i: (jnp.minimum(i, n_ctx - 1), 0))
    lat = lambda w: pl.BlockSpec((TM, w), lambda i: (jnp.maximum(i - n_ctx, 0), 0))
    vec = _layer_row_spec(l, D_MODEL)
    split = isinstance(x, tuple)
    x_specs = [ctx(D_MODEL), lat(D_MODEL)] if split else [row(D_MODEL)]
    x = x if split else (x,)
    in_specs = [
        ctx(D_FOURIER), lat(D_FOURIER), ctx(D_HGRN), lat(D_HGRN), ctx(D_ATTN), lat(D_ATTN), *x_specs,
        _mod_spec(l, TM),
        vec, vec,
        pl.BlockSpec((None, D_MIX, D_MODEL), lambda i: (l, 0, 0), pipeline_mode=pl.Buffered(1)),
    ]
    args = [mixed[0][0], mixed[0][1], mixed[1][0], mixed[1][1], mixed[2][0], mixed[2][1], *x, mod, ln_g, ln_b,
            w_out]
    out_specs = [row(D_MODEL), row(D_MODEL)]
    out_shape = [jax.ShapeDtypeStruct((T, D_MODEL), F32), jax.ShapeDtypeStruct((T, D_MODEL), F32)]
    if router is not None:
        w, b, ri = router
        in_specs += [pl.BlockSpec((None, D_MODEL, LANES), lambda i: (ri, 0, 0)), _layer_row_spec(ri, LANES)]
        args += [w, b]
        out_specs += [row(LANES), row(LANES)]
        out_shape += [jax.ShapeDtypeStruct((T, LANES), jnp.int32), jax.ShapeDtypeStruct((T, LANES), F32)]
    return pl.pallas_call(
        functools.partial(_outproj_kernel, route=router is not None, split=split),
        grid=(T // TM,),
        in_specs=in_specs,
        out_specs=out_specs,
        out_shape=out_shape,
        scratch_shapes=[pltpu.VMEM((D_MIX, D_MODEL), BF16)],
        compiler_params=_cparams(("arbitrary",)),
        name="out_projection_route" if router is not None else "out_projection",
    )(*args)


def _expert_changed(te_ref, i):
    return (i == 0) | (te_ref[i] != te_ref[jnp.maximum(i - 1, 0)])


def _post_ffn(x1, y, g2, ln_g, ln_b):
    return _ln(DEEPNORM_ALPHA * x1 + g2 * y) * ln_g + ln_b


def _for_used_rows(rows, out_ref, compute):
    tm = out_ref.shape[0]
    for n in range(FFN_ROW_STEP, tm + 1, FFN_ROW_STEP):
        @pl.when((rows > n - FFN_ROW_STEP) & (rows <= n))
        def _(n=n):
            compute(n)
            if n < tm:
                out_ref[n:, :] = jnp.zeros((tm - n, out_ref.shape[1]), out_ref.dtype)

    @pl.when(rows == 0)
    def _():
        out_ref[...] = jnp.zeros_like(out_ref)


def _cast_columns(w_refs, c0, c1):
    parts, base = [], 0
    for w_ref in w_refs:
        width = w_ref.shape[-1]
        lo, hi = max(c0 - base, 0), min(c1 - base, width)
        if lo < hi:
            parts.append(w_ref[:, lo:hi].astype(BF16))
        base += width
    return parts[0] if len(parts) == 1 else jnp.concatenate(parts, axis=1)


def _dot_streamed(x, w_refs):
    total = sum(w_ref.shape[-1] for w_ref in w_refs)
    outs = [_dot(x, _cast_columns(w_refs, c0, min(c0 + MXU_COLS, total))) for c0 in range(0, total, MXU_COLS)]
    return outs[0] if len(outs) == 1 else jnp.concatenate(outs, axis=1)


def _ffn_up_kernel(te_ref, nv_ref, tr_ref, x_ref, wg_ref, wu_ref, h_ref, *scratch, stream):
    i = pl.program_id(1)
    fh = wg_ref.shape[-1]
    if not stream:
        wb_ref, = scratch

        @pl.when(_expert_changed(te_ref, i))
        def _():
            wb_ref[:, :fh] = wg_ref[...].astype(BF16)
            wb_ref[:, fh:] = wu_ref[...].astype(BF16)

    def compute(n):
        x = x_ref[0:n, :].astype(BF16)
        if stream:
            for c0 in range(0, fh, MXU_COLS):
                cols = slice(c0, min(c0 + MXU_COLS, fh))
                a = _dot(x, wg_ref[:, cols].astype(BF16))
                b = _dot(x, wu_ref[:, cols].astype(BF16))
                h_ref[0:n, cols] = (_silu(a) * b).astype(BF16)
        else:
            ab = _dot(x, wb_ref[...])
            h_ref[0:n, :] = (_silu(ab[:, :fh]) * ab[:, fh:]).astype(BF16)

    _for_used_rows(tr_ref[i], h_ref, compute)


def _ffn_down_kernel(te_ref, nv_ref, tr_ref, h_ref, wd_ref, *rest, norm):
    i = pl.program_id(1)
    if norm:
        x1_ref, mod_ref, g_ref, beta_ref, y_ref, wdb_ref = rest

        @pl.when(_expert_changed(te_ref, i))
        def _():
            wdb_ref[...] = wd_ref[...].astype(BF16)

        for r0 in range(0, y_ref.shape[0], ROW_CHUNK):
            rows = slice(r0, r0 + ROW_CHUNK)
            y = _dot(h_ref[rows, :], wdb_ref[...])
            y_ref[rows, :] = _post_ffn(x1_ref[rows, :], y, mod_ref[5:6, :], g_ref[...], beta_ref[...])
    else:
        y_ref, = rest

        def compute(n):
            y_ref[0:n, :] = _dot_streamed(h_ref[0:n, :], (wd_ref,))

        _for_used_rows(tr_ref[i], y_ref, compute)


def _grouped_ffn(x_rows, tile_expert, n_valid, tile_rows, w_gate, w_up, w_down, f_splits, tm, norm_args=None):
    r = x_rows.shape[0]
    nt = r // tm
    f = w_gate.shape[-1]
    fh = f // f_splits
    assert fh * f_splits == f and fh % LANES == 0
    used = lambda i, nv: jnp.minimum(i, nv[0] - 1)
    stream = norm_args is None
    h = pl.pallas_call(
        functools.partial(_ffn_up_kernel, stream=stream),
        grid_spec=pltpu.PrefetchScalarGridSpec(
            num_scalar_prefetch=3,
            grid=(f_splits, nt),
            in_specs=[
                pl.BlockSpec((tm, D_MODEL), lambda j, i, te, nv, tr: (used(i, nv), 0)),
                pl.BlockSpec((None, D_MODEL, fh), lambda j, i, te, nv, tr: (te[i], 0, j)),
                pl.BlockSpec((None, D_MODEL, fh), lambda j, i, te, nv, tr: (te[i], 0, j)),
            ],
            out_specs=pl.BlockSpec((tm, fh), lambda j, i, te, nv, tr: (i, j)),
            scratch_shapes=[] if stream else [pltpu.VMEM((D_MODEL, 2 * fh), BF16)],
        ),
        out_shape=jax.ShapeDtypeStruct((r, f), BF16),
        compiler_params=_cparams(("arbitrary", "arbitrary")),
        name="ffn_up",
    )(tile_expert, n_valid, tile_rows, x_rows, w_gate, w_up)

    d_splits = 1 if norm_args is not None else 2
    dh = D_MODEL // d_splits
    in_specs = [
        pl.BlockSpec((tm, f), lambda j, i, te, nv, tr: (used(i, nv), 0)),
        pl.BlockSpec((None, f, dh), lambda j, i, te, nv, tr: (te[i], 0, j)),
    ]
    args = [tile_expert, n_valid, tile_rows, h, w_down]
    if norm_args is not None:
        assert r == T
        x1, mod, ln_g, ln_b, l = norm_args
        vec = _layer_row_spec(l, D_MODEL)
        in_specs += [
            pl.BlockSpec((tm, D_MODEL), lambda j, i, te, nv, tr: (i, 0)),
            _mod_spec(l, tm, lambda j, i, *_: i),
            vec, vec]
        args += [x1, mod, ln_g, ln_b]
    return pl.pallas_call(
        functools.partial(_ffn_down_kernel, norm=norm_args is not None),
        grid_spec=pltpu.PrefetchScalarGridSpec(
            num_scalar_prefetch=3,
            grid=(d_splits, nt),
            in_specs=in_specs,
            out_specs=pl.BlockSpec((tm, dh), lambda j, i, te, nv, tr: (i, j)),
            scratch_shapes=[] if stream else [pltpu.VMEM((f, dh), BF16)],
        ),
        out_shape=jax.ShapeDtypeStruct((r, D_MODEL), F32),
        compiler_params=_cparams(("arbitrary", "arbitrary")),
        name="ffn_down_norm" if norm_args is not None else "ffn_down",
    )(*args)


def _row_copy(src, dst, s, d, sem):
    return pltpu.make_async_copy(src.at[pl.ds(s, 1), :], dst.at[pl.ds(d, 1), :], sem)


def _dispatch_kernel(dest_ref, last_ref, x_ref, o_hbm, zero_scr, sem, zsem):
    i = pl.program_id(0)

    @pl.when(i == 0)
    def _():
        zero_scr[...] = jnp.zeros_like(zero_scr)

        def fill(tile):
            r0 = pl.multiple_of(tile * TM_MOE, TM_MOE)
            return pltpu.make_async_copy(zero_scr, o_hbm.at[pl.ds(r0, TM_MOE), :], zsem)

        n_tiles = o_hbm.shape[0] // TM_MOE
        min_tiles = (T * TOP_K) // TM_MOE
        jobs = [(last_ref[e] >= 0, last_ref[e]) for e in range(N_EXPERTS)]
        jobs += [(t >= last_ref[N_EXPERTS], t) for t in range(min_tiles, n_tiles)]
        for go, tile in jobs:
            @pl.when(go)
            def _(tile=tile):
                fill(tile).start()
        for go, tile in jobs:
            @pl.when(go)
            def _(tile=tile):
                fill(tile).wait()

    base = i * TM_LN

    def issue(r, carry):
        for k in range(TOP_K):
            _row_copy(x_ref, o_hbm, r, dest_ref[(base + r) * TOP_K + k], sem).start(priority=k % 2)
        return carry

    lax.fori_loop(0, TM_LN, issue, 0, unroll=8)
    for k in range(TOP_K):
        pltpu.make_async_copy(x_ref, o_hbm.at[pl.ds(0, TM_LN), :], sem).wait()


def _dispatch(h2, dest, last_tile, n_rows):
    return pl.pallas_call(
        _dispatch_kernel,
        grid_spec=pltpu.PrefetchScalarGridSpec(
            num_scalar_prefetch=2,
            grid=(T // TM_LN,),
            in_specs=[pl.BlockSpec((TM_LN, D_MODEL), lambda i, d, l: (i, 0))],
            out_specs=pl.BlockSpec(memory_space=pl.ANY),
            scratch_shapes=[pltpu.VMEM((TM_MOE, D_MODEL), F32), pltpu.SemaphoreType.DMA(()),
                            pltpu.SemaphoreType.DMA(())],
        ),
        out_shape=jax.ShapeDtypeStruct((n_rows, D_MODEL), F32),
        compiler_params=_cparams(("arbitrary",)),
        name="dispatch",
    )(dest, last_tile, h2)


def _combine_kernel(pos_ref, x1_ref, p_ref, y_hbm, mod_ref, g_ref, beta_ref, o_ref, buf, sem):
    i = pl.program_id(0)
    n = pl.num_programs(0)

    def fetch(step, slot):
        def issue(r, carry):
            for k in range(TOP_K):
                src = pos_ref[(step * TM_LN + r) * TOP_K + k]
                _row_copy(y_hbm, buf.at[slot, k], src, r, sem.at[slot, k]).start(priority=k % 2)
            return carry

        lax.fori_loop(0, TM_LN, issue, 0, unroll=8)

    @pl.when(i == 0)
    def _():
        fetch(0, 0)

    slot = i % 2

    @pl.when(i + 1 < n)
    def _():
        fetch(i + 1, 1 - slot)

    for k in range(TOP_K):
        pltpu.make_async_copy(y_hbm.at[pl.ds(0, TM_LN), :], buf.at[slot, k], sem.at[slot, k]).wait()
    y = p_ref[:, 0:1] * buf[slot, 0]
    for k in range(1, TOP_K):
        y = y + p_ref[:, k:k + 1] * buf[slot, k]
    o_ref[...] = _post_ffn(x1_ref[...], y, mod_ref[5:6, :], g_ref[...], beta_ref[...])


def _combine_norm(x1, p, y_rows, pos, mod, ln_g, ln_b, l):
    row = pl.BlockSpec((TM_LN, D_MODEL), lambda i, s: (i, 0))
    vec = _layer_row_spec(l, D_MODEL)
    return pl.pallas_call(
        _combine_kernel,
        grid_spec=pltpu.PrefetchScalarGridSpec(
            num_scalar_prefetch=1,
            grid=(T // TM_LN,),
            in_specs=[row, pl.BlockSpec((TM_LN, LANES), lambda i, s: (i, 0)),
                      pl.BlockSpec(memory_space=pl.ANY),
                      _mod_spec(l, TM_LN),
                      vec, vec],
            out_specs=row,
            scratch_shapes=[pltpu.VMEM((2, TOP_K, TM_LN, D_MODEL), F32),
                            pltpu.SemaphoreType.DMA((2, TOP_K))],
        ),
        out_shape=jax.ShapeDtypeStruct((T, D_MODEL), F32),
        compiler_params=_cparams(("arbitrary",)),
        name="combine_norm",
    )(pos, x1, p, y_rows, mod, ln_g, ln_b)


def _moe_plan(idx):
    n_assign = T * TOP_K
    n_tiles = n_assign // TM_MOE + N_EXPERTS
    e = idx[:, :TOP_K].reshape(n_assign)
    onehot = (e[:, None] == jnp.arange(N_EXPERTS, dtype=jnp.int32)[None, :]).astype(jnp.int32)
    csum = jnp.cumsum(onehot, axis=0)
    counts = csum[-1]
    tiles_e = (counts + TM_MOE - 1) // TM_MOE
    tile_end = jnp.cumsum(tiles_e)
    row0 = (tile_end - tiles_e) * TM_MOE
    dest = jnp.sum((csum - 1 + row0[None, :]) * onehot, axis=1)
    n_valid = tile_end[-1]
    tiles = jnp.arange(n_tiles, dtype=jnp.int32)
    tile_id = jnp.minimum(tiles, n_valid - 1)
    tile_expert = jnp.sum((tile_id[:, None] >= tile_end[None, :]).astype(jnp.int32), axis=1)
    tile_start = tile_end - tiles_e
    in_group = (tiles[:, None] >= tile_start[None, :]) & (tiles[:, None] < tile_end[None, :])
    left = counts[None, :] - (tiles[:, None] - tile_start[None, :]) * TM_MOE
    tile_rows = jnp.sum(jnp.where(in_group, jnp.clip(left, 0, TM_MOE), 0), axis=1)
    last_tile = jnp.concatenate([jnp.where(tiles_e > 0, tile_end - 1, -1), n_valid.reshape(1)])
    return (dest.astype(jnp.int32), tile_expert.astype(jnp.int32), n_valid.reshape(1).astype(jnp.int32),
            tile_rows.astype(jnp.int32), last_tile.astype(jnp.int32), n_tiles * TM_MOE)


def _moe_ffn(h2, x1, idx, p, mod, ln_g, ln_b, l, w_gate, w_up, w_down, expert0):
    dest, tile_expert, n_valid, tile_rows, last_tile, n_rows = _moe_plan(idx)
    x_rows = _dispatch(h2, dest, last_tile, n_rows)
    y_rows = _grouped_ffn(x_rows, tile_expert + expert0, n_valid, tile_rows, w_gate, w_up, w_down, 2, TM_MOE)
    return _combine_norm(x1, p, y_rows, dest, mod, ln_g, ln_b, l)


def _dense_ffn(h2, x1, mod, ln_g, ln_b, l, w_gate, w_up, w_down, index):
    nt = T // TM_FFN
    return _grouped_ffn(h2, jnp.full((nt,), index, jnp.int32), jnp.full((1,), nt, jnp.int32),
                        jnp.full((nt,), TM_FFN, jnp.int32), w_gate, w_up, w_down, 2, TM_FFN,
                        norm_args=(x1, mod, ln_g, ln_b, l))


def kernel(x_prompt, x_sample, c, cache_k, cache_v, state_hgrn, c_ctx, w_mod, b_mod, w_in, w_fourier, lb_logits, hgrn_norm, attn_sink, w_out, ln1_g, ln1_b, ln2_g, ln2_b, ffn_w_gate, ffn_w_up, ffn_w_down, router_w, router_b, moe_w_gate, moe_w_up, moe_w_down):
    lb_sm = jax.nn.softmax(lb_logits.astype(F32), axis=0)
    lower_bounds = jnp.clip(jnp.cumsum(lb_sm, axis=0) - lb_sm[0], 0.0, 1.0).reshape(DEPTH, 2, D_HGRN)
    gw = jnp.tile(hgrn_norm, (1, N_HGRN_HEADS)).reshape(DEPTH, 1, D_HGRN)
    per_layer = lambda v: v.reshape(DEPTH, 1, D_MODEL)
    ln1_g, ln1_b, ln2_g, ln2_b = per_layer(ln1_g), per_layer(ln1_b), per_layer(ln2_g), per_layer(ln2_b)
    n_moe = moe_w_gate.shape[0]
    router_wp = jnp.zeros((n_moe, D_MODEL, LANES), F32).at[:, :, :N_EXPERTS].set(router_w)
    router_bp = jnp.full((n_moe, 1, LANES), NEG_BIG, F32).at[:, 0, :N_EXPERTS].set(router_b)
    s0_lat = _states_to_kernel_layout(state_hgrn)
    s0_ctx = jnp.zeros((BATCH, 1, 2, HGRN_HEAD, D_HGRN), F32)

    cond = jnp.zeros((COND_PAD, D_MODEL), F32).at[0].set(c_ctx).at[1:N_COND].set(c)
    mod = _modulation(cond, w_mod, b_mod).reshape(DEPTH, COND_PAD, N_MOD, D_MODEL)

    ch_tabs = _channel_tables()
    pos_tabs_ctx = _dft_tables(SEQ)
    pos_tabs_lat = _dft_tables(DEC_SEQ)
    cos_t, sin_t = _rope_tables()
    band_bias = _band_bias()
    kc = cache_k.reshape(DEC_BATCH, DEPTH, PAST_LEN, D_KV)
    vc = cache_v.reshape(DEC_BATCH, DEPTH, PAST_LEN, D_KV)
    moe_wg = moe_w_gate.reshape(n_moe * N_EXPERTS, D_MODEL, D_FF_EXPERT)
    moe_wu = moe_w_up.reshape(n_moe * N_EXPERTS, D_MODEL, D_FF_EXPERT)
    moe_wd = moe_w_down.reshape(n_moe * N_EXPERTS, D_FF_EXPERT, D_MODEL)

    x = (x_prompt.reshape(TP, D_MODEL), x_sample.reshape(TS, D_MODEL))
    new_kv, new_s = [], []
    for l in range(DEPTH):
        y_in = _in_projection(x, mod, w_in, l)

        a_ctx = _fourier_mix(y_in, 0, BATCH, SEQ, ch_tabs, pos_tabs_ctx, w_fourier, l)
        a_lat = _fourier_mix(y_in, TP, DEC_BATCH, DEC_SEQ, ch_tabs, pos_tabs_lat, w_fourier, l)
        b_ctx, s_ctx = _hgrn_mix(y_in, 0, BATCH, SEQ, lower_bounds, gw, s0_ctx, l, 0)
        b_lat, _ = _hgrn_mix(y_in, TP, DEC_BATCH, DEC_SEQ, lower_bounds, gw, s0_lat, l, l)
        c_ctx_out = _context_attention(y_in, attn_sink, l)
        c_lat = _latent_attention(y_in, attn_sink, kc, vc, l, cos_t, sin_t, band_bias)

        k0 = 6 * 256 + D_ATTN
        new_kv.append(y_in[:TP, k0:k0 + 2 * D_KV])
        new_s.append(s_ctx)

        mixed = ((a_ctx, a_lat), (b_ctx, b_lat), (c_ctx_out, c_lat))
        i = l // 2
        if l % 2 == 0:
            x1, h2 = _out_projection(mixed, x, mod, ln1_g, ln1_b, w_out, l)
            x = _dense_ffn(h2, x1, mod, ln2_g, ln2_b, l, ffn_w_gate, ffn_w_up, ffn_w_down, i)
        else:
            x1, h2, idx, p = _out_projection(mixed, x, mod, ln1_g, ln1_b, w_out, l,
                                             router=(router_wp, router_bp, i))
            x = _moe_ffn(h2, x1, idx, p, mod, ln2_g, ln2_b, l, moe_wg, moe_wu, moe_wd, i * N_EXPERTS)

    xp = x[:TP].reshape(BATCH, SEQ, D_MODEL)
    xs = x[TP:].reshape(DEC_BATCH, DEC_SEQ, D_MODEL)
    kv = jnp.stack(new_kv, axis=0).reshape(DEPTH, BATCH, SEQ, 2, N_KV_HEADS, HEAD_DIM)
    kv = jnp.transpose(kv, (3, 1, 0, 2, 4, 5))
    states = _states_from_kernel_layout(jnp.stack(new_s, axis=1))
    return (xp, xs, kv[0], kv[1], states)
```
